```python
import math
import jax
import jax.numpy as jnp
from jax import lax
import numpy as np

D_MODEL = 1024
BATCH = 8
SEQ = 4096
DEPTH = 2

RET_HEADS = 4
RET_DK = 128
RET_DV = 256
RET_CHUNK = 128
ROPE_BASE = 10000.0
GLA_HEADS = 4
GLA_DK = 128
GLA_DV = 256
GLA_RANK = 16
GLA_TAU = 16.0
GLA_CHUNK = 64
SSD_DINNER = 2 * D_MODEL
SSD_HEADDIM = 64
SSD_HEADS = SSD_DINNER // SSD_HEADDIM
SSD_GROUPS = 8
SSD_STATE = 128
SSD_CONV = 4
SSD_CHUNK = 128
N_EXPERTS = 32
N_GROUPS = 8
EXPERTS_PER_GROUP = N_EXPERTS // N_GROUPS
TOP_K = 2
GROUP_SCORE_K = 2
D_FF_EXPERT = 768
MOE_BLOCK = 256
NORM_EPS = 1e-6

RET_QK = RET_HEADS * RET_DK
RET_V = RET_HEADS * RET_DV
GLA_QK = GLA_HEADS * GLA_DK
GLA_V = GLA_HEADS * GLA_DV
EVEN_IN = 2 * RET_QK + 2 * RET_V + 2 * GLA_QK + 2 * GLA_V + GLA_RANK
EVEN_MIX = RET_V + GLA_V
SSD_CONV_CH = SSD_DINNER + 2 * SSD_GROUPS * SSD_STATE
ODD_IN = SSD_DINNER + SSD_CONV_CH + SSD_HEADS
N_EVEN = (DEPTH + 1) // 2
N_ODD = DEPTH // 2

kernel_name = 'hybrid_ret_gla_ssd_moe'


def rmsnorm(x, gain):
    xf = x.astype(jnp.float32)
    y = xf * lax.rsqrt(jnp.mean(xf * xf, axis=-1, keepdims=True) + NORM_EPS)
    return (y * gain.astype(jnp.float32)).astype(x.dtype)


def rotary(x, pos):
    half = x.shape[-1] // 2
    inv = ROPE_BASE ** (-jnp.arange(half, dtype=jnp.float32) / half)
    ang = pos.astype(jnp.float32)[..., None] * inv
    cos = jnp.cos(ang)[:, :, None, :]
    sin = jnp.sin(ang)[:, :, None, :]
    x1, x2 = x[..., :half], x[..., half:]
    return jnp.concatenate([x1 * cos - x2 * sin, x2 * cos + x1 * sin], axis=-1)


def retention_chunked(q, k, v, log_gamma):
    B, S, H, dk = q.shape
    dv = v.shape[-1]
    L = RET_CHUNK
    n = S // L
    q = q.reshape(B, n, L, H, dk)
    k = k.reshape(B, n, L, H, dk)
    v = v.reshape(B, n, L, H, dv)
    idx = jnp.arange(L, dtype=jnp.float32)
    diff = idx[:, None] - idx[None, :]
    decay = jnp.where(diff >= 0, jnp.exp(log_gamma[:, None, None] * jnp.maximum(diff, 0.0)), 0.0)
    scores = jnp.einsum('bnihd,bnjhd->bnhij', q, k) * decay
    inner = jnp.einsum('bnhij,bnjhe->bnihe', scores, v)
    zeta = jnp.exp(log_gamma[None, :] * (L - 1 - idx)[:, None])
    kv = jnp.einsum('bnjhd,bnjhe->bnhde', k * zeta[:, :, None], v)
    chunk_decay = jnp.exp(log_gamma * L)[:, None, None]

    def step(state, kv_c):
        return state * chunk_decay + kv_c, state

    _, prev = lax.scan(step, jnp.zeros((B, H, dk, dv), kv.dtype), jnp.moveaxis(kv, 1, 0))
    prev = jnp.moveaxis(prev, 0, 1)
    xi = jnp.exp(log_gamma[None, :] * (idx + 1.0)[:, None])
    cross = jnp.einsum('bnihd,bnhde->bnihe', q * xi[:, :, None], prev)
    return (inner + cross).reshape(B, S, H, dv)


def gla_chunked(q, k, v, g):
    B, S, H, dk = q.shape
    dv = v.shape[-1]
    L = GLA_CHUNK
    n = S // L
    q = q.reshape(B, n, L, H, dk) * (dk ** -0.5)
    k = k.reshape(B, n, L, H, dk)
    v = v.reshape(B, n, L, H, dv)
    b = jnp.cumsum(g.reshape(B, n, L, H, dk), axis=2)
    ref = b[:, :, L // 2:L // 2 + 1]
    att = jnp.einsum('bnihd,bnjhd->bnhij', q * jnp.exp(b - ref), k * jnp.exp(ref - b))
    causal = jnp.tril(jnp.ones((L, L), dtype=bool))
    att = jnp.where(causal, att, 0.0)
    inner = jnp.einsum('bnhij,bnjhe->bnihe', att, v)
    b_last = b[:, :, -1]
    kv = jnp.einsum('bnjhd,bnjhe->bnhde', k * jnp.exp(b_last[:, :, None] - b), v)

    def step(state, inp):
        dec, kv_c = inp
        return state * dec[..., None] + kv_c, state

    _, prev = lax.scan(step, jnp.zeros((B, H, dk, dv), kv.dtype),
                       (jnp.moveaxis(jnp.exp(b_last), 1, 0), jnp.moveaxis(kv, 1, 0)))
    prev = jnp.moveaxis(prev, 0, 1)
    cross = jnp.einsum('bnihd,bnhde->bnihe', q * jnp.exp(b), prev)
    return (inner + cross).reshape(B, S, H, dv)


def ssd_chunked(x, dt, a, bmat, cmat):
    B, S, H, P = x.shape
    G, N = bmat.shape[-2:]
    R = H // G
    L = SSD_CHUNK
    n = S // L
    x = x.reshape(B, n, L, G, R, P)
    dt = dt.reshape(B, n, L, G, R)
    bm = bmat.reshape(B, n, L, G, N)
    cm = cmat.reshape(B, n, L, G, N)
    cum = jnp.cumsum(dt * a.reshape(G, R), axis=2)
    seg = cum[:, :, :, None] - cum[:, :, None]
    causal = jnp.tril(jnp.ones((L, L), dtype=bool))[:, :, None, None]
    decay = jnp.exp(jnp.where(causal, seg, -jnp.inf))
    cb = jnp.einsum('bclgn,bcsgn->bclsg', cm, bm)
    xdt = x * dt[..., None]
    inner = jnp.einsum('bclsgr,bcsgrp->bclgrp', cb[..., None] * decay, xdt)
    to_end = jnp.exp(cum[:, :, -1:] - cum)
    states = jnp.einsum('bcsgn,bcsgrp->bcgrpn', bm, xdt * to_end[..., None])

    def step(state, inp):
        dec, st = inp
        return state * dec[..., None, None] + st, state

    _, prev = lax.scan(step, jnp.zeros((B, G, R, P, N), states.dtype),
                       (jnp.moveaxis(jnp.exp(cum[:, :, -1]), 1, 0), jnp.moveaxis(states, 1, 0)))
    prev = jnp.moveaxis(prev, 0, 1)
    cross = jnp.einsum('bclgn,bcgrpn->bclgrp', cm, prev) * jnp.exp(cum)[..., None]
    return (inner + cross).reshape(B, S, H, P)


def causal_depthwise_conv(x, w, b):
    K, C = w.shape
    y = lax.conv_general_dilated(x, w[:, None, :].astype(x.dtype), window_strides=(1,),
                                 padding=[(K - 1, 0)], dimension_numbers=('NWC', 'WIO', 'NWC'),
                                 feature_group_count=C)
    return y + b.astype(x.dtype)


def ret_gla_mixer(h, pos, w_in, ret_norm, gla_wa2, gla_ba, gla_norm, w_out):
    B, S, _ = h.shape
    f32 = jnp.float32
    widths = [RET_QK, RET_QK, RET_V, RET_V, GLA_QK, GLA_QK, GLA_V, GLA_V, GLA_RANK]
    offs = np.cumsum(widths)[:-1].tolist()
    rq, rk, rv, rg, gq, gk, gv, gg, ga = jnp.split((h @ w_in).astype(f32), offs, axis=-1)
    rq = rotary(rq.reshape(B, S, RET_HEADS, RET_DK), pos)
    rk = rotary(rk.reshape(B, S, RET_HEADS, RET_DK), pos) * (RET_DK ** -0.5)
    rv = rv.reshape(B, S, RET_HEADS, RET_DV)
    log_gamma = jnp.log1p(-jnp.exp2(-5.0 - jnp.arange(RET_HEADS, dtype=f32)))
    ro = retention_chunked(rq, rk, rv, log_gamma)
    ro = rmsnorm(ro, ret_norm.reshape(RET_HEADS, RET_DV)).reshape(B, S, RET_V) * jax.nn.silu(rg)
    glog = jax.nn.log_sigmoid(ga @ gla_wa2.astype(f32) + gla_ba.astype(f32)) / GLA_TAU
    go = gla_chunked(gq.reshape(B, S, GLA_HEADS, GLA_DK), gk.reshape(B, S, GLA_HEADS, GLA_DK),
                     gv.reshape(B, S, GLA_HEADS, GLA_DV), glog.reshape(B, S, GLA_HEADS, GLA_DK))
    go = rmsnorm(go, gla_norm.reshape(GLA_HEADS, GLA_DV)).reshape(B, S, GLA_V) * jax.nn.silu(gg)
    return jnp.concatenate([ro, go], axis=-1).astype(h.dtype) @ w_out


def ssd_mixer(h, w_in, conv_w, conv_b, dt_bias, a_log, d_skip, norm_g, w_out):
    B, S, _ = h.shape
    f32 = jnp.float32
    z, xbc, dt_raw = jnp.split(h @ w_in, [SSD_DINNER, SSD_DINNER + SSD_CONV_CH], axis=-1)
    xbc = jax.nn.silu(causal_depthwise_conv(xbc, conv_w, conv_b))
    xs, bm, cm = jnp.split(xbc, [SSD_DINNER, SSD_DINNER + SSD_GROUPS * SSD_STATE], axis=-1)
    xs = xs.reshape(B, S, SSD_HEADS, SSD_HEADDIM).astype(f32)
    bm = bm.reshape(B, S, SSD_GROUPS, SSD_STATE).astype(f32)
    cm = cm.reshape(B, S, SSD_GROUPS, SSD_STATE).astype(f32)
    dt = jax.nn.softplus(dt_raw.astype(f32) + dt_bias.astype(f32))
    a = -jnp.exp(a_log.astype(f32))
    y = ssd_chunked(xs, dt, a, bm, cm) + d_skip.astype(f32)[:, None] * xs
    y = y.reshape(B, S, SSD_DINNER) * jax.nn.silu(z.astype(f32))
    gsz = SSD_DINNER // SSD_GROUPS
    y = rmsnorm(y.reshape(B, S, SSD_GROUPS, gsz), norm_g.reshape(SSD_GROUPS, gsz)).reshape(B, S, SSD_DINNER)
    return y.astype(h.dtype) @ w_out


def group_limited_route(t, router_w, router_b):
    T = t.shape[0]
    s = jax.nn.sigmoid(t.astype(jnp.float32) @ router_w.astype(jnp.float32))
    sb = (s + router_b.astype(jnp.float32)).reshape(T, N_GROUPS, EXPERTS_PER_GROUP)
    grp_score = lax.top_k(sb, GROUP_SCORE_K)[0].sum(-1)
    g_idx = jnp.argmax(grp_score, axis=-1)
    in_grp = jnp.arange(N_GROUPS)[None, :] == g_idx[:, None]
    masked = jnp.where(in_grp[:, :, None], sb, -jnp.inf).reshape(T, N_EXPERTS)
    _, e_idx = lax.top_k(masked, TOP_K)
    w = jnp.take_along_axis(s, e_idx, axis=-1)
    return e_idx, w / jnp.sum(w, axis=-1, keepdims=True)


def sparse_experts(t, e_idx, gates, w_gate, w_up, w_down):
    T, D = t.shape
    E = w_gate.shape[0]
    A = T * TOP_K
    flat_e = e_idx.reshape(-1)
    flat_t = jnp.arange(A, dtype=jnp.int32) // TOP_K
    flat_w = gates.reshape(-1)
    order = jnp.argsort(flat_e)
    se, st, sw = flat_e[order], flat_t[order], flat_w[order]
    counts = jnp.bincount(flat_e, length=E)
    padded = (counts + MOE_BLOCK - 1) // MOE_BLOCK * MOE_BLOCK
    start = jnp.cumsum(counts) - counts
    pend = jnp.cumsum(padded)
    pstart = pend - padded
    dest = pstart[se] + (jnp.arange(A) - start[se])
    NB = (A + E * (MOE_BLOCK - 1) + MOE_BLOCK - 1) // MOE_BLOCK
    NP = NB * MOE_BLOCK
    tok_pad = jnp.zeros((NP,), jnp.int32).at[dest].set(st)
    w_pad = jnp.zeros((NP,), jnp.float32).at[dest].set(sw)
    blk_e = jnp.minimum(jnp.searchsorted(pend, jnp.arange(NB) * MOE_BLOCK, side='right'), E - 1)

    def expert_block(args):
        tok, e = args
        xb = t[tok]
        return (jax.nn.silu(xb @ w_gate[e]) * (xb @ w_up[e])) @ w_down[e]

    y = lax.map(expert_block, (tok_pad.reshape(NB, MOE_BLOCK), blk_e)).reshape(NP, D)
    return jnp.zeros_like(t).at[tok_pad].add(y * w_pad[:, None].astype(t.dtype))


def moe_ffn(h, router_w, router_b, w_gate, w_up, w_down):
    B, S, D = h.shape
    t = h.reshape(B * S, D)
    e_idx, gates = group_limited_route(t, router_w, router_b)
    return sparse_experts(t, e_idx, gates, w_gate, w_up, w_down).reshape(B, S, D)


def setup_inputs(seed: int = 0) -> dict:
    key = jax.random.key(seed)
    ks = iter(jax.random.split(key, 40))
    f32 = jnp.float32

    def nrm(shape, scale):
        return scale * jax.random.normal(next(ks), shape, f32)

    def gain(shape):
        return 1.0 + nrm(shape, 0.02)

    D = D_MODEL
    offs = jax.random.randint(next(ks), (BATCH, 1), 0, 2048, dtype=jnp.int32)
    positions = (offs + jnp.arange(SEQ, dtype=jnp.int32)[None, :]).astype(jnp.int32)
    dt0 = jnp.exp(jax.random.uniform(next(ks), (N_ODD, SSD_HEADS), f32, math.log(1e-3), math.log(1e-1)))
    return {
        'x': nrm((BATCH, SEQ, D), 1.0),
        'c': nrm((BATCH, D), 1.0),
        'positions': positions,
        'mod_w': nrm((DEPTH, D, 6 * D), 0.5 * D ** -0.5),
        'mod_b': nrm((DEPTH, 6 * D), 0.02),
        'norm_mix': gain((DEPTH, D)),
        'norm_ffn': gain((DEPTH, D)),
        'ev_w_in': nrm((N_EVEN, D, EVEN_IN), D ** -0.5),
        'ev_ret_norm': gain((N_EVEN, RET_V)),
        'ev_gla_wa2': nrm((N_EVEN, GLA_RANK, GLA_QK), GLA_RANK ** -0.5),
        'ev_gla_ba': nrm((N_EVEN, GLA_QK), 0.02),
        'ev_gla_norm': gain((N_EVEN, GLA_V)),
        'ev_w_out': nrm((N_EVEN, EVEN_MIX, D), EVEN_MIX ** -0.5),
        'od_w_in': nrm((N_ODD, D, ODD_IN), D ** -0.5),
        'od_conv_w': nrm((N_ODD, SSD_CONV, SSD_CONV_CH), SSD_CONV ** -0.5),
        'od_conv_b': nrm((N_ODD, SSD_CONV_CH), 0.02),
        'od_dt_bias': dt0 + jnp.log(-jnp.expm1(-dt0)),
        'od_a_log': jnp.log(jax.random.uniform(next(ks), (N_ODD, SSD_HEADS), f32, 1.0, 16.0)),
        'od_d': gain((N_ODD, SSD_HEADS)),
        'od_norm': gain((N_ODD, SSD_DINNER)),
        'od_w_out': nrm((N_ODD, SSD_DINNER, D), SSD_DINNER ** -0.5),
        'router_w': nrm((D, N_EXPERTS), D ** -0.5),
        'router_b': nrm((N_EXPERTS,), 0.01),
        'exp_w_gate': nrm((DEPTH, N_EXPERTS, D, D_FF_EXPERT), D ** -0.5),
        'exp_w_up': nrm((DEPTH, N_EXPERTS, D, D_FF_EXPERT), D ** -0.5),
        'exp_w_down': nrm((DEPTH, N_EXPERTS, D_FF_EXPERT, D), D_FF_EXPERT ** -0.5),
        'final_norm': gain((D,)),
    }


def reference(x, c, positions, mod_w, mod_b, norm_mix, norm_ffn, ev_w_in, ev_ret_norm, ev_gla_wa2,
              ev_gla_ba, ev_gla_norm, ev_w_out, od_w_in, od_conv_w, od_conv_b, od_dt_bias, od_a_log,
              od_d, od_norm, od_w_out, router_w, router_b, exp_w_gate, exp_w_up, exp_w_down, final_norm):
    silu_c = jax.nn.silu(c)
    for layer in range(DEPTH):
        i = layer // 2
        mod = silu_c @ mod_w[layer] + mod_b[layer]
        sh1, sc1, g1, sh2, sc2, g2 = [m[:, None, :] for m in jnp.split(mod, 6, axis=-1)]
        h = rmsnorm(x, norm_mix[layer]) * (1.0 + sc1) + sh1
        if layer % 2 == 0:
            mix = ret_gla_mixer(h, positions, ev_w_in[i], ev_ret_norm[i], ev_gla_wa2[i], ev_gla_ba[i],
                                ev_gla_norm[i], ev_w_out[i])
        else:
            mix = ssd_mixer(h, od_w_in[i], od_conv_w[i], od_conv_b[i], od_dt_bias[i], od_a_log[i],
                            od_d[i], od_norm[i], od_w_out[i])
        x = x + g1 * mix
        h = rmsnorm(x, norm_ffn[layer]) * (1.0 + sc2) + sh2
        x = x + g2 * moe_ffn(h, router_w, router_b, exp_w_gate[layer], exp_w_up[layer], exp_w_down[layer])
    return rmsnorm(x, final_norm)
```

```python
import functools
import math

import jax
import jax.numpy as jnp
import numpy as np
from jax import lax
from jax.experimental import pallas as pl
from jax.experimental.pallas import tpu as pltpu

F32 = jnp.float32
BF16 = jnp.bfloat16
I32 = jnp.int32

RET_HEADS, RET_DK, RET_DV, RET_CHUNK = 4, 128, 256, 128
ROPE_BASE = 10000.0
GLA_HEADS, GLA_DK, GLA_DV, GLA_RANK, GLA_TAU, GLA_CHUNK = 4, 128, 256, 16, 16.0, 64
SSD_HEADDIM, SSD_GROUPS, SSD_STATE, SSD_CONV, SSD_CHUNK = 64, 8, 128, 4, 128
N_GROUPS, TOP_K, GROUP_SCORE_K, MOE_BLOCK = 8, 2, 2, 256
NORM_EPS = 1e-6

LANES = 128
VMEM_LIMIT = 56 * 1024 * 1024


def _cparams(sem, vmem=VMEM_LIMIT):
    return pltpu.CompilerParams(dimension_semantics=sem, vmem_limit_bytes=vmem)


def _dot(a, b):
    return jnp.dot(a, b, preferred_element_type=F32)


def _dot_nt(a, b):
    return lax.dot_general(a, b, (((1,), (1,)), ((), ())), preferred_element_type=F32)


def _dot_tn(a, b):
    return lax.dot_general(a, b, (((0,), (0,)), ((), ())), preferred_element_type=F32)


def _split3(x):
    a = x.astype(BF16)
    r = x - a.astype(F32)
    b = r.astype(BF16)
    c = (r - b.astype(F32)).astype(BF16)
    return a, b, c


def _dot_exact_rhs(m01, x):
    a, b, c = _split3(x)
    return _dot(m01, a) + _dot(m01, b) + _dot(m01, c)


def _dot_exact_lhs(x, m01):
    a, b, c = _split3(x)
    return _dot(a, m01) + _dot(b, m01) + _dot(c, m01)


def _dot_tn_exact(x, m01):
    a, b, c = _split3(x)
    return _dot_tn(a, m01) + _dot_tn(b, m01) + _dot_tn(c, m01)


def _silu(x):
    return x * jax.nn.sigmoid(x)


def _rms(x, eps=NORM_EPS):
    return x * lax.rsqrt(jnp.mean(x * x, axis=-1, keepdims=True) + eps)


def _iota(shape, dim):
    return lax.broadcasted_iota(I32, shape, dim)


def _mod_kernel(c_ref, w_ref, b_ref, o_ref):
    sc = _silu(c_ref[...])
    o_ref[0] = _dot(sc.astype(BF16), w_ref[0].astype(BF16)) + b_ref[0]


def _modulation(c, mod_w, mod_b):
    depth, d, d6 = mod_w.shape
    bsz = c.shape[0]
    nb = d6 // d
    out = pl.pallas_call(
        _mod_kernel,
        out_shape=jax.ShapeDtypeStruct((depth, bsz, d6), F32),
        grid=(depth, nb),
        in_specs=[pl.BlockSpec((bsz, d), lambda l, j: (0, 0)),
                  pl.BlockSpec((1, d, d), lambda l, j: (l, 0, j)),
                  pl.BlockSpec((1, 1, d), lambda l, j: (l, 0, j))],
        out_specs=pl.BlockSpec((1, bsz, d), lambda l, j: (l, 0, j)),
        compiler_params=_cparams(("arbitrary", "arbitrary")),
        name="adaln_mod",
    )(c, mod_w, mod_b.reshape(depth, 1, d6))
    return out.reshape(depth, bsz, nb, d)


def _norm_mod(x, gain, mod, shift_row, scale_row):
    y = _rms(x) * gain
    return y * (1.0 + mod[scale_row:scale_row + 1, :]) + mod[shift_row:shift_row + 1, :]


def _even_proj_kernel(x_ref, mod_ref, gain_ref, pos_ref, inv_ref, w_ref, wga_ref, wa2_ref, ba_ref,
                      y_ref, glog_ref, *, ncol):
    h = _norm_mod(x_ref[...], gain_ref[...], mod_ref[0], 0, 1)
    hb = h.astype(BF16)
    tm = hb.shape[0]
    ang = pos_ref[...].astype(F32) * inv_ref[...]
    cos = jnp.cos(ang)
    lane = _iota((tm, LANES), 1)
    sin = jnp.where(lane < LANES // 2, -jnp.sin(ang), jnp.sin(ang))
    qk_scale = RET_DK ** -0.5
    rqk = RET_HEADS * RET_DK
    rv = RET_HEADS * RET_DV
    gq0 = 2 * rqk + 2 * rv
    width = w_ref.shape[1] // ncol
    for j in range(ncol):
        c0 = j * width
        acc = _dot(hb, w_ref[:, c0:c0 + width])
        for s in range(width // LANES):
            col = c0 + s * LANES
            blk = acc[:, s * LANES:(s + 1) * LANES]
            if col < 2 * rqk:
                blk = blk * cos + pltpu.roll(blk, LANES // 2, 1) * sin
                if col >= rqk:
                    blk = blk * qk_scale
            elif gq0 <= col < gq0 + GLA_HEADS * GLA_DK:
                blk = blk * (GLA_DK ** -0.5)
            y_ref[:, col:col + LANES] = blk.astype(y_ref.dtype)
    ga = _dot(hb, wga_ref[...])
    z = _dot(ga.astype(BF16), wa2_ref[...]) + ba_ref[...]
    logsig = jnp.minimum(z, 0.0) - jnp.log1p(jnp.exp(-jnp.abs(z)))
    glog_ref[...] = logsig * (1.0 / GLA_TAU)


def _even_proj(x2d, mod_l, gain, pos_col, inv2, w_main, w_ga, wa2, ba, tm, seq):
    t, d = x2d.shape
    n = w_main.shape[1]
    tiles_per_b = seq // tm
    gk = wa2.shape[1]
    const = lambda i: (0, 0)
    return pl.pallas_call(
        functools.partial(_even_proj_kernel, ncol=n // 512),
        out_shape=(jax.ShapeDtypeStruct((t, n), BF16), jax.ShapeDtypeStruct((t, gk), F32)),
        grid=(t // tm,),
        in_specs=[pl.BlockSpec((tm, d), lambda i: (i, 0)),
                  pl.BlockSpec((1, 6, d), lambda i: (i // tiles_per_b, 0, 0)),
                  pl.BlockSpec((1, d), const),
                  pl.BlockSpec((tm, 1), lambda i: (i, 0)),
                  pl.BlockSpec((1, LANES), const),
                  pl.BlockSpec((d, n), const),
                  pl.BlockSpec((d, LANES), const),
                  pl.BlockSpec((LANES, gk), const),
                  pl.BlockSpec((1, gk), const)],
        out_specs=(pl.BlockSpec((tm, n), lambda i: (i, 0)), pl.BlockSpec((tm, gk), lambda i: (i, 0))),
        compiler_params=_cparams(("arbitrary",)),
        name="even_proj",
    )(x2d, mod_l, gain, pos_col, inv2, w_main, w_ga, wa2, ba)


def _ret_gla_kernel(rq_ref, rk_ref, rv_ref, rg_ref, gq_ref, gk_ref, gv_ref, gg_ref, gl_ref,
                    rn_ref, gn_ref, o_ref, rst_ref, gst_ref, *, log_gamma):
    @pl.when(pl.program_id(1) == 0)
    def _():
        rst_ref[...] = jnp.zeros_like(rst_ref)
        gst_ref[...] = jnp.zeros_like(gst_ref)

    ls = rq_ref.shape[0]
    lr, lg = RET_CHUNK, GLA_CHUNK
    row = _iota((lr, lr), 0)
    colm = _iota((lr, lr), 1)
    diff = (row - colm).astype(F32)
    rowf = row.astype(F32)

    for h in range(RET_HEADS):
        lgam = log_gamma[h]
        decay = jnp.where(diff >= 0, jnp.exp(lgam * jnp.maximum(diff, 0.0)), 0.0)
        zeta = jnp.exp(lgam * (lr - 1 - rowf))
        xi = jnp.exp(lgam * (rowf + 1.0))
        cdec = math.exp(lgam * lr)
        gain = rn_ref[:, h * RET_DV:(h + 1) * RET_DV]

        def ret_chunk(c, carry, h=h, decay=decay, zeta=zeta, xi=xi, cdec=cdec, gain=gain):
            r0 = pl.multiple_of(c * lr, lr)
            q = rq_ref[pl.ds(r0, lr), h * RET_DK:(h + 1) * RET_DK]
            k = rk_ref[pl.ds(r0, lr), h * RET_DK:(h + 1) * RET_DK]
            v = rv_ref[pl.ds(r0, lr), h * RET_DV:(h + 1) * RET_DV]
            g = rg_ref[pl.ds(r0, lr), h * RET_DV:(h + 1) * RET_DV].astype(F32)
            st = rst_ref[h]
            s = _dot_nt(q, k) * decay
            inner = _dot(s.astype(BF16), v)
            cross = _dot((q.astype(F32) * xi).astype(BF16), st.astype(BF16))
            kv = _dot_tn((k.astype(F32) * zeta).astype(BF16), v)
            rst_ref[h] = st * cdec + kv
            o = _rms(inner + cross) * gain * _silu(g)
            o_ref[pl.ds(r0, lr), h * RET_DV:(h + 1) * RET_DV] = o.astype(o_ref.dtype)
            return carry

        lax.fori_loop(0, ls // lr, ret_chunk, 0)

    trow = _iota((lg, lg), 0)
    tcol = _iota((lg, lg), 1)
    tri = (trow >= tcol).astype(BF16)
    causal = trow >= tcol
    base = RET_HEADS * RET_DV
    for h in range(GLA_HEADS):
        gain = gn_ref[:, h * GLA_DV:(h + 1) * GLA_DV]

        def gla_chunk(c, carry, h=h, gain=gain):
            r0 = pl.multiple_of(c * lg, lg)
            q = gq_ref[pl.ds(r0, lg), h * GLA_DK:(h + 1) * GLA_DK].astype(F32)
            k = gk_ref[pl.ds(r0, lg), h * GLA_DK:(h + 1) * GLA_DK].astype(F32)
            v = gv_ref[pl.ds(r0, lg), h * GLA_DV:(h + 1) * GLA_DV]
            g = gg_ref[pl.ds(r0, lg), h * GLA_DV:(h + 1) * GLA_DV].astype(F32)
            gl = gl_ref[pl.ds(r0, lg), h * GLA_DK:(h + 1) * GLA_DK]
            b = _dot_exact_rhs(tri, gl)
            bref = b[lg // 2:lg // 2 + 1, :]
            blast = b[lg - 1:lg, :]
            st = gst_ref[h]
            att = _dot_nt((q * jnp.exp(b - bref)).astype(BF16), (k * jnp.exp(bref - b)).astype(BF16))
            att = jnp.where(causal, att, 0.0)
            inner = _dot(att.astype(BF16), v)
            cross = _dot_nt((q * jnp.exp(b)).astype(BF16), st.astype(BF16))
            kvt = _dot_tn(v, (k * jnp.exp(blast - b)).astype(BF16))
            gst_ref[h] = st * jnp.exp(blast) + kvt
            o = _rms(inner + cross) * gain * _silu(g)
            o_ref[pl.ds(r0, lg), base + h * GLA_DV:base + (h + 1) * GLA_DV] = o.astype(o_ref.dtype)
            return carry

        lax.fori_loop(0, ls // lg, gla_chunk, 0)


def _ret_gla(y, glog, ret_norm, gla_norm, bsz, seq, ls):
    t = y.shape[0]
    ns = seq // ls
    qw, vw = RET_HEADS * RET_DK, RET_HEADS * RET_DV
    log_gamma = tuple(float(np.log1p(-np.exp2(np.float32(-5.0 - i)))) for i in range(RET_HEADS))
    rowmap = lambda j: (lambda b, i: (b * ns + i, j))
    const = lambda b, i: (0, 0)
    return pl.pallas_call(
        functools.partial(_ret_gla_kernel, log_gamma=log_gamma),
        out_shape=jax.ShapeDtypeStruct((t, 2 * vw), BF16),
        grid=(bsz, ns),
        in_specs=[pl.BlockSpec((ls, qw), rowmap(0)),
                  pl.BlockSpec((ls, qw), rowmap(1)),
                  pl.BlockSpec((ls, vw), rowmap(1)),
                  pl.BlockSpec((ls, vw), rowmap(2)),
                  pl.BlockSpec((ls, qw), rowmap(6)),
                  pl.BlockSpec((ls, qw), rowmap(7)),
                  pl.BlockSpec((ls, vw), rowmap(4)),
                  pl.BlockSpec((ls, vw), rowmap(5)),
                  pl.BlockSpec((ls, qw), rowmap(0)),
                  pl.BlockSpec((1, vw), const),
                  pl.BlockSpec((1, vw), const)],
        out_specs=pl.BlockSpec((ls, 2 * vw), lambda b, i: (b * ns + i, 0)),
        scratch_shapes=[pltpu.VMEM((RET_HEADS, RET_DK, RET_DV), F32),
                        pltpu.VMEM((GLA_HEADS, GLA_DV, GLA_DK), F32)],
        compiler_params=_cparams(("arbitrary", "arbitrary")),
        name="ret_gla_mixer",
    )(y, y, y, y, y, y, y, y, glog, ret_norm, gla_norm)


def _odd_proj_kernel(x_ref, mod_ref, gain_ref, w_ref, wdt_ref, dtb_ref, y_ref, dt_ref, *, ncol):
    h = _norm_mod(x_ref[...], gain_ref[...], mod_ref[0], 0, 1)
    hb = h.astype(BF16)
    width = w_ref.shape[1] // ncol
    for j in range(ncol):
        c0 = j * width
        y_ref[:, c0:c0 + width] = _dot(hb, w_ref[:, c0:c0 + width]).astype(y_ref.dtype)
    z = _dot(hb, wdt_ref[...]) + dtb_ref[...]
    dt_ref[...] = jnp.maximum(z, 0.0) + jnp.log1p(jnp.exp(-jnp.abs(z)))


def _odd_proj(x2d, mod_l, gain, w_main, w_dt, dt_bias, tm, seq):
    t, d = x2d.shape
    n = w_main.shape[1]
    tiles_per_b = seq // tm
    const = lambda i: (0, 0)
    return pl.pallas_call(
        functools.partial(_odd_proj_kernel, ncol=n // 512),
        out_shape=(jax.ShapeDtypeStruct((t, n), BF16), jax.ShapeDtypeStruct((t, LANES), F32)),
        grid=(t // tm,),
        in_specs=[pl.BlockSpec((tm, d), lambda i: (i, 0)),
                  pl.BlockSpec((1, 6, d), lambda i: (i // tiles_per_b, 0, 0)),
                  pl.BlockSpec((1, d), const),
                  pl.BlockSpec((d, n), const),
                  pl.BlockSpec((d, LANES), const),
                  pl.BlockSpec((1, LANES), const)],
        out_specs=(pl.BlockSpec((tm, n), lambda i: (i, 0)), pl.BlockSpec((tm, LANES), lambda i: (i, 0))),
        compiler_params=_cparams(("arbitrary",)),
        name="odd_proj",
    )(x2d, mod_l, gain, w_main, w_dt, dt_bias)


def _ssd_kernel(z_ref, xbc_ref, dt_ref, cw_ref, cb_ref, alog_ref, dsk_ref, ng_ref, e_ref,
                o_ref, hist_ref, st_ref):
    @pl.when(pl.program_id(1) == 0)
    def _():
        hist_ref[...] = jnp.zeros_like(hist_ref)
        st_ref[...] = jnp.zeros_like(st_ref)

    L = xbc_ref.shape[0]
    P, N, G = SSD_HEADDIM, SSD_STATE, SSD_GROUPS
    dinner = z_ref.shape[1]
    gw = dinner // G
    hpg = gw // P

    xin = xbc_ref[...].astype(F32)
    hist = hist_ref[...]
    r8 = _iota((8, xin.shape[1]), 0)
    conv = xin * cw_ref[SSD_CONV - 1:SSD_CONV, :] + cb_ref[...]
    for s in range(1, SSD_CONV):
        rolled = pltpu.roll(xin, s, 0)
        top = jnp.where(r8 < s, pltpu.roll(hist, s, 0), rolled[0:8])
        shifted = jnp.concatenate([top, rolled[8:]], axis=0)
        conv = conv + shifted * cw_ref[SSD_CONV - 1 - s:SSD_CONV - s, :]
    hist_ref[...] = xin[L - 8:L]
    act = _silu(conv)
    xs = act[:, :dinner]
    bm = act[:, dinner:dinner + G * N].astype(BF16)
    cm = act[:, dinner + G * N:].astype(BF16)

    dt = dt_ref[...]
    da = dt * (-jnp.exp(alog_ref[...]))
    trow = _iota((L, L), 0)
    tcol = _iota((L, L), 1)
    causal = trow >= tcol
    tril = causal.astype(BF16)
    cum = _dot_exact_rhs(tril, da)
    cum_t = _dot_tn_exact(da, (trow <= tcol).astype(BF16))
    expand = e_ref[...]
    cum_e = _dot_exact_lhs(cum, expand)
    dta, dtb, _ = _split3(dt)
    dt_e = _dot(dta, expand) + _dot(dtb, expand)
    xdt = xs * dt_e
    cum_last = cum_e[L - 1:L, :]
    xdt_end = (xdt * jnp.exp(cum_last - cum_e)).astype(BF16)
    xdt_b = xdt.astype(BF16)
    ecum = jnp.exp(cum_e)
    chunk_dec = jnp.exp(cum_last)
    lane = _iota((L, LANES), 1)

    for g in range(G):
        bm_g = bm[:, g * N:(g + 1) * N]
        cm_g = cm[:, g * N:(g + 1) * N]
        cb = _dot_nt(cm_g, bm_g)
        pieces = []
        for pr in range(hpg // 2):
            ms = []
            for r in (2 * pr, 2 * pr + 1):
                hh = g * hpg + r
                seg = jnp.broadcast_to(cum[:, hh:hh + 1], (L, L)) - cum_t[hh:hh + 1, :]
                ms.append((cb * jnp.where(causal, jnp.exp(seg), 0.0)).astype(BF16))
            lhs = jnp.concatenate(ms, axis=1)
            blk = xdt_b[:, g * gw + pr * 2 * P:g * gw + (pr + 1) * 2 * P]
            zero = jnp.zeros_like(blk)
            rhs = jnp.concatenate([jnp.where(lane < P, blk, zero), jnp.where(lane >= P, blk, zero)], axis=0)
            pieces.append(_dot(lhs, rhs))
        inner = jnp.concatenate(pieces, axis=1)
        st = st_ref[g]
        cross = _dot(cm_g, st.astype(BF16)) * ecum[:, g * gw:(g + 1) * gw]
        st_ref[g] = st * chunk_dec[:, g * gw:(g + 1) * gw] + _dot_tn(bm_g, xdt_end[:, g * gw:(g + 1) * gw])
        y = inner + cross + dsk_ref[:, g * gw:(g + 1) * gw] * xs[:, g * gw:(g + 1) * gw]
        y = y * _silu(z_ref[:, g * gw:(g + 1) * gw].astype(F32))
        o_ref[:, g * gw:(g + 1) * gw] = (_rms(y) * ng_ref[:, g * gw:(g + 1) * gw]).astype(o_ref.dtype)


def _ssd(y, dt, conv_w, conv_b, a_log_row, d_row, norm_g, expand, bsz, seq):
    t = y.shape[0]
    L = SSD_CHUNK
    ns = seq // L
    dinner = norm_g.shape[1]
    cch = conv_w.shape[1]
    gw = dinner // SSD_GROUPS
    const = lambda b, i: (0, 0)
    zblk = dinner // 2048 if dinner >= 2048 else 1
    assert dinner == 2048 and cch == 4096
    return pl.pallas_call(
        _ssd_kernel,
        out_shape=jax.ShapeDtypeStruct((t, dinner), BF16),
        grid=(bsz, ns),
        in_specs=[pl.BlockSpec((L, dinner), lambda b, i: (b * ns + i, cch // dinner)),
                  pl.BlockSpec((L, cch), lambda b, i: (b * ns + i, 0)),
                  pl.BlockSpec((L, LANES), lambda b, i: (b * ns + i, 0)),
                  pl.BlockSpec((SSD_CONV, cch), const),
                  pl.BlockSpec((1, cch), const),
                  pl.BlockSpec((1, LANES), const),
                  pl.BlockSpec((1, dinner), const),
                  pl.BlockSpec((1, dinner), const),
                  pl.BlockSpec((LANES, dinner), const)],
        out_specs=pl.BlockSpec((L, dinner), lambda b, i: (b * ns + i, 0)),
        scratch_shapes=[pltpu.VMEM((8, cch), F32),
                        pltpu.VMEM((SSD_GROUPS, SSD_STATE, gw), F32)],
        compiler_params=_cparams(("arbitrary", "arbitrary")),
        name="ssd_mixer",
    )(y, y, dt, conv_w, conv_b, a_log_row, d_row, norm_g, expand)


def _outproj_kernel(o_ref, x_ref, mod_ref, gain_ref, w_ref, x1_ref, h2_ref):
    mod = mod_ref[0]
    x1 = x_ref[...] + mod[2:3, :] * _dot(o_ref[...], w_ref[...])
    x1_ref[...] = x1
    h2_ref[...] = _norm_mod(x1, gain_ref[...], mod, 3, 4)


def _outproj(o, x2d, mod_l, gain2, w_out, tm, seq):
    t, d = x2d.shape
    kin = o.shape[1]
    tiles_per_b = seq // tm
    const = lambda i: (0, 0)
    return pl.pallas_call(
        _outproj_kernel,
        out_shape=(jax.ShapeDtypeStruct((t, d), F32), jax.ShapeDtypeStruct((t, d), F32)),
        grid=(t // tm,),
        in_specs=[pl.BlockSpec((tm, kin), lambda i: (i, 0)),
                  pl.BlockSpec((tm, d), lambda i: (i, 0)),
                  pl.BlockSpec((1, 6, d), lambda i: (i // tiles_per_b, 0, 0)),
                  pl.BlockSpec((1, d), const),
                  pl.BlockSpec((kin, d), const)],
        out_specs=(pl.BlockSpec((tm, d), lambda i: (i, 0)), pl.BlockSpec((tm, d), lambda i: (i, 0))),
        compiler_params=_cparams(("arbitrary",)),
        name="out_proj",
    )(o, x2d, mod_l, gain2, w_out)


def _route_kernel(h_ref, rw_ref, rb_ref, dest_ref, gate_ref, blk_ref, cnt_ref, pst_ref, *, nbpad):
    phase = pl.program_id(0)
    step = pl.program_id(1)
    tm = h_ref.shape[0]
    ne = rw_ref.shape[0]
    epg = ne // N_GROUPS

    @pl.when(jnp.logical_and(phase == 0, step == 0))
    def _():
        cnt_ref[...] = jnp.zeros_like(cnt_ref)
        pst_ref[...] = jnp.zeros_like(pst_ref)

    @pl.when(jnp.logical_and(phase == 1, step == 0))
    def _():
        nblk = jnp.floor((cnt_ref[...] + (MOE_BLOCK - 1.0)) * (1.0 / MOE_BLOCK))
        hi = jnp.floor(nblk * (1.0 / 16.0))
        lo = nblk - hi * 16.0
        er = _iota((ne, ne), 0)
        ec = _iota((ne, ne), 1)
        lower = (ec < er).astype(BF16)
        pst = _dot(lower, hi.astype(BF16)) * 16.0 + _dot(lower, lo.astype(BF16))
        pst_ref[...] = pst
        pend = jnp.concatenate([pst + nblk] * (nbpad // LANES), axis=1)
        bidx = _iota((ne, nbpad), 1).astype(F32)
        be = jnp.sum(jnp.where(pend <= bidx, 1.0, 0.0), axis=0, keepdims=True)
        total = jnp.max(pend, axis=0, keepdims=True)
        blk_ref[0:1, :] = jnp.minimum(be, ne - 1.0).astype(I32)
        blk_ref[1:2, :] = total.astype(I32)
        cnt_ref[...] = jnp.zeros_like(cnt_ref)

    logits = _dot_nt(rw_ref[...], h_ref[...].astype(BF16))
    s = jax.nn.sigmoid(logits)
    sb = s + jnp.concatenate([rb_ref[...]] * (tm // LANES), axis=1)
    G = N_GROUPS
    v = [sb[j * G:(j + 1) * G, :] for j in range(epg)]
    m01, n01 = jnp.maximum(v[0], v[1]), jnp.minimum(v[0], v[1])
    m23, n23 = jnp.maximum(v[2], v[3]), jnp.minimum(v[2], v[3])
    top1 = jnp.maximum(m01, m23)
    top2 = jnp.maximum(jnp.minimum(m01, m23), jnp.maximum(n01, n23))
    gscore = top1 + top2
    gi = _iota((G, tm), 0)
    gmax = jnp.max(gscore, axis=0, keepdims=True)
    gidx = jnp.min(jnp.where(gscore == gmax, gi, G), axis=0, keepdims=True)
    sel = gi == gidx
    vb = [jnp.sum(jnp.where(sel, v[j], 0.0), axis=0, keepdims=True) for j in range(epg)]
    vs = [jnp.sum(jnp.where(sel, s[j * G:(j + 1) * G, :], 0.0), axis=0, keepdims=True) for j in range(epg)]
    zero = jnp.zeros_like(vb[0])
    e1 = zero
    e2 = zero
    w1 = zero
    w2 = zero
    for j in range(epg):
        rank = zero
        for i in range(epg):
            if i == j:
                continue
            ahead = (vb[i] >= vb[j]) if i < j else (vb[i] > vb[j])
            rank = rank + jnp.where(ahead, 1.0, 0.0)
        first = rank == 0.0
        second = rank == 1.0
        e1 = e1 + jnp.where(first, float(j), 0.0)
        e2 = e2 + jnp.where(second, float(j), 0.0)
        w1 = w1 + jnp.where(first, vs[j], 0.0)
        w2 = w2 + jnp.where(second, vs[j], 0.0)
    gf = gidx.astype(F32) * float(epg)
    e1 = e1 + gf
    e2 = e2 + gf
    wsum = w1 + w2
    ei = _iota((ne, tm), 0).astype(F32)
    oh1 = ei == e1
    oh2 = ei == e2
    onehot = jnp.where(oh1 | oh2, 1.0, 0.0)
    upper = (_iota((tm, tm), 0) < _iota((tm, tm), 1)).astype(BF16)
    prefix = _dot(onehot.astype(BF16), upper)
    cnt = cnt_ref[...]
    base = jnp.concatenate([cnt] * (tm // LANES), axis=1) + prefix
    pst = jnp.concatenate([pst_ref[...]] * (tm // LANES), axis=1) * float(MOE_BLOCK)
    slot = base + pst
    d1 = jnp.sum(jnp.where(oh1, slot, 0.0), axis=0, keepdims=True)
    d2 = jnp.sum(jnp.where(oh2, slot, 0.0), axis=0, keepdims=True)
    cnt_ref[...] = cnt + jnp.sum(onehot, axis=1, keepdims=True)
    dest_ref[0, 0:1, :] = d1.astype(I32)
    dest_ref[0, 1:2, :] = d2.astype(I32)
    gate_ref[0, 0:1, :] = w1 / wsum
    gate_ref[0, 1:2, :] = w2 / wsum


def _route(h2, rw_t, rb_b, tm, nbpad):
    t, d = h2.shape
    ne = rw_t.shape[0]
    nt = t // tm
    return pl.pallas_call(
        functools.partial(_route_kernel, nbpad=nbpad),
        out_shape=(jax.ShapeDtypeStruct((nt, 2, tm), I32), jax.ShapeDtypeStruct((nt, 2, tm), F32),
                   jax.ShapeDtypeStruct((2, nbpad), I32)),
        grid=(2, nt),
        in_specs=[pl.BlockSpec((tm, d), lambda p, i: (i, 0)),
                  pl.BlockSpec((ne, d), lambda p, i: (0, 0)),
                  pl.BlockSpec((ne, LANES), lambda p, i: (0, 0))],
        out_specs=(pl.BlockSpec((1, 2, tm), lambda p, i: (i * p, 0, 0)),
                   pl.BlockSpec((1, 2, tm), lambda p, i: (i * p, 0, 0)),
                   pl.BlockSpec((2, nbpad), lambda p, i: (0, 0))),
        scratch_shapes=[pltpu.VMEM((ne, LANES), F32), pltpu.VMEM((ne, LANES), F32)],
        compiler_params=_cparams(("arbitrary", "arbitrary")),
        name="router",
    )(h2, rw_t, rb_b)


def _dispatch_kernel(dest_ref, h_ref, xs_in_ref, xs_ref, sem):
    del xs_in_ref
    tg = h_ref.shape[0]

    def issue(t, carry):
        for k in range(TOP_K):
            d = dest_ref[0, 0, k * tg + t]
            pltpu.make_async_copy(h_ref.at[pl.ds(t, 1)], xs_ref.at[pl.ds(d, 1)], sem).start()
        return carry

    lax.fori_loop(0, tg, issue, 0)
    for k in range(TOP_K):
        pltpu.make_async_copy(h_ref, xs_ref.at[pl.ds(0, tg)], sem).wait()


def _dispatch(h2, dest_flat, np_rows, tg):
    t, d = h2.shape
    nt = t // tg
    xs0 = jnp.zeros((np_rows, d), h2.dtype)
    return pl.pallas_call(
        _dispatch_kernel,
        out_shape=jax.ShapeDtypeStruct((np_rows, d), h2.dtype),
        grid=(nt,),
        in_specs=[pl.BlockSpec((1, 1, TOP_K * tg), lambda i: (i, 0, 0), memory_space=pltpu.SMEM),
                  pl.BlockSpec((tg, d), lambda i: (i, 0)),
                  pl.BlockSpec(memory_space=pl.ANY)],
        out_specs=pl.BlockSpec(memory_space=pl.ANY),
        scratch_shapes=[pltpu.SemaphoreType.DMA(())],
        input_output_aliases={2: 0},
        compiler_params=_cparams(("arbitrary",)),
        name="moe_dispatch",
    )(dest_flat, h2, xs0)


def _expert_kernel(blk_ref, nused_ref, xs_ref, wg_ref, wu_ref, wd_ref, y_ref, wgb, wub, wdb):
    b = pl.program_id(0)
    prev = blk_ref[jnp.maximum(b - 1, 0)]
    changed = jnp.logical_or(b == 0, blk_ref[b] != prev)

    @pl.when(changed)
    def _():
        wgb[...] = wg_ref[0].astype(BF16)
        wub[...] = wu_ref[0].astype(BF16)
        wdb[...] = wd_ref[0].astype(BF16)

    @pl.when(b < nused_ref[0])
    def _():
        xb = xs_ref[...].astype(BF16)
        g = _dot(xb, wgb[...])
        u = _dot(xb, wub[...])
        a = (_silu(g) * u).astype(BF16)
        y_ref[...] = _dot(a, wdb[...]).astype(y_ref.dtype)

    @pl.when(b >= nused_ref[0])
    def _():
        y_ref[...] = jnp.zeros_like(y_ref)


def _experts(xs, blk_e, nused, w_gate, w_up, w_down):
    np_rows, d = xs.shape
    ne, _, ff = w_gate.shape
    nb = np_rows // MOE_BLOCK
    grid_spec = pltpu.PrefetchScalarGridSpec(
        num_scalar_prefetch=2,
        grid=(nb,),
        in_specs=[pl.BlockSpec((MOE_BLOCK, d), lambda b, be, nu: (b, 0)),
                  pl.BlockSpec((1, d, ff), lambda b, be, nu: (be[b], 0, 0)),
                  pl.BlockSpec((1, d, ff), lambda b, be, nu: (be[b], 0, 0)),
                  pl.BlockSpec((1, ff, d), lambda b, be, nu: (be[b], 0, 0))],
        out_specs=pl.BlockSpec((MOE_BLOCK, d), lambda b, be, nu: (b, 0)),
        scratch_shapes=[pltpu.VMEM((d, ff), BF16), pltpu.VMEM((d, ff), BF16), pltpu.VMEM((ff, d), BF16)],
    )
    return pl.pallas_call(
        _expert_kernel,
        out_shape=jax.ShapeDtypeStruct((np_rows, d), F32),
        grid_spec=grid_spec,
        compiler_params=_cparams(("arbitrary",)),
        name="moe_experts",
    )(blk_e, nused, xs, w_gate, w_up, w_down)


def _combine_kernel(dest_ref, x1_ref, gate_ref, mod_ref, fg_ref, ys_ref, o_ref, ybuf, sem, *, final):
    tc = x1_ref.shape[0]
    d = x1_ref.shape[1]

    def issue(t, carry):
        for k in range(TOP_K):
            src = dest_ref[0, 0, k * tc + t]
            pltpu.make_async_copy(ys_ref.at[pl.ds(src, 1)], ybuf.at[k, pl.ds(t, 1)], sem).start()
        return carry

    lax.fori_loop(0, tc, issue, 0)
    for k in range(TOP_K):
        pltpu.make_async_copy(ys_ref.at[pl.ds(0, tc)], ybuf.at[k], sem).wait()

    gates = gate_ref[0]
    acc = jnp.zeros((tc, d), F32)
    for k in range(TOP_K):
        gcol = jnp.transpose(jnp.broadcast_to(gates[k:k + 1, :], (LANES, tc)))
        acc = acc + ybuf[k] * jnp.concatenate([gcol] * (d // LANES), axis=1)
    x2 = x1_ref[...] + mod_ref[0][5:6, :] * acc
    if final:
        x2 = _rms(x2) * fg_ref[...]
    o_ref[...] = x2


def _combine(x1, ys, dest_flat, gates, mod_l, final_gain, tc, seq, final):
    t, d = x1.shape
    nt = t // tc
    tiles_per_b = seq // tc
    return pl.pallas_call(
        functools.partial(_combine_kernel, final=final),
        out_shape=jax.ShapeDtypeStruct((t, d), F32),
        grid=(nt,),
        in_specs=[pl.BlockSpec((1, 1, TOP_K * tc), lambda i: (i, 0, 0), memory_space=pltpu.SMEM),
                  pl.BlockSpec((tc, d), lambda i: (i, 0)),
                  pl.BlockSpec((1, TOP_K, tc), lambda i: (i, 0, 0)),
                  pl.BlockSpec((1, 6, d), lambda i: (i // tiles_per_b, 0, 0)),
                  pl.BlockSpec((1, d), lambda i: (0, 0)),
                  pl.BlockSpec(memory_space=pl.ANY)],
        out_specs=pl.BlockSpec((tc, d), lambda i: (i, 0)),
        scratch_shapes=[pltpu.VMEM((TOP_K, tc, d), F32), pltpu.SemaphoreType.DMA(())],
        compiler_params=_cparams(("arbitrary",)),
        name="moe_combine",
    )(dest_flat, x1, gates, mod_l, final_gain, ys)


ROUTE_TILE = 512


def _moe(x1, h2, mod_l, rw_t, rb_b, w_gate, w_up, w_down, final_gain, seq, final):
    t, d = x1.shape
    ne = w_gate.shape[0]
    nb = (t * TOP_K + ne * (MOE_BLOCK - 1) + MOE_BLOCK - 1) // MOE_BLOCK
    nbpad = -(-nb // LANES) * LANES
    tm = ROUTE_TILE
    dest, gates, blk = _route(h2, rw_t, rb_b, tm, nbpad)
    dest_flat = dest.reshape(t // tm, 1, TOP_K * tm)
    xs = _dispatch(h2, dest_flat, nb * MOE_BLOCK, tm)
    ys = _experts(xs, blk[0, :nb], blk[1, :1], w_gate, w_up, w_down)
    return _combine(x1, ys, dest_flat, gates, mod_l, final_gain, tm, seq, final)


def kernel(x, c, positions, mod_w, mod_b, norm_mix, norm_ffn, ev_w_in, ev_ret_norm, ev_gla_wa2, ev_gla_ba,
           ev_gla_norm, ev_w_out, od_w_in, od_conv_w, od_conv_b, od_dt_bias, od_a_log, od_d, od_norm, od_w_out,
           router_w, router_b, exp_w_gate, exp_w_up, exp_w_down, final_norm):
    bsz, seq, d = x.shape
    depth = mod_w.shape[0]
    t = bsz * seq
    ne = router_w.shape[1]
    epg = ne // N_GROUPS
    mod = _modulation(c, mod_w, mod_b)
    x2d = x.reshape(t, d)

    perm = np.array([g * epg + j for j in range(epg) for g in range(N_GROUPS)])
    rw_t = router_w.T[perm].astype(BF16)
    rb_b = jnp.broadcast_to(router_b.astype(F32)[perm][:, None], (ne, LANES))
    final_gain = final_norm.reshape(1, d)

    half = RET_DK // 2
    inv = ROPE_BASE ** (-jnp.arange(half, dtype=F32) / half)
    inv2 = jnp.concatenate([inv, inv]).reshape(1, LANES)
    pos_col = positions.reshape(t, 1)

    for layer in range(depth):
        i = layer // 2
        mod_l = mod[layer]
        gain1 = norm_mix[layer].reshape(1, d)
        gain2 = norm_ffn[layer].reshape(1, d)
        if layer % 2 == 0:
            w_in = ev_w_in[i]
            nmain = w_in.shape[1] - GLA_RANK
            w_main = w_in[:, :nmain].astype(BF16)
            w_ga = jnp.pad(w_in[:, nmain:], ((0, 0), (0, LANES - GLA_RANK))).astype(BF16)
            wa2 = jnp.pad(ev_gla_wa2[i], ((0, LANES - GLA_RANK), (0, 0))).astype(BF16)
            y, glog = _even_proj(x2d, mod_l, gain1, pos_col, inv2, w_main, w_ga, wa2,
                                 ev_gla_ba[i].reshape(1, -1), 512, seq)
            o = _ret_gla(y, glog, ev_ret_norm[i].reshape(1, -1), ev_gla_norm[i].reshape(1, -1), bsz, seq, 512)
            w_out = ev_w_out[i].astype(BF16)
        else:
            w_in = od_w_in[i]
            nheads = od_a_log.shape[1]
            nmain = w_in.shape[1] - nheads
            dinner = od_norm.shape[1]
            w_main = jnp.concatenate([w_in[:, dinner:nmain], w_in[:, :dinner]], axis=1).astype(BF16)
            w_dt = jnp.pad(w_in[:, nmain:], ((0, 0), (0, LANES - nheads))).astype(BF16)
            dtb = jnp.pad(od_dt_bias[i], (0, LANES - nheads)).reshape(1, LANES)
            y, dt = _odd_proj(x2d, mod_l, gain1, w_main, w_dt, dtb, 512, seq)
            alog = jnp.pad(od_a_log[i], (0, LANES - nheads)).reshape(1, LANES)
            d_row = jnp.repeat(od_d[i], SSD_HEADDIM).reshape(1, dinner)
            expand = (jnp.arange(LANES)[:, None] == (jnp.arange(dinner) // SSD_HEADDIM)[None, :]).astype(BF16)
            o = _ssd(y, dt, od_conv_w[i], od_conv_b[i].reshape(1, -1), alog, d_row,
                     od_norm[i].reshape(1, dinner), expand, bsz, seq)
            w_out = od_w_out[i].astype(BF16)
        x1, h2 = _outproj(o, x2d, mod_l, gain2, w_out, 512, seq)
        x2d = _moe(x1, h2, mod_l, rw_t, rb_b, exp_w_gate[layer], exp_w_up[layer], exp_w_down[layer],
                   final_gain, seq, final=(layer == depth - 1))
    return x2d.reshape(bsz, seq, d)
```

```python
import functools

import jax
import jax.numpy as jnp
import numpy as np
from jax import lax
from jax.experimental import pallas as pl
from jax.experimental.pallas import tpu as pltpu

F32 = jnp.float32
BF16 = jnp.bfloat16
I32 = jnp.int32

RET_HEADS, RET_DK, RET_DV, RET_CHUNK = 4, 128, 256, 128
ROPE_BASE = 10000.0
GLA_HEADS, GLA_DK, GLA_DV, GLA_RANK, GLA_TAU, GLA_CHUNK = 4, 128, 256, 16, 16.0, 64
SSD_HEADDIM, SSD_GROUPS, SSD_STATE, SSD_CONV, SSD_CHUNK = 64, 8, 128, 4, 128
N_GROUPS, TOP_K, GROUP_SCORE_K, MOE_BLOCK = 8, 2, 2, 256
NORM_EPS = 1e-6

LANES = 128
VMEM_LIMIT = 56 * 1024 * 1024


def _cparams(sem, vmem=VMEM_LIMIT):
    return pltpu.CompilerParams(dimension_semantics=sem, vmem_limit_bytes=vmem)


def _dot(a, b):
    return jnp.dot(a, b, preferred_element_type=F32)


def _dot_nt(a, b):
    return lax.dot_general(a, b, (((1,), (1,)), ((), ())), preferred_element_type=F32)


def _dot_tn(a, b):
    return lax.dot_general(a, b, (((0,), (0,)), ((), ())), preferred_element_type=F32)


def _split3(x):
    a = x.astype(BF16)
    r = x - a.astype(F32)
    b = r.astype(BF16)
    c = (r - b.astype(F32)).astype(BF16)
    return a, b, c


def _dot_exact_rhs(m01, x):
    a, b, c = _split3(x)
    return _dot(m01, a) + _dot(m01, b) + _dot(m01, c)


def _dot_exact_lhs(x, m01):
    a, b, c = _split3(x)
    return _dot(a, m01) + _dot(b, m01) + _dot(c, m01)


def _dot_tn_exact(x, m01):
    a, b, c = _split3(x)
    return _dot_tn(a, m01) + _dot_tn(b, m01) + _dot_tn(c, m01)


def _silu(x):
    return x * jax.nn.sigmoid(x)


def _rms(x, eps=NORM_EPS):
    return x * lax.rsqrt(jnp.mean(x * x, axis=-1, keepdims=True) + eps)


def _iota(shape, dim):
    return lax.broadcasted_iota(I32, shape, dim)


def _mod_kernel(c_ref, w_ref, b_ref, o_ref):
    sc = _silu(c_ref[...])
    o_ref[0] = _dot(sc.astype(BF16), w_ref[0].astype(BF16)) + b_ref[0]


def _modulation(c, mod_w, mod_b):
    depth, d, d6 = mod_w.shape
    bsz = c.shape[0]
    nb = d6 // d
    out = pl.pallas_call(
        _mod_kernel,
        out_shape=jax.ShapeDtypeStruct((depth, bsz, d6), F32),
        grid=(depth, nb),
        in_specs=[pl.BlockSpec((bsz, d), lambda l, j: (0, 0)),
                  pl.BlockSpec((1, d, d), lambda l, j: (l, 0, j)),
                  pl.BlockSpec((1, 1, d), lambda l, j: (l, 0, j))],
        out_specs=pl.BlockSpec((1, bsz, d), lambda l, j: (l, 0, j)),
        compiler_params=_cparams(("arbitrary", "arbitrary")),
        name="adaln_mod",
    )(c, mod_w, mod_b.reshape(depth, 1, d6))
    return out.reshape(depth, bsz, nb, d)


def _norm_mod(x, gain, mod, shift_row, scale_row):
    y = _rms(x) * gain
    return y * (1.0 + mod[scale_row:scale_row + 1, :]) + mod[shift_row:shift_row + 1, :]


def _even_proj_kernel(x_ref, mod_ref, gain_ref, pos_ref, inv_ref, w_ref, wga_ref, wa2_ref, ba_ref,
                      y_ref, glog_ref, *, ncol):
    h = _norm_mod(x_ref[...], gain_ref[...], mod_ref[0], 0, 1)
    hb = h.astype(BF16)
    tm = hb.shape[0]
    ang = pos_ref[...].astype(F32) * inv_ref[...]
    cos = jnp.cos(ang)
    lane = _iota((tm, LANES), 1)
    sin = jnp.where(lane < LANES // 2, -jnp.sin(ang), jnp.sin(ang))
    qk_scale = RET_DK ** -0.5
    rqk = RET_HEADS * RET_DK
    rv = RET_HEADS * RET_DV
    gq0 = 2 * rqk + 2 * rv
    width = w_ref.shape[1] // ncol
    for j in range(ncol):
        c0 = j * width
        acc = _dot(hb, w_ref[:, c0:c0 + width])
        for s in range(width // LANES):
            col = c0 + s * LANES
            blk = acc[:, s * LANES:(s + 1) * LANES]
            if col < 2 * rqk:
                blk = blk * cos + pltpu.roll(blk, LANES // 2, 1) * sin
                if col >= rqk:
                    blk = blk * qk_scale
            elif gq0 <= col < gq0 + GLA_HEADS * GLA_DK:
                blk = blk * (GLA_DK ** -0.5)
            y_ref[:, col:col + LANES] = blk.astype(y_ref.dtype)
    ga = _dot(hb, wga_ref[...])
    z = _dot(ga.astype(BF16), wa2_ref[...]) + ba_ref[...]
    logsig = jnp.minimum(z, 0.0) - jnp.log1p(jnp.exp(-jnp.abs(z)))
    glog_ref[...] = logsig * (1.0 / GLA_TAU)


def _even_proj(x2d, mod_l, gain, pos_col, inv2, w_main, w_ga, wa2, ba, tm, seq):
    t, d = x2d.shape
    n = w_main.shape[1]
    tiles_per_b = seq // tm
    gk = wa2.shape[1]
    const = lambda i: (0, 0)
    return pl.pallas_call(
        functools.partial(_even_proj_kernel, ncol=n // 512),
        out_shape=(jax.ShapeDtypeStruct((t, n), BF16), jax.ShapeDtypeStruct((t, gk), F32)),
        grid=(t // tm,),
        in_specs=[pl.BlockSpec((tm, d), lambda i: (i, 0)),
                  pl.BlockSpec((1, 6, d), lambda i: (i // tiles_per_b, 0, 0)),
                  pl.BlockSpec((1, d), const),
                  pl.BlockSpec((tm, 1), lambda i: (i, 0)),
                  pl.BlockSpec((1, LANES), const),
                  pl.BlockSpec((d, n), const),
                  pl.BlockSpec((d, LANES), const),
                  pl.BlockSpec((LANES, gk), const),
                  pl.BlockSpec((1, gk), const)],
        out_specs=(pl.BlockSpec((tm, n), lambda i: (i, 0)), pl.BlockSpec((tm, gk), lambda i: (i, 0))),
        compiler_params=_cparams(("arbitrary",)),
        name="even_proj",
    )(x2d, mod_l, gain, pos_col, inv2, w_main, w_ga, wa2, ba)


RET_GLA_UNROLL = 2


def _ret_gla_kernel(rq_ref, rk_ref, rv_ref, rg_ref, gq_ref, gk_ref, gv_ref, gg_ref, gl_ref,
                    rn_ref, gn_ref, dec_ref, zeta_ref, xi_ref, o_ref, rst_ref, gst_ref, *, chunk_decay):
    @pl.when(pl.program_id(1) == 0)
    def _():
        rst_ref[...] = jnp.zeros_like(rst_ref)
        gst_ref[...] = jnp.zeros_like(gst_ref)

    ls = rq_ref.shape[0]
    lr, lg = RET_CHUNK, GLA_CHUNK
    trow = _iota((lg, lg), 0)
    tcol = _iota((lg, lg), 1)
    causal = trow >= tcol
    tri = causal.astype(BF16)
    base = RET_HEADS * RET_DV

    def ret_chunk(r0, h):
        q = rq_ref[pl.ds(r0, lr), h * RET_DK:(h + 1) * RET_DK]
        k = rk_ref[pl.ds(r0, lr), h * RET_DK:(h + 1) * RET_DK]
        v = rv_ref[pl.ds(r0, lr), h * RET_DV:(h + 1) * RET_DV]
        g = rg_ref[pl.ds(r0, lr), h * RET_DV:(h + 1) * RET_DV].astype(F32)
        st = rst_ref[h]
        s = _dot_nt(q, k) * dec_ref[h]
        inner = _dot(s.astype(BF16), v)
        cross = _dot((q.astype(F32) * xi_ref[h]).astype(BF16), st.astype(BF16))
        kv = _dot_tn((k.astype(F32) * zeta_ref[h]).astype(BF16), v)
        rst_ref[h] = st * chunk_decay[h] + kv
        o = _rms(inner + cross) * rn_ref[:, h * RET_DV:(h + 1) * RET_DV] * _silu(g)
        o_ref[pl.ds(r0, lr), h * RET_DV:(h + 1) * RET_DV] = o.astype(o_ref.dtype)

    def gla_chunk(r0, h):
        q = gq_ref[pl.ds(r0, lg), h * GLA_DK:(h + 1) * GLA_DK].astype(F32)
        k = gk_ref[pl.ds(r0, lg), h * GLA_DK:(h + 1) * GLA_DK].astype(F32)
        v = gv_ref[pl.ds(r0, lg), h * GLA_DV:(h + 1) * GLA_DV]
        g = gg_ref[pl.ds(r0, lg), h * GLA_DV:(h + 1) * GLA_DV].astype(F32)
        gl = gl_ref[pl.ds(r0, lg), h * GLA_DK:(h + 1) * GLA_DK]
        b = _dot_exact_rhs(tri, gl)
        bref = b[lg // 2:lg // 2 + 1, :]
        blast = b[lg - 1:lg, :]
        st = gst_ref[h]
        att = _dot_nt((q * jnp.exp(b - bref)).astype(BF16), (k * jnp.exp(bref - b)).astype(BF16))
        att = jnp.where(causal, att, 0.0)
        inner = _dot(att.astype(BF16), v)
        cross = _dot_nt((q * jnp.exp(b)).astype(BF16), st.astype(BF16))
        kvt = _dot_tn(v, (k * jnp.exp(blast - b)).astype(BF16))
        gst_ref[h] = st * jnp.exp(blast) + kvt
        o = _rms(inner + cross) * gn_ref[:, h * GLA_DV:(h + 1) * GLA_DV] * _silu(g)
        o_ref[pl.ds(r0, lg), base + h * GLA_DV:base + (h + 1) * GLA_DV] = o.astype(o_ref.dtype)

    def step(c, carry):
        for u in range(RET_GLA_UNROLL):
            r0 = pl.multiple_of((c * RET_GLA_UNROLL + u) * lr, lr)
            for h in range(RET_HEADS):
                ret_chunk(r0, h)
            for j in range(lr // lg):
                for h in range(GLA_HEADS):
                    gla_chunk(pl.multiple_of(r0 + j * lg, lg), h)
        return carry

    lax.fori_loop(0, ls // (lr * RET_GLA_UNROLL), step, 0)


def _ret_gla(y, glog, ret_norm, gla_norm, bsz, seq, ls):
    t = y.shape[0]
    ns = seq // ls
    qw, vw = RET_HEADS * RET_DK, RET_HEADS * RET_DV
    L = RET_CHUNK
    log_gamma = jnp.log1p(-jnp.exp2(-5.0 - jnp.arange(RET_HEADS, dtype=F32)))
    idx = jnp.arange(L, dtype=F32)
    diff = idx[:, None] - idx[None, :]
    decay = jnp.where(diff >= 0, jnp.exp(log_gamma[:, None, None] * jnp.maximum(diff, 0.0)), 0.0)
    zeta = jnp.broadcast_to(jnp.exp(log_gamma[:, None] * (L - 1 - idx)[None, :])[:, :, None], (RET_HEADS, L, RET_DK))
    xi = jnp.broadcast_to(jnp.exp(log_gamma[:, None] * (idx + 1.0)[None, :])[:, :, None], (RET_HEADS, L, RET_DK))
    chunk_decay = tuple(float(np.exp(np.float32(np.log1p(-np.exp2(np.float32(-5.0 - i)))) * np.float32(L)))
                        for i in range(RET_HEADS))
    rowmap = lambda j: (lambda b, i: (b * ns + i, j))
    const = lambda b, i: (0, 0)
    const3 = lambda b, i: (0, 0, 0)
    return pl.pallas_call(
        functools.partial(_ret_gla_kernel, chunk_decay=chunk_decay),
        out_shape=jax.ShapeDtypeStruct((t, 2 * vw), BF16),
        grid=(bsz, ns),
        in_specs=[pl.BlockSpec((ls, qw), rowmap(0)),
                  pl.BlockSpec((ls, qw), rowmap(1)),
                  pl.BlockSpec((ls, vw), rowmap(1)),
                  pl.BlockSpec((ls, vw), rowmap(2)),
                  pl.BlockSpec((ls, qw), rowmap(6)),
                  pl.BlockSpec((ls, qw), rowmap(7)),
                  pl.BlockSpec((ls, vw), rowmap(4)),
                  pl.BlockSpec((ls, vw), rowmap(5)),
                  pl.BlockSpec((ls, qw), rowmap(0)),
                  pl.BlockSpec((1, vw), const),
                  pl.BlockSpec((1, vw), const),
                  pl.BlockSpec((RET_HEADS, L, L), const3),
                  pl.BlockSpec((RET_HEADS, L, RET_DK), const3),
                  pl.BlockSpec((RET_HEADS, L, RET_DK), const3)],
        out_specs=pl.BlockSpec((ls, 2 * vw), lambda b, i: (b * ns + i, 0)),
        scratch_shapes=[pltpu.VMEM((RET_HEADS, RET_DK, RET_DV), F32),
                        pltpu.VMEM((GLA_HEADS, GLA_DV, GLA_DK), F32)],
        compiler_params=_cparams(("arbitrary", "arbitrary")),
        name="ret_gla_mixer",
    )(y, y, y, y, y, y, y, y, glog, ret_norm, gla_norm, decay, zeta, xi)


def _odd_proj_kernel(x_ref, mod_ref, gain_ref, w_ref, wdt_ref, dtb_ref, y_ref, dt_ref, *, ncol, cch):
    h = _norm_mod(x_ref[...], gain_ref[...], mod_ref[0], 0, 1)
    hb = h.astype(BF16)
    width = w_ref.shape[1] // ncol
    for j in range(ncol):
        c0 = j * width
        acc = _dot(hb, w_ref[:, c0:c0 + width])
        if c0 >= cch:
            acc = _silu(acc)
        y_ref[:, c0:c0 + width] = acc.astype(y_ref.dtype)
    z = _dot(hb, wdt_ref[...]) + dtb_ref[...]
    dt_ref[...] = jnp.maximum(z, 0.0) + jnp.log1p(jnp.exp(-jnp.abs(z)))


def _odd_proj(x2d, mod_l, gain, w_main, w_dt, dt_bias, cch, tm, seq):
    t, d = x2d.shape
    n = w_main.shape[1]
    tiles_per_b = seq // tm
    const = lambda i: (0, 0)
    return pl.pallas_call(
        functools.partial(_odd_proj_kernel, ncol=n // 512, cch=cch),
        out_shape=(jax.ShapeDtypeStruct((t, n), BF16), jax.ShapeDtypeStruct((t, LANES), F32)),
        grid=(t // tm,),
        in_specs=[pl.BlockSpec((tm, d), lambda i: (i, 0)),
                  pl.BlockSpec((1, 6, d), lambda i: (i // tiles_per_b, 0, 0)),
                  pl.BlockSpec((1, d), const),
                  pl.BlockSpec((d, n), const),
                  pl.BlockSpec((d, LANES), const),
                  pl.BlockSpec((1, LANES), const)],
        out_specs=(pl.BlockSpec((tm, n), lambda i: (i, 0)), pl.BlockSpec((tm, LANES), lambda i: (i, 0))),
        compiler_params=_cparams(("arbitrary",)),
        name="odd_proj",
    )(x2d, mod_l, gain, w_main, w_dt, dt_bias)


def _ssd_kernel(z_ref, xbc_ref, dt_ref, cw_ref, cb_ref, alog_ref, dsk_ref, ng_ref, o_ref, hist_ref, st_ref):
    @pl.when(pl.program_id(1) == 0)
    def _():
        hist_ref[...] = jnp.zeros_like(hist_ref)
        st_ref[...] = jnp.zeros_like(st_ref)

    L = xbc_ref.shape[0]
    P, N, G = SSD_HEADDIM, SSD_STATE, SSD_GROUPS
    dinner = z_ref.shape[1]
    gw = dinner // G
    hpg = gw // P
    slab = gw + 2 * N
    hrows = hist_ref.shape[0]

    dt = dt_ref[...]
    da = dt * (-jnp.exp(alog_ref[...]))
    trow = _iota((L, L), 0)
    tcol = _iota((L, L), 1)
    causal = trow >= tcol
    cum = _dot_exact_rhs(causal.astype(BF16), da)
    cum_t = jnp.transpose(cum)
    dt_t = jnp.transpose(dt)
    ecum = jnp.exp(cum)
    wend = dt * jnp.exp(cum[L - 1:L, :] - cum)
    lane = _iota((L, LANES), 1)
    lo_half = lane < P

    def pair_cols(a, ha):
        return jnp.where(lo_half, jnp.broadcast_to(a[:, ha:ha + 1], (L, LANES)),
                         jnp.broadcast_to(a[:, ha + 1:ha + 2], (L, LANES)))

    srow = _iota((L, hrows + L), 0)
    scol = _iota((L, hrows + L), 1)
    shift = jnp.concatenate([(scol == srow + (hrows - s)).astype(BF16) for s in range(1, SSD_CONV)], axis=0)

    for g in range(G):
        c0 = g * slab
        xin = xbc_ref[:, c0:c0 + slab]
        xcat = jnp.concatenate([hist_ref[:, c0:c0 + slab], xin], axis=0)
        hist_ref[:, c0:c0 + slab] = xin[L - hrows:L]
        sh = _dot(shift, xcat)
        conv = xin.astype(F32) * cw_ref[SSD_CONV - 1:SSD_CONV, c0:c0 + slab] + cb_ref[:, c0:c0 + slab]
        for s in range(1, SSD_CONV):
            conv = conv + sh[(s - 1) * L:s * L] * cw_ref[SSD_CONV - 1 - s:SSD_CONV - s, c0:c0 + slab]
        act = _silu(conv)
        xs = act[:, :gw]
        bm_g = act[:, gw:gw + N].astype(BF16)
        cm_g = act[:, gw + N:].astype(BF16)

        cb = _dot_nt(cm_g, bm_g)
        xs_b = xs.astype(BF16)
        pieces, xw, ec = [], [], []
        for pr in range(hpg // 2):
            ha = g * hpg + 2 * pr
            ms = []
            for hh in (ha, ha + 1):
                seg = jnp.broadcast_to(cum[:, hh:hh + 1], (L, L)) - cum_t[hh:hh + 1, :]
                ms.append((cb * jnp.where(causal, jnp.exp(seg), 0.0) * dt_t[hh:hh + 1, :]).astype(BF16))
            lhs = jnp.concatenate(ms, axis=1)
            blk = xs_b[:, pr * 2 * P:(pr + 1) * 2 * P]
            zero = jnp.zeros_like(blk)
            rhs = jnp.concatenate([jnp.where(lo_half, blk, zero), jnp.where(lo_half, zero, blk)], axis=0)
            pieces.append(_dot(lhs, rhs))
            xw.append((xs[:, pr * 2 * P:(pr + 1) * 2 * P] * pair_cols(wend, ha)).astype(BF16))
            ec.append(pair_cols(ecum, ha))
        inner = jnp.concatenate(pieces, axis=1)
        ecum_g = jnp.concatenate(ec, axis=1)
        st = st_ref[g]
        cross = _dot(cm_g, st.astype(BF16)) * ecum_g
        st_ref[g] = st * ecum_g[L - 1:L, :] + _dot_tn(bm_g, jnp.concatenate(xw, axis=1))
        y = inner + cross + dsk_ref[:, g * gw:(g + 1) * gw] * xs
        y = y * z_ref[:, g * gw:(g + 1) * gw].astype(F32)
        o_ref[:, g * gw:(g + 1) * gw] = (_rms(y) * ng_ref[:, g * gw:(g + 1) * gw]).astype(o_ref.dtype)


SSD_HIST_ROWS = 16


def _ssd(y, dt, conv_w, conv_b, a_log_row, d_row, norm_g, bsz, seq):
    t = y.shape[0]
    L = SSD_CHUNK
    ns = seq // L
    dinner = norm_g.shape[1]
    cch = conv_w.shape[1]
    gw = dinner // SSD_GROUPS
    const = lambda b, i: (0, 0)
    assert cch % dinner == 0
    return pl.pallas_call(
        _ssd_kernel,
        out_shape=jax.ShapeDtypeStruct((t, dinner), BF16),
        grid=(bsz, ns),
        in_specs=[pl.BlockSpec((L, dinner), lambda b, i: (b * ns + i, cch // dinner)),
                  pl.BlockSpec((L, cch), lambda b, i: (b * ns + i, 0)),
                  pl.BlockSpec((L, LANES), lambda b, i: (b * ns + i, 0)),
                  pl.BlockSpec((SSD_CONV, cch), const),
                  pl.BlockSpec((1, cch), const),
                  pl.BlockSpec((1, LANES), const),
                  pl.BlockSpec((1, dinner), const),
                  pl.BlockSpec((1, dinner), const)],
        out_specs=pl.BlockSpec((L, dinner), lambda b, i: (b * ns + i, 0)),
        scratch_shapes=[pltpu.VMEM((SSD_HIST_ROWS, cch), BF16),
                        pltpu.VMEM((SSD_GROUPS, SSD_STATE, gw), F32)],
        compiler_params=_cparams(("arbitrary", "arbitrary")),
        name="ssd_mixer",
    )(y, y, dt, conv_w, conv_b, a_log_row, d_row, norm_g)


def _outproj_kernel(o_ref, x_ref, mod_ref, gain_ref, w_ref, x1_ref, h2_ref):
    mod = mod_ref[0]
    x1 = x_ref[...] + mod[2:3, :] * _dot(o_ref[...], w_ref[...])
    x1_ref[...] = x1
    h2_ref[...] = _norm_mod(x1, gain_ref[...], mod, 3, 4)


def _outproj(o, x2d, mod_l, gain2, w_out, tm, seq):
    t, d = x2d.shape
    kin = o.shape[1]
    tiles_per_b = seq // tm
    const = lambda i: (0, 0)
    return pl.pallas_call(
        _outproj_kernel,
        out_shape=(jax.ShapeDtypeStruct((t, d), F32), jax.ShapeDtypeStruct((t, d), F32)),
        grid=(t // tm,),
        in_specs=[pl.BlockSpec((tm, kin), lambda i: (i, 0)),
                  pl.BlockSpec((tm, d), lambda i: (i, 0)),
                  pl.BlockSpec((1, 6, d), lambda i: (i // tiles_per_b, 0, 0)),
                  pl.BlockSpec((1, d), const),
                  pl.BlockSpec((kin, d), const)],
        out_specs=(pl.BlockSpec((tm, d), lambda i: (i, 0)), pl.BlockSpec((tm, d), lambda i: (i, 0))),
        compiler_params=_cparams(("arbitrary",)),
        name="out_proj",
    )(o, x2d, mod_l, gain2, w_out)


def _route_kernel(h_ref, rw_ref, rb_ref, dest_ref, gate_ref, blk_ref, cnt_ref, pst_ref, *, nbpad):
    phase = pl.program_id(0)
    step = pl.program_id(1)
    tm = h_ref.shape[0]
    ne = rw_ref.shape[0]
    epg = ne // N_GROUPS

    @pl.when(jnp.logical_and(phase == 0, step == 0))
    def _():
        cnt_ref[...] = jnp.zeros_like(cnt_ref)
        pst_ref[...] = jnp.zeros_like(pst_ref)

    @pl.when(jnp.logical_and(phase == 1, step == 0))
    def _():
        nblk = jnp.floor((cnt_ref[...] + (MOE_BLOCK - 1.0)) * (1.0 / MOE_BLOCK))
        hi = jnp.floor(nblk * (1.0 / 16.0))
        lo = nblk - hi * 16.0
        er = _iota((ne, ne), 0)
        ec = _iota((ne, ne), 1)
        lower = (ec < er).astype(BF16)
        pst = _dot(lower, hi.astype(BF16)) * 16.0 + _dot(lower, lo.astype(BF16))
        pst_ref[...] = pst
        pend = jnp.concatenate([pst + nblk] * (nbpad // LANES), axis=1)
        bidx = _iota((ne, nbpad), 1).astype(F32)
        be = jnp.sum(jnp.where(pend <= bidx, 1.0, 0.0), axis=0, keepdims=True)
        total = jnp.max(pend, axis=0, keepdims=True)
        blk_ref[0:1, :] = jnp.minimum(be, ne - 1.0).astype(I32)
        blk_ref[1:2, :] = total.astype(I32)
        cnt_ref[...] = jnp.zeros_like(cnt_ref)

    logits = _dot_nt(rw_ref[...], h_ref[...].astype(BF16))
    s = jax.nn.sigmoid(logits)
    sb = s + jnp.concatenate([rb_ref[...]] * (tm // LANES), axis=1)
    G = N_GROUPS
    v = [sb[j * G:(j + 1) * G, :] for j in range(epg)]
    m01, n01 = jnp.maximum(v[0], v[1]), jnp.minimum(v[0], v[1])
    m23, n23 = jnp.maximum(v[2], v[3]), jnp.minimum(v[2], v[3])
    top1 = jnp.maximum(m01, m23)
    top2 = jnp.maximum(jnp.minimum(m01, m23), jnp.maximum(n01, n23))
    gscore = top1 + top2
    gi = _iota((G, tm), 0)
    gmax = jnp.max(gscore, axis=0, keepdims=True)
    gidx = jnp.min(jnp.where(gscore == gmax, gi, G), axis=0, keepdims=True)
    sel = gi == gidx
    vb = [jnp.sum(jnp.where(sel, v[j], 0.0), axis=0, keepdims=True) for j in range(epg)]
    vs = [jnp.sum(jnp.where(sel, s[j * G:(j + 1) * G, :], 0.0), axis=0, keepdims=True) for j in range(epg)]
    zero = jnp.zeros_like(vb[0])
    e1 = zero
    e2 = zero
    w1 = zero
    w2 = zero
    for j in range(epg):
        rank = zero
        for i in range(epg):
            if i == j:
                continue
            ahead = (vb[i] >= vb[j]) if i < j else (vb[i] > vb[j])
            rank = rank + jnp.where(ahead, 1.0, 0.0)
        first = rank == 0.0
        second = rank == 1.0
        e1 = e1 + jnp.where(first, float(j), 0.0)
        e2 = e2 + jnp.where(second, float(j), 0.0)
        w1 = w1 + jnp.where(first, vs[j], 0.0)
        w2 = w2 + jnp.where(second, vs[j], 0.0)
    gf = gidx.astype(F32) * float(epg)
    e1 = e1 + gf
    e2 = e2 + gf
    wsum = w1 + w2
    ei = _iota((ne, tm), 0).astype(F32)
    oh1 = ei == e1
    oh2 = ei == e2
    onehot = jnp.where(oh1 | oh2, 1.0, 0.0)
    upper = (_iota((tm, tm), 0) < _iota((tm, tm), 1)).astype(BF16)
    prefix = _dot(onehot.astype(BF16), upper)
    cnt = cnt_ref[...]
    base = jnp.concatenate([cnt] * (tm // LANES), axis=1) + prefix
    pst = jnp.concatenate([pst_ref[...]] * (tm // LANES), axis=1) * float(MOE_BLOCK)
    slot = base + pst
    d1 = jnp.sum(jnp.where(oh1, slot, 0.0), axis=0, keepdims=True)
    d2 = jnp.sum(jnp.where(oh2, slot, 0.0), axis=0, keepdims=True)
    cnt_ref[...] = cnt + jnp.sum(onehot, axis=1, keepdims=True)
    dest_ref[0, 0:1, :] = d1.astype(I32)
    dest_ref[0, 1:2, :] = d2.astype(I32)
    gate_ref[0, 0:1, :] = w1 / wsum
    gate_ref[0, 1:2, :] = w2 / wsum


def _route(h2, rw_t, rb_b, tm, nbpad):
    t, d = h2.shape
    ne = rw_t.shape[0]
    nt = t // tm
    return pl.pallas_call(
        functools.partial(_route_kernel, nbpad=nbpad),
        out_shape=(jax.ShapeDtypeStruct((nt, 2, tm), I32), jax.ShapeDtypeStruct((nt, 2, tm), F32),
                   jax.ShapeDtypeStruct((2, nbpad), I32)),
        grid=(2, nt),
        in_specs=[pl.BlockSpec((tm, d), lambda p, i: (i, 0)),
                  pl.BlockSpec((ne, d), lambda p, i: (0, 0)),
                  pl.BlockSpec((ne, LANES), lambda p, i: (0, 0))],
        out_specs=(pl.BlockSpec((1, 2, tm), lambda p, i: (i * p, 0, 0)),
                   pl.BlockSpec((1, 2, tm), lambda p, i: (i * p, 0, 0)),
                   pl.BlockSpec((2, nbpad), lambda p, i: (0, 0))),
        scratch_shapes=[pltpu.VMEM((ne, LANES), F32), pltpu.VMEM((ne, LANES), F32)],
        compiler_params=_cparams(("arbitrary", "arbitrary")),
        name="router",
    )(h2, rw_t, rb_b)


def _dispatch_kernel(dest_ref, h_ref, xs_in_ref, xs_ref, sem):
    del xs_in_ref
    tg = h_ref.shape[0]

    def issue(t, carry):
        for k in range(TOP_K):
            d = dest_ref[0, 0, k * tg + t]
            pltpu.make_async_copy(h_ref.at[pl.ds(t, 1)], xs_ref.at[pl.ds(d, 1)], sem).start(priority=k % 2)
        return carry

    lax.fori_loop(0, tg, issue, 0)
    for k in range(TOP_K):
        pltpu.make_async_copy(h_ref, xs_ref.at[pl.ds(0, tg)], sem).wait()


def _dispatch(h2, dest_flat, np_rows, tg):
    t, d = h2.shape
    nt = t // tg
    xs0 = jnp.zeros((np_rows, d), h2.dtype)
    return pl.pallas_call(
        _dispatch_kernel,
        out_shape=jax.ShapeDtypeStruct((np_rows, d), h2.dtype),
        grid=(nt,),
        in_specs=[pl.BlockSpec((1, 1, TOP_K * tg), lambda i: (i, 0, 0), memory_space=pltpu.SMEM),
                  pl.BlockSpec((tg, d), lambda i: (i, 0)),
                  pl.BlockSpec(memory_space=pl.ANY)],
        out_specs=pl.BlockSpec(memory_space=pl.ANY),
        scratch_shapes=[pltpu.SemaphoreType.DMA(())],
        input_output_aliases={2: 0},
        compiler_params=_cparams(("arbitrary",)),
        name="moe_dispatch",
    )(dest_flat, h2, xs0)


def _expert_kernel(blk_ref, nused_ref, xs_ref, wg_ref, wu_ref, wd_ref, y_ref, wgb, wub, wdb):
    b = pl.program_id(0)
    prev = blk_ref[jnp.maximum(b - 1, 0)]
    changed = jnp.logical_or(b == 0, blk_ref[b] != prev)

    @pl.when(changed)
    def _():
        wgb[...] = wg_ref[0, 0].astype(BF16)
        wub[...] = wu_ref[0, 0].astype(BF16)
        wdb[...] = wd_ref[0, 0].astype(BF16)

    @pl.when(b < nused_ref[0])
    def _():
        xb = xs_ref[...].astype(BF16)
        g = _dot(xb, wgb[...])
        u = _dot(xb, wub[...])
        a = (_silu(g) * u).astype(BF16)
        y_ref[...] = _dot(a, wdb[...]).astype(y_ref.dtype)

    @pl.when(b >= nused_ref[0])
    def _():
        y_ref[...] = jnp.zeros_like(y_ref)


def _experts(xs, blk_e, nused, w_gate, w_up, w_down, layer):
    np_rows, d = xs.shape
    _, ne, _, ff = w_gate.shape
    nb = np_rows // MOE_BLOCK
    grid_spec = pltpu.PrefetchScalarGridSpec(
        num_scalar_prefetch=2,
        grid=(nb,),
        in_specs=[pl.BlockSpec((MOE_BLOCK, d), lambda b, be, nu: (b, 0)),
                  pl.BlockSpec((1, 1, d, ff), lambda b, be, nu: (layer, be[b], 0, 0)),
                  pl.BlockSpec((1, 1, d, ff), lambda b, be, nu: (layer, be[b], 0, 0)),
                  pl.BlockSpec((1, 1, ff, d), lambda b, be, nu: (layer, be[b], 0, 0))],
        out_specs=pl.BlockSpec((MOE_BLOCK, d), lambda b, be, nu: (b, 0)),
        scratch_shapes=[pltpu.VMEM((d, ff), BF16), pltpu.VMEM((d, ff), BF16), pltpu.VMEM((ff, d), BF16)],
    )
    return pl.pallas_call(
        _expert_kernel,
        out_shape=jax.ShapeDtypeStruct((np_rows, d), F32),
        grid_spec=grid_spec,
        compiler_params=_cparams(("arbitrary",)),
        name="moe_experts",
    )(blk_e, nused, xs, w_gate, w_up, w_down)


def _combine_kernel(dest_ref, x1_ref, gate_ref, mod_ref, fg_ref, ys_ref, o_ref, ybuf, sem, *, final):
    tc = x1_ref.shape[0]
    d = x1_ref.shape[1]

    def issue(t, carry):
        for k in range(TOP_K):
            src = dest_ref[0, 0, k * tc + t]
            pltpu.make_async_copy(ys_ref.at[pl.ds(src, 1)], ybuf.at[k, pl.ds(t, 1)], sem).start(priority=k % 2)
        return carry

    lax.fori_loop(0, tc, issue, 0)
    for k in range(TOP_K):
        pltpu.make_async_copy(ys_ref.at[pl.ds(0, tc)], ybuf.at[k], sem).wait()

    gates = gate_ref[0]
    acc = jnp.zeros((tc, d), F32)
    for k in range(TOP_K):
        gcol = jnp.transpose(jnp.broadcast_to(gates[k:k + 1, :], (LANES, tc)))
        acc = acc + ybuf[k] * jnp.concatenate([gcol] * (d // LANES), axis=1)
    x2 = x1_ref[...] + mod_ref[0][5:6, :] * acc
    if final:
        x2 = _rms(x2) * fg_ref[...]
    o_ref[...] = x2


def _combine(x1, ys, dest_flat, gates, mod_l, final_gain, tc, seq, final):
    t, d = x1.shape
    nt = t // tc
    tiles_per_b = seq // tc
    return pl.pallas_call(
        functools.partial(_combine_kernel, final=final),
        out_shape=jax.ShapeDtypeStruct((t, d), F32),
        grid=(nt,),
        in_specs=[pl.BlockSpec((1, 1, TOP_K * tc), lambda i: (i, 0, 0), memory_space=pltpu.SMEM),
                  pl.BlockSpec((tc, d), lambda i: (i, 0)),
                  pl.BlockSpec((1, TOP_K, tc), lambda i: (i, 0, 0)),
                  pl.BlockSpec((1, 6, d), lambda i: (i // tiles_per_b, 0, 0)),
                  pl.BlockSpec((1, d), lambda i: (0, 0)),
                  pl.BlockSpec(memory_space=pl.ANY)],
        out_specs=pl.BlockSpec((tc, d), lambda i: (i, 0)),
        scratch_shapes=[pltpu.VMEM((TOP_K, tc, d), F32), pltpu.SemaphoreType.DMA(())],
        compiler_params=_cparams(("arbitrary",)),
        name="moe_combine",
    )(dest_flat, x1, gates, mod_l, final_gain, ys)


ROUTE_TILE = 512


def _moe(x1, h2, mod_l, rw_t, rb_b, w_gate, w_up, w_down, layer, final_gain, seq, final):
    t, d = x1.shape
    ne = w_gate.shape[1]
    nb = (t * TOP_K + ne * (MOE_BLOCK - 1) + MOE_BLOCK - 1) // MOE_BLOCK
    nbpad = -(-nb // LANES) * LANES
    tm = ROUTE_TILE
    dest, gates, blk = _route(h2, rw_t, rb_b, tm, nbpad)
    dest_flat = dest.reshape(t // tm, 1, TOP_K * tm)
    xs = _dispatch(h2, dest_flat, nb * MOE_BLOCK, tm)
    ys = _experts(xs, blk[0, :nb], blk[1, :1], w_gate, w_up, w_down, layer)
    return _combine(x1, ys, dest_flat, gates, mod_l, final_gain, tm, seq, final)


def kernel(x, c, positions, mod_w, mod_b, norm_mix, norm_ffn, ev_w_in, ev_ret_norm, ev_gla_wa2, ev_gla_ba,
           ev_gla_norm, ev_w_out, od_w_in, od_conv_w, od_conv_b, od_dt_bias, od_a_log, od_d, od_norm, od_w_out,
           router_w, router_b, exp_w_gate, exp_w_up, exp_w_down, final_norm):
    bsz, seq, d = x.shape
    depth = mod_w.shape[0]
    t = bsz * seq
    ne = router_w.shape[1]
    epg = ne // N_GROUPS
    mod = _modulation(c, mod_w, mod_b)
    x2d = x.reshape(t, d)

    perm = np.array([g * epg + j for j in range(epg) for g in range(N_GROUPS)])
    rw_t = router_w.T[perm].astype(BF16)
    rb_b = jnp.broadcast_to(router_b.astype(F32)[perm][:, None], (ne, LANES))
    final_gain = final_norm.reshape(1, d)

    half = RET_DK // 2
    inv = ROPE_BASE ** (-jnp.arange(half, dtype=F32) / half)
    inv2 = jnp.concatenate([inv, inv]).reshape(1, LANES)
    pos_col = positions.reshape(t, 1)

    for layer in range(depth):
        i = layer // 2
        mod_l = mod[layer]
        gain1 = norm_mix[layer].reshape(1, d)
        gain2 = norm_ffn[layer].reshape(1, d)
        if layer % 2 == 0:
            w_in = ev_w_in[i]
            nmain = w_in.shape[1] - GLA_RANK
            w_main = w_in[:, :nmain].astype(BF16)
            w_ga = jnp.pad(w_in[:, nmain:], ((0, 0), (0, LANES - GLA_RANK))).astype(BF16)
            wa2 = jnp.pad(ev_gla_wa2[i], ((0, LANES - GLA_RANK), (0, 0))).astype(BF16)
            y, glog = _even_proj(x2d, mod_l, gain1, pos_col, inv2, w_main, w_ga, wa2,
                                 ev_gla_ba[i].reshape(1, -1), 512, seq)
            o = _ret_gla(y, glog, ev_ret_norm[i].reshape(1, -1), ev_gla_norm[i].reshape(1, -1), bsz, seq, 512)
            w_out = ev_w_out[i].astype(BF16)
        else:
            w_in = od_w_in[i]
            nheads = od_a_log.shape[1]
            nmain = w_in.shape[1] - nheads
            dinner = od_norm.shape[1]
            gw, ns_ = dinner // SSD_GROUPS, SSD_STATE
            bc0 = dinner + SSD_GROUPS * ns_
            cperm = np.concatenate([np.concatenate([np.arange(g * gw, (g + 1) * gw),
                                                    dinner + np.arange(g * ns_, (g + 1) * ns_),
                                                    bc0 + np.arange(g * ns_, (g + 1) * ns_)])
                                    for g in range(SSD_GROUPS)])
            conv_w = od_conv_w[i][:, cperm]
            conv_b = od_conv_b[i][cperm].reshape(1, -1)
            w_main = jnp.concatenate([w_in[:, dinner + cperm], w_in[:, :dinner]], axis=1).astype(BF16)
            w_dt = jnp.pad(w_in[:, nmain:], ((0, 0), (0, LANES - nheads))).astype(BF16)
            dtb = jnp.pad(od_dt_bias[i], (0, LANES - nheads)).reshape(1, LANES)
            y, dt = _odd_proj(x2d, mod_l, gain1, w_main, w_dt, dtb, od_conv_w.shape[2], 512, seq)
            alog = jnp.pad(od_a_log[i], (0, LANES - nheads)).reshape(1, LANES)
            d_row = jnp.repeat(od_d[i], SSD_HEADDIM).reshape(1, dinner)
            o = _ssd(y, dt, conv_w, conv_b, alog, d_row, od_norm[i].reshape(1, dinner), bsz, seq)
            w_out = od_w_out[i].astype(BF16)
        x1, h2 = _outproj(o, x2d, mod_l, gain2, w_out, 512, seq)
        x2d = _moe(x1, h2, mod_l, rw_t, rb_b, exp_w_gate, exp_w_up, exp_w_down, layer,
                   final_gain, seq, final=(layer == depth - 1))
    return x2d.reshape(bsz, seq, d)
```

```python
import functools

import jax
import jax.numpy as jnp
import numpy as np
from jax import lax
from jax.experimental import pallas as pl
from jax.experimental.pallas import tpu as pltpu
from jax.experimental.pallas import tpu_sc as plsc

F32 = jnp.float32
BF16 = jnp.bfloat16
I32 = jnp.int32

RET_HEADS, RET_DK, RET_DV, RET_CHUNK = 4, 128, 256, 128
ROPE_BASE = 10000.0
GLA_HEADS, GLA_DK, GLA_DV, GLA_RANK, GLA_TAU, GLA_CHUNK = 4, 128, 256, 16, 16.0, 64
SSD_HEADDIM, SSD_GROUPS, SSD_STATE, SSD_CONV, SSD_CHUNK = 64, 8, 128, 4, 128
N_GROUPS, TOP_K, GROUP_SCORE_K, MOE_BLOCK = 8, 2, 2, 256
NORM_EPS = 1e-6

LANES = 128
VMEM_LIMIT = 56 * 1024 * 1024


def _cparams(sem, vmem=VMEM_LIMIT):
    return pltpu.CompilerParams(dimension_semantics=sem, vmem_limit_bytes=vmem)


def _dot(a, b):
    return jnp.dot(a, b, preferred_element_type=F32)


def _dot_nt(a, b):
    return lax.dot_general(a, b, (((1,), (1,)), ((), ())), preferred_element_type=F32)


def _dot_tn(a, b):
    return lax.dot_general(a, b, (((0,), (0,)), ((), ())), preferred_element_type=F32)


def _split3(x):
    a = x.astype(BF16)
    r = x - a.astype(F32)
    b = r.astype(BF16)
    c = (r - b.astype(F32)).astype(BF16)
    return a, b, c


def _dot_exact_rhs(m01, x):
    a, b, c = _split3(x)
    return _dot(m01, a) + _dot(m01, b) + _dot(m01, c)


def _dot_exact_lhs(x, m01):
    a, b, c = _split3(x)
    return _dot(a, m01) + _dot(b, m01) + _dot(c, m01)


def _dot_tn_exact(x, m01):
    a, b, c = _split3(x)
    return _dot_tn(a, m01) + _dot_tn(b, m01) + _dot_tn(c, m01)


def _silu(x):
    return x * jax.nn.sigmoid(x)


def _rms(x, eps=NORM_EPS):
    return x * lax.rsqrt(jnp.mean(x * x, axis=-1, keepdims=True) + eps)


def _iota(shape, dim):
    return lax.broadcasted_iota(I32, shape, dim)


def _mod_kernel(c_ref, w_ref, b_ref, o_ref):
    sc = _silu(c_ref[...])
    o_ref[0] = _dot(sc.astype(BF16), w_ref[0].astype(BF16)) + b_ref[0]


def _modulation(c, mod_w, mod_b):
    depth, d, d6 = mod_w.shape
    bsz = c.shape[0]
    nb = d6 // d
    out = pl.pallas_call(
        _mod_kernel,
        out_shape=jax.ShapeDtypeStruct((depth, bsz, d6), F32),
        grid=(depth, nb),
        in_specs=[pl.BlockSpec((bsz, d), lambda l, j: (0, 0)),
                  pl.BlockSpec((1, d, d), lambda l, j: (l, 0, j)),
                  pl.BlockSpec((1, 1, d), lambda l, j: (l, 0, j))],
        out_specs=pl.BlockSpec((1, bsz, d), lambda l, j: (l, 0, j)),
        compiler_params=_cparams(("arbitrary", "arbitrary")),
        name="adaln_mod",
    )(c, mod_w, mod_b.reshape(depth, 1, d6))
    return out.reshape(depth, bsz, nb, d)


def _norm_mod(x, gain, mod, shift_row, scale_row):
    y = _rms(x) * gain
    return y * (1.0 + mod[scale_row:scale_row + 1, :]) + mod[shift_row:shift_row + 1, :]


def _even_proj_kernel(x_ref, mod_ref, gain_ref, pos_ref, inv_ref, w_ref, wga_ref, wa2_ref, ba_ref,
                      y_ref, glog_ref, *, ncol):
    h = _norm_mod(x_ref[...], gain_ref[...], mod_ref[0], 0, 1)
    hb = h.astype(BF16)
    tm = hb.shape[0]
    ang = pos_ref[...].astype(F32) * inv_ref[...]
    cos = jnp.cos(ang)
    lane = _iota((tm, LANES), 1)
    sin = jnp.where(lane < LANES // 2, -jnp.sin(ang), jnp.sin(ang))
    qk_scale = RET_DK ** -0.5
    rqk = RET_HEADS * RET_DK
    rv = RET_HEADS * RET_DV
    gq0 = 2 * rqk + 2 * rv
    width = w_ref.shape[1] // ncol
    for j in range(ncol):
        c0 = j * width
        acc = _dot(hb, w_ref[:, c0:c0 + width])
        for s in range(width // LANES):
            col = c0 + s * LANES
            blk = acc[:, s * LANES:(s + 1) * LANES]
            if col < 2 * rqk:
                blk = blk * cos + pltpu.roll(blk, LANES // 2, 1) * sin
                if col >= rqk:
                    blk = blk * qk_scale
            elif gq0 <= col < gq0 + GLA_HEADS * GLA_DK:
                blk = blk * (GLA_DK ** -0.5)
            y_ref[:, col:col + LANES] = blk.astype(y_ref.dtype)
    ga = _dot(hb, wga_ref[...])
    z = _dot(ga.astype(BF16), wa2_ref[...]) + ba_ref[...]
    logsig = jnp.minimum(z, 0.0) - jnp.log1p(jnp.exp(-jnp.abs(z)))
    glog_ref[...] = logsig * (1.0 / GLA_TAU)


def _even_proj(x2d, mod_l, gain, pos_col, inv2, w_main, w_ga, wa2, ba, tm, seq):
    t, d = x2d.shape
    n = w_main.shape[1]
    tiles_per_b = seq // tm
    gk = wa2.shape[1]
    const = lambda i: (0, 0)
    return pl.pallas_call(
        functools.partial(_even_proj_kernel, ncol=n // 512),
        out_shape=(jax.ShapeDtypeStruct((t, n), BF16), jax.ShapeDtypeStruct((t, gk), F32)),
        grid=(t // tm,),
        in_specs=[pl.BlockSpec((tm, d), lambda i: (i, 0)),
                  pl.BlockSpec((1, 6, d), lambda i: (i // tiles_per_b, 0, 0)),
                  pl.BlockSpec((1, d), const),
                  pl.BlockSpec((tm, 1), lambda i: (i, 0)),
                  pl.BlockSpec((1, LANES), const),
                  pl.BlockSpec((d, n), const),
                  pl.BlockSpec((d, LANES), const),
                  pl.BlockSpec((LANES, gk), const),
                  pl.BlockSpec((1, gk), const)],
        out_specs=(pl.BlockSpec((tm, n), lambda i: (i, 0)), pl.BlockSpec((tm, gk), lambda i: (i, 0))),
        compiler_params=_cparams(("arbitrary",)),
        name="even_proj",
    )(x2d, mod_l, gain, pos_col, inv2, w_main, w_ga, wa2, ba)


RET_GLA_UNROLL = 2


def _ret_gla_kernel(rq_ref, rk_ref, rv_ref, rg_ref, gq_ref, gk_ref, gv_ref, gg_ref, gl_ref,
                    rn_ref, gn_ref, dec_ref, zeta_ref, xi_ref, o_ref, rst_ref, gst_ref, *, chunk_decay):
    @pl.when(pl.program_id(1) == 0)
    def _():
        rst_ref[...] = jnp.zeros_like(rst_ref)
        gst_ref[...] = jnp.zeros_like(gst_ref)

    ls = rq_ref.shape[0]
    lr, lg = RET_CHUNK, GLA_CHUNK
    trow = _iota((lg, lg), 0)
    tcol = _iota((lg, lg), 1)
    causal = trow >= tcol
    tri = causal.astype(BF16)
    base = RET_HEADS * RET_DV

    def ret_chunk(r0, h):
        q = rq_ref[pl.ds(r0, lr), h * RET_DK:(h + 1) * RET_DK]
        k = rk_ref[pl.ds(r0, lr), h * RET_DK:(h + 1) * RET_DK]
        v = rv_ref[pl.ds(r0, lr), h * RET_DV:(h + 1) * RET_DV]
        g = rg_ref[pl.ds(r0, lr), h * RET_DV:(h + 1) * RET_DV].astype(F32)
        st = rst_ref[h]
        s = _dot_nt(q, k) * dec_ref[h]
        inner = _dot(s.astype(BF16), v)
        cross = _dot((q.astype(F32) * xi_ref[h]).astype(BF16), st.astype(BF16))
        kv = _dot_tn((k.astype(F32) * zeta_ref[h]).astype(BF16), v)
        rst_ref[h] = st * chunk_decay[h] + kv
        o = _rms(inner + cross) * rn_ref[:, h * RET_DV:(h + 1) * RET_DV] * _silu(g)
        o_ref[pl.ds(r0, lr), h * RET_DV:(h + 1) * RET_DV] = o.astype(o_ref.dtype)

    def gla_chunk(r0, h):
        q = gq_ref[pl.ds(r0, lg), h * GLA_DK:(h + 1) * GLA_DK].astype(F32)
        k = gk_ref[pl.ds(r0, lg), h * GLA_DK:(h + 1) * GLA_DK].astype(F32)
        v = gv_ref[pl.ds(r0, lg), h * GLA_DV:(h + 1) * GLA_DV]
        g = gg_ref[pl.ds(r0, lg), h * GLA_DV:(h + 1) * GLA_DV].astype(F32)
        gl = gl_ref[pl.ds(r0, lg), h * GLA_DK:(h + 1) * GLA_DK]
        b = _dot_exact_rhs(tri, gl)
        bref = b[lg // 2:lg // 2 + 1, :]
        blast = b[lg - 1:lg, :]
        st = gst_ref[h]
        att = _dot_nt((q * jnp.exp(b - bref)).astype(BF16), (k * jnp.exp(bref - b)).astype(BF16))
        att = jnp.where(causal, att, 0.0)
        inner = _dot(att.astype(BF16), v)
        cross = _dot_nt((q * jnp.exp(b)).astype(BF16), st.astype(BF16))
        kvt = _dot_tn(v, (k * jnp.exp(blast - b)).astype(BF16))
        gst_ref[h] = st * jnp.exp(blast) + kvt
        o = _rms(inner + cross) * gn_ref[:, h * GLA_DV:(h + 1) * GLA_DV] * _silu(g)
        o_ref[pl.ds(r0, lg), base + h * GLA_DV:base + (h + 1) * GLA_DV] = o.astype(o_ref.dtype)

    def step(c, carry):
        for u in range(RET_GLA_UNROLL):
            r0 = pl.multiple_of((c * RET_GLA_UNROLL + u) * lr, lr)
            for h in range(RET_HEADS):
                ret_chunk(r0, h)
            for j in range(lr // lg):
                for h in range(GLA_HEADS):
                    gla_chunk(pl.multiple_of(r0 + j * lg, lg), h)
        return carry

    lax.fori_loop(0, ls // (lr * RET_GLA_UNROLL), step, 0)


def _ret_gla(y, glog, ret_norm, gla_norm, bsz, seq, ls):
    t = y.shape[0]
    ns = seq // ls
    qw, vw = RET_HEADS * RET_DK, RET_HEADS * RET_DV
    L = RET_CHUNK
    log_gamma = jnp.log1p(-jnp.exp2(-5.0 - jnp.arange(RET_HEADS, dtype=F32)))
    idx = jnp.arange(L, dtype=F32)
    diff = idx[:, None] - idx[None, :]
    decay = jnp.where(diff >= 0, jnp.exp(log_gamma[:, None, None] * jnp.maximum(diff, 0.0)), 0.0)
    zeta = jnp.broadcast_to(jnp.exp(log_gamma[:, None] * (L - 1 - idx)[None, :])[:, :, None], (RET_HEADS, L, RET_DK))
    xi = jnp.broadcast_to(jnp.exp(log_gamma[:, None] * (idx + 1.0)[None, :])[:, :, None], (RET_HEADS, L, RET_DK))
    chunk_decay = tuple(float(np.exp(np.float32(np.log1p(-np.exp2(np.float32(-5.0 - i)))) * np.float32(L)))
                        for i in range(RET_HEADS))
    rowmap = lambda j: (lambda b, i: (b * ns + i, j))
    const = lambda b, i: (0, 0)
    const3 = lambda b, i: (0, 0, 0)
    return pl.pallas_call(
        functools.partial(_ret_gla_kernel, chunk_decay=chunk_decay),
        out_shape=jax.ShapeDtypeStruct((t, 2 * vw), BF16),
        grid=(bsz, ns),
        in_specs=[pl.BlockSpec((ls, qw), rowmap(0)),
                  pl.BlockSpec((ls, qw), rowmap(1)),
                  pl.BlockSpec((ls, vw), rowmap(1)),
                  pl.BlockSpec((ls, vw), rowmap(2)),
                  pl.BlockSpec((ls, qw), rowmap(6)),
                  pl.BlockSpec((ls, qw), rowmap(7)),
                  pl.BlockSpec((ls, vw), rowmap(4)),
                  pl.BlockSpec((ls, vw), rowmap(5)),
                  pl.BlockSpec((ls, qw), rowmap(0)),
                  pl.BlockSpec((1, vw), const),
                  pl.BlockSpec((1, vw), const),
                  pl.BlockSpec((RET_HEADS, L, L), const3),
                  pl.BlockSpec((RET_HEADS, L, RET_DK), const3),
                  pl.BlockSpec((RET_HEADS, L, RET_DK), const3)],
        out_specs=pl.BlockSpec((ls, 2 * vw), lambda b, i: (b * ns + i, 0)),
        scratch_shapes=[pltpu.VMEM((RET_HEADS, RET_DK, RET_DV), F32),
                        pltpu.VMEM((GLA_HEADS, GLA_DV, GLA_DK), F32)],
        compiler_params=_cparams(("arbitrary", "arbitrary")),
        name="ret_gla_mixer",
    )(y, y, y, y, y, y, y, y, glog, ret_norm, gla_norm, decay, zeta, xi)


def _odd_proj_kernel(x_ref, mod_ref, gain_ref, w_ref, wdt_ref, dtb_ref, y_ref, dt_ref, *, ncol, cch):
    h = _norm_mod(x_ref[...], gain_ref[...], mod_ref[0], 0, 1)
    hb = h.astype(BF16)
    width = w_ref.shape[1] // ncol
    for j in range(ncol):
        c0 = j * width
        acc = _dot(hb, w_ref[:, c0:c0 + width])
        if c0 >= cch:
            acc = _silu(acc)
        y_ref[:, c0:c0 + width] = acc.astype(y_ref.dtype)
    z = _dot(hb, wdt_ref[...]) + dtb_ref[...]
    dt_ref[...] = jnp.maximum(z, 0.0) + jnp.log1p(jnp.exp(-jnp.abs(z)))


def _odd_proj(x2d, mod_l, gain, w_main, w_dt, dt_bias, cch, tm, seq):
    t, d = x2d.shape
    n = w_main.shape[1]
    tiles_per_b = seq // tm
    const = lambda i: (0, 0)
    return pl.pallas_call(
        functools.partial(_odd_proj_kernel, ncol=n // 512, cch=cch),
        out_shape=(jax.ShapeDtypeStruct((t, n), BF16), jax.ShapeDtypeStruct((t, LANES), F32)),
        grid=(t // tm,),
        in_specs=[pl.BlockSpec((tm, d), lambda i: (i, 0)),
                  pl.BlockSpec((1, 6, d), lambda i: (i // tiles_per_b, 0, 0)),
                  pl.BlockSpec((1, d), const),
                  pl.BlockSpec((d, n), const),
                  pl.BlockSpec((d, LANES), const),
                  pl.BlockSpec((1, LANES), const)],
        out_specs=(pl.BlockSpec((tm, n), lambda i: (i, 0)), pl.BlockSpec((tm, LANES), lambda i: (i, 0))),
        compiler_params=_cparams(("arbitrary",)),
        name="odd_proj",
    )(x2d, mod_l, gain, w_main, w_dt, dt_bias)


def _ssd_kernel(z_ref, xbc_ref, dt_ref, cw_ref, cb_ref, alog_ref, dsk_ref, ng_ref, o_ref, hist_ref, st_ref):
    @pl.when(pl.program_id(1) == 0)
    def _():
        hist_ref[...] = jnp.zeros_like(hist_ref)
        st_ref[...] = jnp.zeros_like(st_ref)

    L = xbc_ref.shape[0]
    P, N, G = SSD_HEADDIM, SSD_STATE, SSD_GROUPS
    dinner = z_ref.shape[1]
    gw = dinner // G
    hpg = gw // P
    slab = gw + 2 * N
    hrows = hist_ref.shape[0]

    dt = dt_ref[...]
    da = dt * (-jnp.exp(alog_ref[...]))
    trow = _iota((L, L), 0)
    tcol = _iota((L, L), 1)
    causal = trow >= tcol
    cum = _dot_exact_rhs(causal.astype(BF16), da)
    cum_t = jnp.transpose(cum)
    dt_t = jnp.transpose(dt)
    ecum = jnp.exp(cum)
    wend = dt * jnp.exp(cum[L - 1:L, :] - cum)
    lane = _iota((L, LANES), 1)
    lo_half = lane < P

    def pair_cols(a, ha):
        return jnp.where(lo_half, jnp.broadcast_to(a[:, ha:ha + 1], (L, LANES)),
                         jnp.broadcast_to(a[:, ha + 1:ha + 2], (L, LANES)))

    srow = _iota((L, hrows + L), 0)
    scol = _iota((L, hrows + L), 1)
    shift = jnp.concatenate([(scol == srow + (hrows - s)).astype(BF16) for s in range(1, SSD_CONV)], axis=0)

    for g in range(G):
        c0 = g * slab
        xin = xbc_ref[:, c0:c0 + slab]
        xcat = jnp.concatenate([hist_ref[:, c0:c0 + slab], xin], axis=0)
        hist_ref[:, c0:c0 + slab] = xin[L - hrows:L]
        sh = _dot(shift, xcat)
        conv = xin.astype(F32) * cw_ref[SSD_CONV - 1:SSD_CONV, c0:c0 + slab] + cb_ref[:, c0:c0 + slab]
        for s in range(1, SSD_CONV):
            conv = conv + sh[(s - 1) * L:s * L] * cw_ref[SSD_CONV - 1 - s:SSD_CONV - s, c0:c0 + slab]
        act = _silu(conv)
        xs = act[:, :gw]
        bm_g = act[:, gw:gw + N].astype(BF16)
        cm_g = act[:, gw + N:].astype(BF16)

        cb = _dot_nt(cm_g, bm_g)
        xs_b = xs.astype(BF16)
        pieces, xw, ec = [], [], []
        for pr in range(hpg // 2):
            ha = g * hpg + 2 * pr
            ms = []
            for hh in (ha, ha + 1):
                seg = jnp.broadcast_to(cum[:, hh:hh + 1], (L, L)) - cum_t[hh:hh + 1, :]
                ms.append((cb * jnp.where(causal, jnp.exp(seg), 0.0) * dt_t[hh:hh + 1, :]).astype(BF16))
            lhs = jnp.concatenate(ms, axis=1)
            blk = xs_b[:, pr * 2 * P:(pr + 1) * 2 * P]
            zero = jnp.zeros_like(blk)
            rhs = jnp.concatenate([jnp.where(lo_half, blk, zero), jnp.where(lo_half, zero, blk)], axis=0)
            pieces.append(_dot(lhs, rhs))
            xw.append((xs[:, pr * 2 * P:(pr + 1) * 2 * P] * pair_cols(wend, ha)).astype(BF16))
            ec.append(pair_cols(ecum, ha))
        inner = jnp.concatenate(pieces, axis=1)
        ecum_g = jnp.concatenate(ec, axis=1)
        st = st_ref[g]
        cross = _dot(cm_g, st.astype(BF16)) * ecum_g
        st_ref[g] = st * ecum_g[L - 1:L, :] + _dot_tn(bm_g, jnp.concatenate(xw, axis=1))
        y = inner + cross + dsk_ref[:, g * gw:(g + 1) * gw] * xs
        y = y * z_ref[:, g * gw:(g + 1) * gw].astype(F32)
        o_ref[:, g * gw:(g + 1) * gw] = (_rms(y) * ng_ref[:, g * gw:(g + 1) * gw]).astype(o_ref.dtype)


SSD_HIST_ROWS = 16


def _ssd(y, dt, conv_w, conv_b, a_log_row, d_row, norm_g, bsz, seq):
    t = y.shape[0]
    L = SSD_CHUNK
    ns = seq // L
    dinner = norm_g.shape[1]
    cch = conv_w.shape[1]
    gw = dinner // SSD_GROUPS
    const = lambda b, i: (0, 0)
    assert cch % dinner == 0
    return pl.pallas_call(
        _ssd_kernel,
        out_shape=jax.ShapeDtypeStruct((t, dinner), BF16),
        grid=(bsz, ns),
        in_specs=[pl.BlockSpec((L, dinner), lambda b, i: (b * ns + i, cch // dinner)),
                  pl.BlockSpec((L, cch), lambda b, i: (b * ns + i, 0)),
                  pl.BlockSpec((L, LANES), lambda b, i: (b * ns + i, 0)),
                  pl.BlockSpec((SSD_CONV, cch), const),
                  pl.BlockSpec((1, cch), const),
                  pl.BlockSpec((1, LANES), const),
                  pl.BlockSpec((1, dinner), const),
                  pl.BlockSpec((1, dinner), const)],
        out_specs=pl.BlockSpec((L, dinner), lambda b, i: (b * ns + i, 0)),
        scratch_shapes=[pltpu.VMEM((SSD_HIST_ROWS, cch), BF16),
                        pltpu.VMEM((SSD_GROUPS, SSD_STATE, gw), F32)],
        compiler_params=_cparams(("arbitrary", "arbitrary")),
        name="ssd_mixer",
    )(y, y, dt, conv_w, conv_b, a_log_row, d_row, norm_g)


def _outproj_kernel(o_ref, x_ref, mod_ref, gain_ref, w_ref, x1_ref, h2_ref):
    mod = mod_ref[0]
    x1 = x_ref[...] + mod[2:3, :] * _dot(o_ref[...], w_ref[...])
    x1_ref[...] = x1
    h2_ref[...] = _norm_mod(x1, gain_ref[...], mod, 3, 4)


def _outproj(o, x2d, mod_l, gain2, w_out, tm, seq):
    t, d = x2d.shape
    kin = o.shape[1]
    tiles_per_b = seq // tm
    const = lambda i: (0, 0)
    return pl.pallas_call(
        _outproj_kernel,
        out_shape=(jax.ShapeDtypeStruct((t, d), F32), jax.ShapeDtypeStruct((t, d), F32)),
        grid=(t // tm,),
        in_specs=[pl.BlockSpec((tm, kin), lambda i: (i, 0)),
                  pl.BlockSpec((tm, d), lambda i: (i, 0)),
                  pl.BlockSpec((1, 6, d), lambda i: (i // tiles_per_b, 0, 0)),
                  pl.BlockSpec((1, d), const),
                  pl.BlockSpec((kin, d), const)],
        out_specs=(pl.BlockSpec((tm, d), lambda i: (i, 0)), pl.BlockSpec((tm, d), lambda i: (i, 0))),
        compiler_params=_cparams(("arbitrary",)),
        name="out_proj",
    )(o, x2d, mod_l, gain2, w_out)


def _route_kernel(h_ref, rw_ref, rb_ref, dest_ref, gate_ref, blk_ref, cnt_ref, pst_ref, *, nbpad):
    phase = pl.program_id(0)
    step = pl.program_id(1)
    tm = h_ref.shape[0]
    ne = rw_ref.shape[0]
    epg = ne // N_GROUPS

    @pl.when(jnp.logical_and(phase == 0, step == 0))
    def _():
        cnt_ref[...] = jnp.zeros_like(cnt_ref)
        pst_ref[...] = jnp.zeros_like(pst_ref)

    @pl.when(jnp.logical_and(phase == 1, step == 0))
    def _():
        nblk = jnp.floor((cnt_ref[...] + (MOE_BLOCK - 1.0)) * (1.0 / MOE_BLOCK))
        hi = jnp.floor(nblk * (1.0 / 16.0))
        lo = nblk - hi * 16.0
        er = _iota((ne, ne), 0)
        ec = _iota((ne, ne), 1)
        lower = (ec < er).astype(BF16)
        pst = _dot(lower, hi.astype(BF16)) * 16.0 + _dot(lower, lo.astype(BF16))
        pst_ref[...] = pst
        pend = jnp.concatenate([pst + nblk] * (nbpad // LANES), axis=1)
        bidx = _iota((ne, nbpad), 1).astype(F32)
        be = jnp.sum(jnp.where(pend <= bidx, 1.0, 0.0), axis=0, keepdims=True)
        total = jnp.max(pend, axis=0, keepdims=True)
        blk_ref[0:1, :] = jnp.minimum(be, ne - 1.0).astype(I32)
        blk_ref[1:2, :] = total.astype(I32)
        cnt_ref[...] = jnp.zeros_like(cnt_ref)

    logits = _dot_nt(rw_ref[...], h_ref[...].astype(BF16))
    s = jax.nn.sigmoid(logits)
    sb = s + jnp.concatenate([rb_ref[...]] * (tm // LANES), axis=1)
    G = N_GROUPS
    v = [sb[j * G:(j + 1) * G, :] for j in range(epg)]
    m01, n01 = jnp.maximum(v[0], v[1]), jnp.minimum(v[0], v[1])
    m23, n23 = jnp.maximum(v[2], v[3]), jnp.minimum(v[2], v[3])
    top1 = jnp.maximum(m01, m23)
    top2 = jnp.maximum(jnp.minimum(m01, m23), jnp.maximum(n01, n23))
    gscore = top1 + top2
    gi = _iota((G, tm), 0)
    gmax = jnp.max(gscore, axis=0, keepdims=True)
    gidx = jnp.min(jnp.where(gscore == gmax, gi, G), axis=0, keepdims=True)
    sel = gi == gidx
    vb = [jnp.sum(jnp.where(sel, v[j], 0.0), axis=0, keepdims=True) for j in range(epg)]
    vs = [jnp.sum(jnp.where(sel, s[j * G:(j + 1) * G, :], 0.0), axis=0, keepdims=True) for j in range(epg)]
    zero = jnp.zeros_like(vb[0])
    e1 = zero
    e2 = zero
    w1 = zero
    w2 = zero
    for j in range(epg):
        rank = zero
        for i in range(epg):
            if i == j:
                continue
            ahead = (vb[i] >= vb[j]) if i < j else (vb[i] > vb[j])
            rank = rank + jnp.where(ahead, 1.0, 0.0)
        first = rank == 0.0
        second = rank == 1.0
        e1 = e1 + jnp.where(first, float(j), 0.0)
        e2 = e2 + jnp.where(second, float(j), 0.0)
        w1 = w1 + jnp.where(first, vs[j], 0.0)
        w2 = w2 + jnp.where(second, vs[j], 0.0)
    gf = gidx.astype(F32) * float(epg)
    e1 = e1 + gf
    e2 = e2 + gf
    wsum = w1 + w2
    ei = _iota((ne, tm), 0).astype(F32)
    oh1 = ei == e1
    oh2 = ei == e2
    onehot = jnp.where(oh1 | oh2, 1.0, 0.0)
    upper = (_iota((tm, tm), 0) < _iota((tm, tm), 1)).astype(BF16)
    prefix = _dot(onehot.astype(BF16), upper)
    cnt = cnt_ref[...]
    base = jnp.concatenate([cnt] * (tm // LANES), axis=1) + prefix
    pst = jnp.concatenate([pst_ref[...]] * (tm // LANES), axis=1) * float(MOE_BLOCK)
    slot = base + pst
    d1 = jnp.sum(jnp.where(oh1, slot, 0.0), axis=0, keepdims=True)
    d2 = jnp.sum(jnp.where(oh2, slot, 0.0), axis=0, keepdims=True)
    cnt_ref[...] = cnt + jnp.sum(onehot, axis=1, keepdims=True)
    dest_ref[0, 0:1, :] = d1.astype(I32)
    dest_ref[0, 1:2, :] = d2.astype(I32)
    gate_ref[0, 0:1, :] = w1 / wsum
    gate_ref[0, 1:2, :] = w2 / wsum


def _route(h2, rw_t, rb_b, tm, nbpad):
    t, d = h2.shape
    ne = rw_t.shape[0]
    nt = t // tm
    return pl.pallas_call(
        functools.partial(_route_kernel, nbpad=nbpad),
        out_shape=(jax.ShapeDtypeStruct((nt, 2, tm), I32), jax.ShapeDtypeStruct((nt, 2, tm), F32),
                   jax.ShapeDtypeStruct((2, nbpad), I32)),
        grid=(2, nt),
        in_specs=[pl.BlockSpec((tm, d), lambda p, i: (i, 0)),
                  pl.BlockSpec((ne, d), lambda p, i: (0, 0)),
                  pl.BlockSpec((ne, LANES), lambda p, i: (0, 0))],
        out_specs=(pl.BlockSpec((1, 2, tm), lambda p, i: (i * p, 0, 0)),
                   pl.BlockSpec((1, 2, tm), lambda p, i: (i * p, 0, 0)),
                   pl.BlockSpec((2, nbpad), lambda p, i: (0, 0))),
        scratch_shapes=[pltpu.VMEM((ne, LANES), F32), pltpu.VMEM((ne, LANES), F32)],
        compiler_params=_cparams(("arbitrary", "arbitrary")),
        name="router",
    )(h2, rw_t, rb_b)


SC_CORES, SC_SUBCORES = 2, 16
SC_ROWS = 32


def _sc_mesh():
    return plsc.VectorSubcoreMesh(core_axis_name="c", subcore_axis_name="s",
                                  num_cores=SC_CORES, num_subcores=SC_SUBCORES)


def _sc_worker_base(rows_per_worker):
    return (lax.axis_index("s") * SC_CORES + lax.axis_index("c")) * rows_per_worker


def _sc_scatter_rows(src, idx, n_out):
    t, d = src.shape
    copies = idx.shape[0] // t
    per_w = t // (SC_CORES * SC_SUBCORES)
    assert per_w % SC_ROWS == 0

    @functools.partial(
        pl.kernel, mesh=_sc_mesh(), out_type=jax.ShapeDtypeStruct((n_out, d), src.dtype),
        scratch_types=[pltpu.VMEM((SC_ROWS,), I32), pltpu.VMEM((SC_ROWS, d), src.dtype), pltpu.SemaphoreType.DMA],
        compiler_params=pltpu.CompilerParams(use_tc_tiling_on_sc=True), name="moe_dispatch_sc")
    def k(src_hbm, idx_hbm, out_hbm, idx_v, rows_v, sem):
        base = _sc_worker_base(per_w)

        @pl.loop(0, per_w // SC_ROWS)
        def _(j):
            off = pl.multiple_of(base + j * SC_ROWS, SC_ROWS)
            pltpu.sync_copy(src_hbm.at[pl.ds(off, SC_ROWS)], rows_v)
            for c in range(copies):
                pltpu.sync_copy(idx_hbm.at[pl.ds(c * t + off, SC_ROWS)], idx_v)
                pltpu.async_copy(rows_v, out_hbm.at[idx_v], sem).wait()

    return k(src, idx)


def _sc_gather_rows(table, idx):
    _, d = table.shape
    b = idx.shape[0]
    per_w = b // (SC_CORES * SC_SUBCORES)
    assert per_w % SC_ROWS == 0

    @functools.partial(
        pl.kernel, mesh=_sc_mesh(), out_type=jax.ShapeDtypeStruct((b, d), table.dtype),
        scratch_types=[pltpu.VMEM((SC_ROWS,), I32), pltpu.VMEM((SC_ROWS, d), table.dtype), pltpu.SemaphoreType.DMA],
        compiler_params=pltpu.CompilerParams(use_tc_tiling_on_sc=True), name="moe_gather_sc")
    def k(table_hbm, idx_hbm, out_hbm, idx_v, rows_v, sem):
        base = _sc_worker_base(per_w)

        @pl.loop(0, per_w // SC_ROWS)
        def _(j):
            off = pl.multiple_of(base + j * SC_ROWS, SC_ROWS)
            pltpu.sync_copy(idx_hbm.at[pl.ds(off, SC_ROWS)], idx_v)
            pltpu.async_copy(table_hbm.at[idx_v], rows_v, sem).wait()
            pltpu.sync_copy(rows_v, out_hbm.at[pl.ds(off, SC_ROWS)])

    return k(table, idx)


def _expert_kernel(blk_ref, nused_ref, xs_ref, wg_ref, wu_ref, wd_ref, y_ref, wgb, wub, wdb):
    b = pl.program_id(0)
    prev = blk_ref[jnp.maximum(b - 1, 0)]
    changed = jnp.logical_or(b == 0, blk_ref[b] != prev)

    @pl.when(changed)
    def _():
        wgb[...] = wg_ref[0, 0].astype(BF16)
        wub[...] = wu_ref[0, 0].astype(BF16)
        wdb[...] = wd_ref[0, 0].astype(BF16)

    @pl.when(b < nused_ref[0])
    def _():
        xb = xs_ref[...].astype(BF16)
        g = _dot(xb, wgb[...])
        u = _dot(xb, wub[...])
        a = (_silu(g) * u).astype(BF16)
        y_ref[...] = _dot(a, wdb[...]).astype(y_ref.dtype)

    @pl.when(b >= nused_ref[0])
    def _():
        y_ref[...] = jnp.zeros_like(y_ref)


def _experts(xs, blk_e, nused, w_gate, w_up, w_down, layer):
    np_rows, d = xs.shape
    _, ne, _, ff = w_gate.shape
    nb = np_rows // MOE_BLOCK
    grid_spec = pltpu.PrefetchScalarGridSpec(
        num_scalar_prefetch=2,
        grid=(nb,),
        in_specs=[pl.BlockSpec((MOE_BLOCK, d), lambda b, be, nu: (b, 0)),
                  pl.BlockSpec((1, 1, d, ff), lambda b, be, nu: (layer, be[b], 0, 0)),
                  pl.BlockSpec((1, 1, d, ff), lambda b, be, nu: (layer, be[b], 0, 0)),
                  pl.BlockSpec((1, 1, ff, d), lambda b, be, nu: (layer, be[b], 0, 0))],
        out_specs=pl.BlockSpec((MOE_BLOCK, d), lambda b, be, nu: (b, 0)),
        scratch_shapes=[pltpu.VMEM((d, ff), BF16), pltpu.VMEM((d, ff), BF16), pltpu.VMEM((ff, d), BF16)],
    )
    return pl.pallas_call(
        _expert_kernel,
        out_shape=jax.ShapeDtypeStruct((np_rows, d), F32),
        grid_spec=grid_spec,
        compiler_params=_cparams(("arbitrary",)),
        name="moe_experts",
    )(blk_e, nused, xs, w_gate, w_up, w_down)


def _combine_kernel(x1_ref, y1_ref, y2_ref, gate_ref, mod_ref, fg_ref, o_ref, *, final):
    tc, d = x1_ref.shape
    gates = gate_ref[0]
    acc = jnp.zeros((tc, d), F32)
    for k, y_ref in enumerate((y1_ref, y2_ref)):
        gcol = jnp.transpose(jnp.broadcast_to(gates[k:k + 1, :], (LANES, tc)))
        acc = acc + y_ref[...] * jnp.concatenate([gcol] * (d // LANES), axis=1)
    x2 = x1_ref[...] + mod_ref[0][5:6, :] * acc
    if final:
        x2 = _rms(x2) * fg_ref[...]
    o_ref[...] = x2


def _combine(x1, y12, gates, mod_l, final_gain, tc, seq, final):
    t, d = x1.shape
    nt = t // tc
    tiles_per_b = seq // tc
    return pl.pallas_call(
        functools.partial(_combine_kernel, final=final),
        out_shape=jax.ShapeDtypeStruct((t, d), F32),
        grid=(nt,),
        in_specs=[pl.BlockSpec((tc, d), lambda i: (i, 0)),
                  pl.BlockSpec((tc, d), lambda i: (i, 0)),
                  pl.BlockSpec((tc, d), lambda i: (nt + i, 0)),
                  pl.BlockSpec((1, TOP_K, tc), lambda i: (i, 0, 0)),
                  pl.BlockSpec((1, 6, d), lambda i: (i // tiles_per_b, 0, 0)),
                  pl.BlockSpec((1, d), lambda i: (0, 0))],
        out_specs=pl.BlockSpec((tc, d), lambda i: (i, 0)),
        compiler_params=_cparams(("arbitrary",)),
        name="moe_combine",
    )(x1, y12, y12, gates, mod_l, final_gain)


ROUTE_TILE = 512


def _moe(x1, h2, mod_l, rw_t, rb_b, w_gate, w_up, w_down, layer, final_gain, seq, final):
    t, d = x1.shape
    ne = w_gate.shape[1]
    nb = (t * TOP_K + ne * (MOE_BLOCK - 1) + MOE_BLOCK - 1) // MOE_BLOCK
    nbpad = -(-nb // LANES) * LANES
    tm = ROUTE_TILE
    dest, gates, blk = _route(h2, rw_t, rb_b, tm, nbpad)
    slot = jnp.transpose(dest, (1, 0, 2)).reshape(TOP_K * t)
    xs = _sc_scatter_rows(h2, slot, nb * MOE_BLOCK)
    ys = _experts(xs, blk[0, :nb], blk[1, :1], w_gate, w_up, w_down, layer)
    y12 = _sc_gather_rows(ys, slot)
    return _combine(x1, y12, gates, mod_l, final_gain, tm, seq, final)


def kernel(x, c, positions, mod_w, mod_b, norm_mix, norm_ffn, ev_w_in, ev_ret_norm, ev_gla_wa2, ev_gla_ba,
           ev_gla_norm, ev_w_out, od_w_in, od_conv_w, od_conv_b, od_dt_bias, od_a_log, od_d, od_norm, od_w_out,
           router_w, router_b, exp_w_gate, exp_w_up, exp_w_down, final_norm):
    bsz, seq, d = x.shape
    depth = mod_w.shape[0]
    t = bsz * seq
    ne = router_w.shape[1]
    epg = ne // N_GROUPS
    mod = _modulation(c, mod_w, mod_b)
    x2d = x.reshape(t, d)

    perm = np.array([g * epg + j for j in range(epg) for g in range(N_GROUPS)])
    rw_t = router_w.T[perm].astype(BF16)
    rb_b = jnp.broadcast_to(router_b.astype(F32)[perm][:, None], (ne, LANES))
    final_gain = final_norm.reshape(1, d)

    half = RET_DK // 2
    inv = ROPE_BASE ** (-jnp.arange(half, dtype=F32) / half)
    inv2 = jnp.concatenate([inv, inv]).reshape(1, LANES)
    pos_col = positions.reshape(t, 1)

    for layer in range(depth):
        i = layer // 2
        mod_l = mod[layer]
        gain1 = norm_mix[layer].reshape(1, d)
        gain2 = norm_ffn[layer].reshape(1, d)
        if layer % 2 == 0:
            w_in = ev_w_in[i]
            nmain = w_in.shape[1] - GLA_RANK
            w_main = w_in[:, :nmain].astype(BF16)
            w_ga = jnp.pad(w_in[:, nmain:], ((0, 0), (0, LANES - GLA_RANK))).astype(BF16)
            wa2 = jnp.pad(ev_gla_wa2[i], ((0, LANES - GLA_RANK), (0, 0))).astype(BF16)
            y, glog = _even_proj(x2d, mod_l, gain1, pos_col, inv2, w_main, w_ga, wa2,
                                 ev_gla_ba[i].reshape(1, -1), 512, seq)
            o = _ret_gla(y, glog, ev_ret_norm[i].reshape(1, -1), ev_gla_norm[i].reshape(1, -1), bsz, seq, 512)
            w_out = ev_w_out[i].astype(BF16)
        else:
            w_in = od_w_in[i]
            nheads = od_a_log.shape[1]
            nmain = w_in.shape[1] - nheads
            dinner = od_norm.shape[1]
            gw, ns_ = dinner // SSD_GROUPS, SSD_STATE
            bc0 = dinner + SSD_GROUPS * ns_
            cperm = np.concatenate([np.concatenate([np.arange(g * gw, (g + 1) * gw),
                                                    dinner + np.arange(g * ns_, (g + 1) * ns_),
                                                    bc0 + np.arange(g * ns_, (g + 1) * ns_)])
                                    for g in range(SSD_GROUPS)])
            conv_w = od_conv_w[i][:, cperm]
            conv_b = od_conv_b[i][cperm].reshape(1, -1)
            w_main = jnp.concatenate([w_in[:, dinner + cperm], w_in[:, :dinner]], axis=1).astype(BF16)
            w_dt = jnp.pad(w_in[:, nmain:], ((0, 0), (0, LANES - nheads))).astype(BF16)
            dtb = jnp.pad(od_dt_bias[i], (0, LANES - nheads)).reshape(1, LANES)
            y, dt = _odd_proj(x2d, mod_l, gain1, w_main, w_dt, dtb, od_conv_w.shape[2], 512, seq)
            alog = jnp.pad(od_a_log[i], (0, LANES - nheads)).reshape(1, LANES)
            d_row = jnp.repeat(od_d[i], SSD_HEADDIM).reshape(1, dinner)
            o = _ssd(y, dt, conv_w, conv_b, alog, d_row, od_norm[i].reshape(1, dinner), bsz, seq)
            w_out = od_w_out[i].astype(BF16)
        x1, h2 = _outproj(o, x2d, mod_l, gain2, w_out, 512, seq)
        x2d = _moe(x1, h2, mod_l, rw_t, rb_b, exp_w_gate, exp_w_up, exp_w_down, layer,
                   final_gain, seq, final=(layer == depth - 1))
    return x2d.reshape(bsz, seq, d)
```

```python
import functools

import jax
import jax.numpy as jnp
import numpy as np
from jax import lax
from jax.experimental import pallas as pl
from jax.experimental.pallas import tpu as pltpu
from jax.experimental.pallas import tpu_sc as plsc

F32 = jnp.float32
BF16 = jnp.bfloat16
I32 = jnp.int32

RET_HEADS, RET_DK, RET_DV, RET_CHUNK = 4, 128, 256, 128
ROPE_BASE = 10000.0
GLA_HEADS, GLA_DK, GLA_DV, GLA_RANK, GLA_TAU, GLA_CHUNK = 4, 128, 256, 16, 16.0, 64
SSD_HEADDIM, SSD_GROUPS, SSD_STATE, SSD_CONV, SSD_CHUNK = 64, 8, 128, 4, 128
N_GROUPS, TOP_K, GROUP_SCORE_K, MOE_BLOCK = 8, 2, 2, 256
NORM_EPS = 1e-6

LANES = 128
VMEM_LIMIT = 56 * 1024 * 1024


def _cparams(sem, vmem=VMEM_LIMIT):
    return pltpu.CompilerParams(dimension_semantics=sem, vmem_limit_bytes=vmem)


def _dot(a, b):
    return jnp.dot(a, b, preferred_element_type=F32)


def _dot_nt(a, b):
    return lax.dot_general(a, b, (((1,), (1,)), ((), ())), preferred_element_type=F32)


def _dot_tn(a, b):
    return lax.dot_general(a, b, (((0,), (0,)), ((), ())), preferred_element_type=F32)


def _split3(x):
    a = x.astype(BF16)
    r = x - a.astype(F32)
    b = r.astype(BF16)
    c = (r - b.astype(F32)).astype(BF16)
    return a, b, c


def _dot_exact_rhs(m01, x):
    a, b, c = _split3(x)
    return _dot(m01, a) + _dot(m01, b) + _dot(m01, c)


def _dot_exact_lhs(x, m01):
    a, b, c = _split3(x)
    return _dot(a, m01) + _dot(b, m01) + _dot(c, m01)


def _dot_tn_exact(x, m01):
    a, b, c = _split3(x)
    return _dot_tn(a, m01) + _dot_tn(b, m01) + _dot_tn(c, m01)


def _silu(x):
    return x * jax.nn.sigmoid(x)


def _rms(x, eps=NORM_EPS):
    return x * lax.rsqrt(jnp.mean(x * x, axis=-1, keepdims=True) + eps)


def _iota(shape, dim):
    return lax.broadcasted_iota(I32, shape, dim)


PACKED = jnp.uint32
_HI16 = 0xFFFF0000


def _packed_width(n):
    return n // 2


def _pack_pair(x):
    n = x.shape[1] // 2
    lo = lax.bitcast_convert_type(x[:, :n].astype(BF16).astype(F32), jnp.uint32)
    hi = lax.bitcast_convert_type(x[:, n:].astype(BF16).astype(F32), jnp.uint32)
    return (hi & jnp.uint32(_HI16)) | (lo >> 16)


def _unpack_pair(w):
    lo = lax.bitcast_convert_type(w << 16, F32)
    hi = lax.bitcast_convert_type(w & jnp.uint32(_HI16), F32)
    return jnp.concatenate([lo, hi], axis=1)


def _mod_kernel(c_ref, w_ref, b_ref, o_ref):
    sc = _silu(c_ref[...])
    o_ref[0] = _dot(sc.astype(BF16), w_ref[0].astype(BF16)) + b_ref[0]


def _modulation(c, mod_w, mod_b):
    depth, d, d6 = mod_w.shape
    bsz = c.shape[0]
    nb = d6 // d
    out = pl.pallas_call(
        _mod_kernel,
        out_shape=jax.ShapeDtypeStruct((depth, bsz, d6), F32),
        grid=(depth, nb),
        in_specs=[pl.BlockSpec((bsz, d), lambda l, j: (0, 0)),
                  pl.BlockSpec((1, d, d), lambda l, j: (l, 0, j)),
                  pl.BlockSpec((1, 1, d), lambda l, j: (l, 0, j))],
        out_specs=pl.BlockSpec((1, bsz, d), lambda l, j: (l, 0, j)),
        compiler_params=_cparams(("arbitrary", "arbitrary")),
        name="adaln_mod",
    )(c, mod_w, mod_b.reshape(depth, 1, d6))
    return out.reshape(depth, bsz, nb, d)


def _norm_mod(x, gain, mod, shift_row, scale_row):
    y = _rms(x) * gain
    return y * (1.0 + mod[scale_row:scale_row + 1, :]) + mod[shift_row:shift_row + 1, :]


def _even_proj_kernel(x_ref, mod_ref, gain_ref, pos_ref, inv_ref, w_ref, wga_ref, wa2_ref, ba_ref,
                      y_ref, glog_ref, *, ncol):
    h = _norm_mod(x_ref[...], gain_ref[...], mod_ref[0], 0, 1)
    hb = h.astype(BF16)
    tm = hb.shape[0]
    ang = pos_ref[...].astype(F32) * inv_ref[...]
    cos = jnp.cos(ang)
    lane = _iota((tm, LANES), 1)
    sin = jnp.where(lane < LANES // 2, -jnp.sin(ang), jnp.sin(ang))
    qk_scale = RET_DK ** -0.5
    rqk = RET_HEADS * RET_DK
    rv = RET_HEADS * RET_DV
    gq0 = 2 * rqk + 2 * rv
    width = w_ref.shape[1] // ncol
    for j in range(ncol):
        c0 = j * width
        acc = _dot(hb, w_ref[:, c0:c0 + width])
        for s in range(width // LANES):
            col = c0 + s * LANES
            blk = acc[:, s * LANES:(s + 1) * LANES]
            if col < 2 * rqk:
                blk = blk * cos + pltpu.roll(blk, LANES // 2, 1) * sin
                if col >= rqk:
                    blk = blk * qk_scale
            elif gq0 <= col < gq0 + GLA_HEADS * GLA_DK:
                blk = blk * (GLA_DK ** -0.5)
            y_ref[:, col:col + LANES] = blk.astype(y_ref.dtype)
    ga = _dot(hb, wga_ref[...])
    z = _dot(ga.astype(BF16), wa2_ref[...]) + ba_ref[...]
    logsig = jnp.minimum(z, 0.0) - jnp.log1p(jnp.exp(-jnp.abs(z)))
    glog_ref[...] = logsig * (1.0 / GLA_TAU)


def _even_proj(x2d, mod_l, gain, pos_col, inv2, w_main, w_ga, wa2, ba, tm, seq):
    t, d = x2d.shape
    n = w_main.shape[1]
    tiles_per_b = seq // tm
    gk = wa2.shape[1]
    const = lambda i: (0, 0)
    return pl.pallas_call(
        functools.partial(_even_proj_kernel, ncol=n // 512),
        out_shape=(jax.ShapeDtypeStruct((t, n), BF16), jax.ShapeDtypeStruct((t, gk), F32)),
        grid=(t // tm,),
        in_specs=[pl.BlockSpec((tm, d), lambda i: (i, 0)),
                  pl.BlockSpec((1, 6, d), lambda i: (i // tiles_per_b, 0, 0)),
                  pl.BlockSpec((1, d), const),
                  pl.BlockSpec((tm, 1), lambda i: (i, 0)),
                  pl.BlockSpec((1, LANES), const),
                  pl.BlockSpec((d, n), const),
                  pl.BlockSpec((d, LANES), const),
                  pl.BlockSpec((LANES, gk), const),
                  pl.BlockSpec((1, gk), const)],
        out_specs=(pl.BlockSpec((tm, n), lambda i: (i, 0)), pl.BlockSpec((tm, gk), lambda i: (i, 0))),
        compiler_params=_cparams(("arbitrary",)),
        name="even_proj",
    )(x2d, mod_l, gain, pos_col, inv2, w_main, w_ga, wa2, ba)


RET_GLA_UNROLL = 2


def _ret_gla_kernel(rq_ref, rk_ref, rv_ref, rg_ref, gq_ref, gk_ref, gv_ref, gg_ref, gl_ref,
                    rn_ref, gn_ref, dec_ref, zeta_ref, xi_ref, o_ref, rst_ref, gst_ref, *, chunk_decay):
    @pl.when(pl.program_id(1) == 0)
    def _():
        rst_ref[...] = jnp.zeros_like(rst_ref)
        gst_ref[...] = jnp.zeros_like(gst_ref)

    ls = rq_ref.shape[0]
    lr, lg = RET_CHUNK, GLA_CHUNK
    trow = _iota((lg, lg), 0)
    tcol = _iota((lg, lg), 1)
    causal = trow >= tcol
    tri = causal.astype(BF16)
    base = RET_HEADS * RET_DV

    def ret_chunk(r0, h):
        q = rq_ref[pl.ds(r0, lr), h * RET_DK:(h + 1) * RET_DK]
        k = rk_ref[pl.ds(r0, lr), h * RET_DK:(h + 1) * RET_DK]
        v = rv_ref[pl.ds(r0, lr), h * RET_DV:(h + 1) * RET_DV]
        g = rg_ref[pl.ds(r0, lr), h * RET_DV:(h + 1) * RET_DV].astype(F32)
        st = rst_ref[h]
        s = _dot_nt(q, k) * dec_ref[h]
        inner = _dot(s.astype(BF16), v)
        cross = _dot((q.astype(F32) * xi_ref[h]).astype(BF16), st.astype(BF16))
        kv = _dot_tn((k.astype(F32) * zeta_ref[h]).astype(BF16), v)
        rst_ref[h] = st * chunk_decay[h] + kv
        o = _rms(inner + cross) * rn_ref[:, h * RET_DV:(h + 1) * RET_DV] * _silu(g)
        o_ref[pl.ds(r0, lr), h * RET_DV:(h + 1) * RET_DV] = o.astype(o_ref.dtype)

    def gla_chunk(r0, h):
        q = gq_ref[pl.ds(r0, lg), h * GLA_DK:(h + 1) * GLA_DK].astype(F32)
        k = gk_ref[pl.ds(r0, lg), h * GLA_DK:(h + 1) * GLA_DK].astype(F32)
        v = gv_ref[pl.ds(r0, lg), h * GLA_DV:(h + 1) * GLA_DV]
        g = gg_ref[pl.ds(r0, lg), h * GLA_DV:(h + 1) * GLA_DV].astype(F32)
        gl = gl_ref[pl.ds(r0, lg), h * GLA_DK:(h + 1) * GLA_DK]
        b = _dot_exact_rhs(tri, gl)
        bref = b[lg // 2:lg // 2 + 1, :]
        blast = b[lg - 1:lg, :]
        st = gst_ref[h]
        att = _dot_nt((q * jnp.exp(b - bref)).astype(BF16), (k * jnp.exp(bref - b)).astype(BF16))
        att = jnp.where(causal, att, 0.0)
        inner = _dot(att.astype(BF16), v)
        cross = _dot_nt((q * jnp.exp(b)).astype(BF16), st.astype(BF16))
        kvt = _dot_tn(v, (k * jnp.exp(blast - b)).astype(BF16))
        gst_ref[h] = st * jnp.exp(blast) + kvt
        o = _rms(inner + cross) * gn_ref[:, h * GLA_DV:(h + 1) * GLA_DV] * _silu(g)
        o_ref[pl.ds(r0, lg), base + h * GLA_DV:base + (h + 1) * GLA_DV] = o.astype(o_ref.dtype)

    def step(c, carry):
        for u in range(RET_GLA_UNROLL):
            r0 = pl.multiple_of((c * RET_GLA_UNROLL + u) * lr, lr)
            for h in range(RET_HEADS):
                ret_chunk(r0, h)
            for j in range(lr // lg):
                for h in range(GLA_HEADS):
                    gla_chunk(pl.multiple_of(r0 + j * lg, lg), h)
        return carry

    lax.fori_loop(0, ls // (lr * RET_GLA_UNROLL), step, 0)


def _ret_gla(y, glog, ret_norm, gla_norm, bsz, seq, ls):
    t = y.shape[0]
    ns = seq // ls
    qw, vw = RET_HEADS * RET_DK, RET_HEADS * RET_DV
    L = RET_CHUNK
    log_gamma = jnp.log1p(-jnp.exp2(-5.0 - jnp.arange(RET_HEADS, dtype=F32)))
    idx = jnp.arange(L, dtype=F32)
    diff = idx[:, None] - idx[None, :]
    decay = jnp.where(diff >= 0, jnp.exp(log_gamma[:, None, None] * jnp.maximum(diff, 0.0)), 0.0)
    zeta = jnp.broadcast_to(jnp.exp(log_gamma[:, None] * (L - 1 - idx)[None, :])[:, :, None], (RET_HEADS, L, RET_DK))
    xi = jnp.broadcast_to(jnp.exp(log_gamma[:, None] * (idx + 1.0)[None, :])[:, :, None], (RET_HEADS, L, RET_DK))
    chunk_decay = tuple(float(np.exp(np.float32(np.log1p(-np.exp2(np.float32(-5.0 - i)))) * np.float32(L)))
                        for i in range(RET_HEADS))
    rowmap = lambda j: (lambda b, i: (b * ns + i, j))
    const = lambda b, i: (0, 0)
    const3 = lambda b, i: (0, 0, 0)
    return pl.pallas_call(
        functools.partial(_ret_gla_kernel, chunk_decay=chunk_decay),
        out_shape=jax.ShapeDtypeStruct((t, 2 * vw), BF16),
        grid=(bsz, ns),
        in_specs=[pl.BlockSpec((ls, qw), rowmap(0)),
                  pl.BlockSpec((ls, qw), rowmap(1)),
                  pl.BlockSpec((ls, vw), rowmap(1)),
                  pl.BlockSpec((ls, vw), rowmap(2)),
                  pl.BlockSpec((ls, qw), rowmap(6)),
                  pl.BlockSpec((ls, qw), rowmap(7)),
                  pl.BlockSpec((ls, vw), rowmap(4)),
                  pl.BlockSpec((ls, vw), rowmap(5)),
                  pl.BlockSpec((ls, qw), rowmap(0)),
                  pl.BlockSpec((1, vw), const),
                  pl.BlockSpec((1, vw), const),
                  pl.BlockSpec((RET_HEADS, L, L), const3),
                  pl.BlockSpec((RET_HEADS, L, RET_DK), const3),
                  pl.BlockSpec((RET_HEADS, L, RET_DK), const3)],
        out_specs=pl.BlockSpec((ls, 2 * vw), lambda b, i: (b * ns + i, 0)),
        scratch_shapes=[pltpu.VMEM((RET_HEADS, RET_DK, RET_DV), F32),
                        pltpu.VMEM((GLA_HEADS, GLA_DV, GLA_DK), F32)],
        compiler_params=_cparams(("arbitrary", "arbitrary")),
        name="ret_gla_mixer",
    )(y, y, y, y, y, y, y, y, glog, ret_norm, gla_norm, decay, zeta, xi)


def _odd_proj_kernel(x_ref, mod_ref, gain_ref, w_ref, wdt_ref, dtb_ref, y_ref, dt_ref, *, ncol, cch):
    h = _norm_mod(x_ref[...], gain_ref[...], mod_ref[0], 0, 1)
    hb = h.astype(BF16)
    width = w_ref.shape[1] // ncol
    for j in range(ncol):
        c0 = j * width
        acc = _dot(hb, w_ref[:, c0:c0 + width])
        if c0 >= cch:
            acc = _silu(acc)
        y_ref[:, c0:c0 + width] = acc.astype(y_ref.dtype)
    z = _dot(hb, wdt_ref[...]) + dtb_ref[...]
    dt_ref[...] = jnp.maximum(z, 0.0) + jnp.log1p(jnp.exp(-jnp.abs(z)))


def _odd_proj(x2d, mod_l, gain, w_main, w_dt, dt_bias, cch, tm, seq):
    t, d = x2d.shape
    n = w_main.shape[1]
    tiles_per_b = seq // tm
    const = lambda i: (0, 0)
    return pl.pallas_call(
        functools.partial(_odd_proj_kernel, ncol=n // 512, cch=cch),
        out_shape=(jax.ShapeDtypeStruct((t, n), BF16), jax.ShapeDtypeStruct((t, LANES), F32)),
        grid=(t // tm,),
        in_specs=[pl.BlockSpec((tm, d), lambda i: (i, 0)),
                  pl.BlockSpec((1, 6, d), lambda i: (i // tiles_per_b, 0, 0)),
                  pl.BlockSpec((1, d), const),
                  pl.BlockSpec((d, n), const),
                  pl.BlockSpec((d, LANES), const),
                  pl.BlockSpec((1, LANES), const)],
        out_specs=(pl.BlockSpec((tm, n), lambda i: (i, 0)), pl.BlockSpec((tm, LANES), lambda i: (i, 0))),
        compiler_params=_cparams(("arbitrary",)),
        name="odd_proj",
    )(x2d, mod_l, gain, w_main, w_dt, dt_bias)


def _ssd_kernel(z_ref, xbc_ref, dt_ref, cw_ref, cb_ref, alog_ref, dsk_ref, ng_ref, o_ref, hist_ref, st_ref):
    @pl.when(pl.program_id(1) == 0)
    def _():
        hist_ref[...] = jnp.zeros_like(hist_ref)
        st_ref[...] = jnp.zeros_like(st_ref)

    L = xbc_ref.shape[0]
    P, N, G = SSD_HEADDIM, SSD_STATE, SSD_GROUPS
    dinner = z_ref.shape[1]
    gw = dinner // G
    hpg = gw // P
    slab = gw + 2 * N
    hrows = hist_ref.shape[0]

    dt = dt_ref[...]
    da = dt * (-jnp.exp(alog_ref[...]))
    trow = _iota((L, L), 0)
    tcol = _iota((L, L), 1)
    causal = trow >= tcol
    cum = _dot_exact_rhs(causal.astype(BF16), da)
    cum_t = jnp.transpose(cum)
    dt_t = jnp.transpose(dt)
    ecum = jnp.exp(cum)
    wend = dt * jnp.exp(cum[L - 1:L, :] - cum)
    lane = _iota((L, LANES), 1)
    lo_half = lane < P

    def pair_cols(a, ha):
        return jnp.where(lo_half, jnp.broadcast_to(a[:, ha:ha + 1], (L, LANES)),
                         jnp.broadcast_to(a[:, ha + 1:ha + 2], (L, LANES)))

    srow = _iota((L, hrows + L), 0)
    scol = _iota((L, hrows + L), 1)
    shift = jnp.concatenate([(scol == srow + (hrows - s)).astype(BF16) for s in range(1, SSD_CONV)], axis=0)

    for g in range(G):
        c0 = g * slab
        xin = xbc_ref[:, c0:c0 + slab]
        xcat = jnp.concatenate([hist_ref[:, c0:c0 + slab], xin], axis=0)
        hist_ref[:, c0:c0 + slab] = xin[L - hrows:L]
        sh = _dot(shift, xcat)
        conv = xin.astype(F32) * cw_ref[SSD_CONV - 1:SSD_CONV, c0:c0 + slab] + cb_ref[:, c0:c0 + slab]
        for s in range(1, SSD_CONV):
            conv = conv + sh[(s - 1) * L:s * L] * cw_ref[SSD_CONV - 1 - s:SSD_CONV - s, c0:c0 + slab]
        act = _silu(conv)
        xs = act[:, :gw]
        bm_g = act[:, gw:gw + N].astype(BF16)
        cm_g = act[:, gw + N:].astype(BF16)

        cb = _dot_nt(cm_g, bm_g)
        xs_b = xs.astype(BF16)
        pieces, xw, ec = [], [], []
        for pr in range(hpg // 2):
            ha = g * hpg + 2 * pr
            ms = []
            for hh in (ha, ha + 1):
                seg = jnp.broadcast_to(cum[:, hh:hh + 1], (L, L)) - cum_t[hh:hh + 1, :]
                ms.append((cb * jnp.where(causal, jnp.exp(seg), 0.0) * dt_t[hh:hh + 1, :]).astype(BF16))
            lhs = jnp.concatenate(ms, axis=1)
            blk = xs_b[:, pr * 2 * P:(pr + 1) * 2 * P]
            zero = jnp.zeros_like(blk)
            rhs = jnp.concatenate([jnp.where(lo_half, blk, zero), jnp.where(lo_half, zero, blk)], axis=0)
            pieces.append(_dot(lhs, rhs))
            xw.append((xs[:, pr * 2 * P:(pr + 1) * 2 * P] * pair_cols(wend, ha)).astype(BF16))
            ec.append(pair_cols(ecum, ha))
        inner = jnp.concatenate(pieces, axis=1)
        ecum_g = jnp.concatenate(ec, axis=1)
        st = st_ref[g]
        cross = _dot(cm_g, st.astype(BF16)) * ecum_g
        st_ref[g] = st * ecum_g[L - 1:L, :] + _dot_tn(bm_g, jnp.concatenate(xw, axis=1))
        y = inner + cross + dsk_ref[:, g * gw:(g + 1) * gw] * xs
        y = y * z_ref[:, g * gw:(g + 1) * gw].astype(F32)
        o_ref[:, g * gw:(g + 1) * gw] = (_rms(y) * ng_ref[:, g * gw:(g + 1) * gw]).astype(o_ref.dtype)


SSD_HIST_ROWS = 16


def _ssd(y, dt, conv_w, conv_b, a_log_row, d_row, norm_g, bsz, seq):
    t = y.shape[0]
    L = SSD_CHUNK
    ns = seq // L
    dinner = norm_g.shape[1]
    cch = conv_w.shape[1]
    gw = dinner // SSD_GROUPS
    const = lambda b, i: (0, 0)
    assert cch % dinner == 0
    return pl.pallas_call(
        _ssd_kernel,
        out_shape=jax.ShapeDtypeStruct((t, dinner), BF16),
        grid=(bsz, ns),
        in_specs=[pl.BlockSpec((L, dinner), lambda b, i: (b * ns + i, cch // dinner)),
                  pl.BlockSpec((L, cch), lambda b, i: (b * ns + i, 0)),
                  pl.BlockSpec((L, LANES), lambda b, i: (b * ns + i, 0)),
                  pl.BlockSpec((SSD_CONV, cch), const),
                  pl.BlockSpec((1, cch), const),
                  pl.BlockSpec((1, LANES), const),
                  pl.BlockSpec((1, dinner), const),
                  pl.BlockSpec((1, dinner), const)],
        out_specs=pl.BlockSpec((L, dinner), lambda b, i: (b * ns + i, 0)),
        scratch_shapes=[pltpu.VMEM((SSD_HIST_ROWS, cch), BF16),
                        pltpu.VMEM((SSD_GROUPS, SSD_STATE, gw), F32)],
        compiler_params=_cparams(("arbitrary", "arbitrary")),
        name="ssd_mixer",
    )(y, y, dt, conv_w, conv_b, a_log_row, d_row, norm_g)


def _outproj_kernel(o_ref, x_ref, mod_ref, gain_ref, w_ref, x1_ref, h2_ref):
    mod = mod_ref[0]
    x1 = x_ref[...] + mod[2:3, :] * _dot(o_ref[...], w_ref[...])
    x1_ref[...] = x1
    h2_ref[...] = _pack_pair(_norm_mod(x1, gain_ref[...], mod, 3, 4))


def _outproj(o, x2d, mod_l, gain2, w_out, tm, seq):
    t, d = x2d.shape
    kin = o.shape[1]
    tiles_per_b = seq // tm
    const = lambda i: (0, 0)
    return pl.pallas_call(
        _outproj_kernel,
        out_shape=(jax.ShapeDtypeStruct((t, d), F32), jax.ShapeDtypeStruct((t, _packed_width(d)), PACKED)),
        grid=(t // tm,),
        in_specs=[pl.BlockSpec((tm, kin), lambda i: (i, 0)),
                  pl.BlockSpec((tm, d), lambda i: (i, 0)),
                  pl.BlockSpec((1, 6, d), lambda i: (i // tiles_per_b, 0, 0)),
                  pl.BlockSpec((1, d), const),
                  pl.BlockSpec((kin, d), const)],
        out_specs=(pl.BlockSpec((tm, d), lambda i: (i, 0)), pl.BlockSpec((tm, _packed_width(d)), lambda i: (i, 0))),
        compiler_params=_cparams(("arbitrary",)),
        name="out_proj",
    )(o, x2d, mod_l, gain2, w_out)


def _route_kernel(h_ref, rw_ref, rb_ref, dest_ref, gate_ref, blk_ref, cnt_ref, pst_ref, *, nbpad):
    phase = pl.program_id(0)
    step = pl.program_id(1)
    tm = h_ref.shape[0]
    ne = rw_ref.shape[0]
    epg = ne // N_GROUPS

    @pl.when(jnp.logical_and(phase == 0, step == 0))
    def _():
        cnt_ref[...] = jnp.zeros_like(cnt_ref)
        pst_ref[...] = jnp.zeros_like(pst_ref)

    @pl.when(jnp.logical_and(phase == 1, step == 0))
    def _():
        nblk = jnp.floor((cnt_ref[...] + (MOE_BLOCK - 1.0)) * (1.0 / MOE_BLOCK))
        hi = jnp.floor(nblk * (1.0 / 16.0))
        lo = nblk - hi * 16.0
        er = _iota((ne, ne), 0)
        ec = _iota((ne, ne), 1)
        lower = (ec < er).astype(BF16)
        pst = _dot(lower, hi.astype(BF16)) * 16.0 + _dot(lower, lo.astype(BF16))
        pst_ref[...] = pst
        pend = jnp.concatenate([pst + nblk] * (nbpad // LANES), axis=1)
        bidx = _iota((ne, nbpad), 1).astype(F32)
        be = jnp.sum(jnp.where(pend <= bidx, 1.0, 0.0), axis=0, keepdims=True)
        total = jnp.max(pend, axis=0, keepdims=True)
        blk_ref[0:1, :] = jnp.minimum(be, ne - 1.0).astype(I32)
        blk_ref[1:2, :] = total.astype(I32)
        cnt_ref[...] = jnp.zeros_like(cnt_ref)

    logits = _dot_nt(rw_ref[...], _unpack_pair(h_ref[...]).astype(BF16))
    s = jax.nn.sigmoid(logits)
    sb = s + jnp.concatenate([rb_ref[...]] * (tm // LANES), axis=1)
    G = N_GROUPS
    v = [sb[j * G:(j + 1) * G, :] for j in range(epg)]
    m01, n01 = jnp.maximum(v[0], v[1]), jnp.minimum(v[0], v[1])
    m23, n23 = jnp.maximum(v[2], v[3]), jnp.minimum(v[2], v[3])
    top1 = jnp.maximum(m01, m23)
    top2 = jnp.maximum(jnp.minimum(m01, m23), jnp.maximum(n01, n23))
    gscore = top1 + top2
    gi = _iota((G, tm), 0)
    gmax = jnp.max(gscore, axis=0, keepdims=True)
    gidx = jnp.min(jnp.where(gscore == gmax, gi, G), axis=0, keepdims=True)
    sel = gi == gidx
    vb = [jnp.sum(jnp.where(sel, v[j], 0.0), axis=0, keepdims=True) for j in range(epg)]
    vs = [jnp.sum(jnp.where(sel, s[j * G:(j + 1) * G, :], 0.0), axis=0, keepdims=True) for j in range(epg)]
    zero = jnp.zeros_like(vb[0])
    e1 = zero
    e2 = zero
    w1 = zero
    w2 = zero
    for j in range(epg):
        rank = zero
        for i in range(epg):
            if i == j:
                continue
            ahead = (vb[i] >= vb[j]) if i < j else (vb[i] > vb[j])
            rank = rank + jnp.where(ahead, 1.0, 0.0)
        first = rank == 0.0
        second = rank == 1.0
        e1 = e1 + jnp.where(first, float(j), 0.0)
        e2 = e2 + jnp.where(second, float(j), 0.0)
        w1 = w1 + jnp.where(first, vs[j], 0.0)
        w2 = w2 + jnp.where(second, vs[j], 0.0)
    gf = gidx.astype(F32) * float(epg)
    e1 = e1 + gf
    e2 = e2 + gf
    wsum = w1 + w2
    ei = _iota((ne, tm), 0).astype(F32)
    oh1 = ei == e1
    oh2 = ei == e2
    onehot = jnp.where(oh1 | oh2, 1.0, 0.0)
    upper = (_iota((tm, tm), 0) < _iota((tm, tm), 1)).astype(BF16)
    prefix = _dot(onehot.astype(BF16), upper)
    cnt = cnt_ref[...]
    base = jnp.concatenate([cnt] * (tm // LANES), axis=1) + prefix
    pst = jnp.concatenate([pst_ref[...]] * (tm // LANES), axis=1) * float(MOE_BLOCK)
    slot = base + pst
    d1 = jnp.sum(jnp.where(oh1, slot, 0.0), axis=0, keepdims=True)
    d2 = jnp.sum(jnp.where(oh2, slot, 0.0), axis=0, keepdims=True)
    cnt_ref[...] = cnt + jnp.sum(onehot, axis=1, keepdims=True)
    dest_ref[0, 0:1, :] = d1.astype(I32)
    dest_ref[0, 1:2, :] = d2.astype(I32)
    gate_ref[0, 0:1, :] = w1 / wsum
    gate_ref[0, 1:2, :] = w2 / wsum


def _route(h2, rw_t, rb_b, tm, nbpad):
    t, dp = h2.shape
    ne, d = rw_t.shape
    nt = t // tm
    return pl.pallas_call(
        functools.partial(_route_kernel, nbpad=nbpad),
        out_shape=(jax.ShapeDtypeStruct((nt, 2, tm), I32), jax.ShapeDtypeStruct((nt, 2, tm), F32),
                   jax.ShapeDtypeStruct((2, nbpad), I32)),
        grid=(2, nt),
        in_specs=[pl.BlockSpec((tm, dp), lambda p, i: (i, 0)),
                  pl.BlockSpec((ne, d), lambda p, i: (0, 0)),
                  pl.BlockSpec((ne, LANES), lambda p, i: (0, 0))],
        out_specs=(pl.BlockSpec((1, 2, tm), lambda p, i: (i * p, 0, 0)),
                   pl.BlockSpec((1, 2, tm), lambda p, i: (i * p, 0, 0)),
                   pl.BlockSpec((2, nbpad), lambda p, i: (0, 0))),
        scratch_shapes=[pltpu.VMEM((ne, LANES), F32), pltpu.VMEM((ne, LANES), F32)],
        compiler_params=_cparams(("arbitrary", "arbitrary")),
        name="router",
    )(h2, rw_t, rb_b)


SC_CORES, SC_SUBCORES = 2, 16
SC_ROWS = 64


def _sc_mesh():
    return plsc.VectorSubcoreMesh(core_axis_name="c", subcore_axis_name="s",
                                  num_cores=SC_CORES, num_subcores=SC_SUBCORES)


def _sc_worker_base(rows_per_worker):
    return (lax.axis_index("s") * SC_CORES + lax.axis_index("c")) * rows_per_worker


def _sc_scatter_rows(src, idx, n_out):
    t, d = src.shape
    assert idx.shape[0] == TOP_K * t
    per_w = t // (SC_CORES * SC_SUBCORES)
    assert per_w % (2 * SC_ROWS) == 0
    rows = pltpu.VMEM((SC_ROWS, d), src.dtype)
    ids = pltpu.VMEM((SC_ROWS,), I32)

    @functools.partial(
        pl.kernel, mesh=_sc_mesh(), out_type=jax.ShapeDtypeStruct((n_out, d), src.dtype),
        scratch_types=[rows, rows, ids, ids, ids, ids] + [pltpu.SemaphoreType.DMA] * 4,
        compiler_params=pltpu.CompilerParams(use_tc_tiling_on_sc=True), name="moe_dispatch_sc")
    def k(src_hbm, idx_hbm, out_hbm, rows_a, rows_b, ia0, ia1, ib0, ib1, s0, s1, s2, s3):
        base = _sc_worker_base(per_w)

        @pl.loop(0, per_w // (2 * SC_ROWS))
        def _(j):
            off_a = pl.multiple_of(base + 2 * j * SC_ROWS, SC_ROWS)
            off_b = pl.multiple_of(off_a + SC_ROWS, SC_ROWS)
            load_a = pltpu.async_copy(src_hbm.at[pl.ds(off_a, SC_ROWS)], rows_a, s0)
            load_b = pltpu.async_copy(src_hbm.at[pl.ds(off_b, SC_ROWS)], rows_b, s1)
            pltpu.sync_copy(idx_hbm.at[pl.ds(off_a, SC_ROWS)], ia0)
            pltpu.sync_copy(idx_hbm.at[pl.ds(t + off_a, SC_ROWS)], ia1)
            pltpu.sync_copy(idx_hbm.at[pl.ds(off_b, SC_ROWS)], ib0)
            pltpu.sync_copy(idx_hbm.at[pl.ds(t + off_b, SC_ROWS)], ib1)
            load_a.wait()
            put_a0 = pltpu.async_copy(rows_a, out_hbm.at[ia0], s0)
            put_a1 = pltpu.async_copy(rows_a, out_hbm.at[ia1], s2)
            load_b.wait()
            put_b0 = pltpu.async_copy(rows_b, out_hbm.at[ib0], s1)
            put_b1 = pltpu.async_copy(rows_b, out_hbm.at[ib1], s3)
            put_a0.wait()
            put_a1.wait()
            put_b0.wait()
            put_b1.wait()

    return k(src, idx)


def _sc_gather_rows(table, idx):
    _, d = table.shape
    b = idx.shape[0]
    per_w = b // (SC_CORES * SC_SUBCORES)
    assert per_w % (2 * SC_ROWS) == 0
    rows = pltpu.VMEM((SC_ROWS, d), table.dtype)
    ids = pltpu.VMEM((SC_ROWS,), I32)

    @functools.partial(
        pl.kernel, mesh=_sc_mesh(), out_type=jax.ShapeDtypeStruct((b, d), table.dtype),
        scratch_types=[rows, rows, ids, ids] + [pltpu.SemaphoreType.DMA] * 2,
        compiler_params=pltpu.CompilerParams(use_tc_tiling_on_sc=True), name="moe_gather_sc")
    def k(table_hbm, idx_hbm, out_hbm, rows_a, rows_b, ia, ib, s0, s1):
        base = _sc_worker_base(per_w)

        @pl.loop(0, per_w // (2 * SC_ROWS))
        def _(j):
            off_a = pl.multiple_of(base + 2 * j * SC_ROWS, SC_ROWS)
            off_b = pl.multiple_of(off_a + SC_ROWS, SC_ROWS)
            pltpu.sync_copy(idx_hbm.at[pl.ds(off_a, SC_ROWS)], ia)
            pltpu.sync_copy(idx_hbm.at[pl.ds(off_b, SC_ROWS)], ib)
            get_a = pltpu.async_copy(table_hbm.at[ia], rows_a, s0)
            get_b = pltpu.async_copy(table_hbm.at[ib], rows_b, s1)
            get_a.wait()
            put_a = pltpu.async_copy(rows_a, out_hbm.at[pl.ds(off_a, SC_ROWS)], s0)
            get_b.wait()
            put_b = pltpu.async_copy(rows_b, out_hbm.at[pl.ds(off_b, SC_ROWS)], s1)
            put_a.wait()
            put_b.wait()

    return k(table, idx)


def _expert_kernel(blk_ref, nused_ref, xs_ref, wg_ref, wu_ref, wd_ref, y_ref, wgb, wub, wdb):
    b = pl.program_id(0)
    prev = blk_ref[jnp.maximum(b - 1, 0)]
    changed = jnp.logical_or(b == 0, blk_ref[b] != prev)

    @pl.when(changed)
    def _():
        wgb[...] = wg_ref[0, 0].astype(BF16)
        wub[...] = wu_ref[0, 0].astype(BF16)
        wdb[...] = wd_ref[0, 0].astype(BF16)

    @pl.when(b < nused_ref[0])
    def _():
        xb = _unpack_pair(xs_ref[...]).astype(BF16)
        g = _dot(xb, wgb[...])
        u = _dot(xb, wub[...])
        a = (_silu(g) * u).astype(BF16)
        y_ref[...] = _pack_pair(_dot(a, wdb[...]))

    @pl.when(b >= nused_ref[0])
    def _():
        y_ref[...] = jnp.zeros_like(y_ref)


def _experts(xs, blk_e, nused, w_gate, w_up, w_down, layer):
    np_rows, dp = xs.shape
    _, ne, d, ff = w_gate.shape
    nb = np_rows // MOE_BLOCK
    grid_spec = pltpu.PrefetchScalarGridSpec(
        num_scalar_prefetch=2,
        grid=(nb,),
        in_specs=[pl.BlockSpec((MOE_BLOCK, dp), lambda b, be, nu: (b, 0)),
                  pl.BlockSpec((1, 1, d, ff), lambda b, be, nu: (layer, be[b], 0, 0)),
                  pl.BlockSpec((1, 1, d, ff), lambda b, be, nu: (layer, be[b], 0, 0)),
                  pl.BlockSpec((1, 1, ff, d), lambda b, be, nu: (layer, be[b], 0, 0))],
        out_specs=pl.BlockSpec((MOE_BLOCK, dp), lambda b, be, nu: (b, 0)),
        scratch_shapes=[pltpu.VMEM((d, ff), BF16), pltpu.VMEM((d, ff), BF16), pltpu.VMEM((ff, d), BF16)],
    )
    return pl.pallas_call(
        _expert_kernel,
        out_shape=jax.ShapeDtypeStruct((np_rows, dp), PACKED),
        grid_spec=grid_spec,
        compiler_params=_cparams(("arbitrary",)),
        name="moe_experts",
    )(blk_e, nused, xs, w_gate, w_up, w_down)


def _combine_kernel(x1_ref, y1_ref, y2_ref, gate_ref, mod_ref, fg_ref, o_ref, *, final):
    tc, d = x1_ref.shape
    gates = gate_ref[0]
    acc = jnp.zeros((tc, d), F32)
    for k, y_ref in enumerate((y1_ref, y2_ref)):
        gcol = jnp.transpose(jnp.broadcast_to(gates[k:k + 1, :], (LANES, tc)))
        acc = acc + _unpack_pair(y_ref[...]) * jnp.concatenate([gcol] * (d // LANES), axis=1)
    x2 = x1_ref[...] + mod_ref[0][5:6, :] * acc
    if final:
        x2 = _rms(x2) * fg_ref[...]
    o_ref[...] = x2


def _combine(x1, y12, gates, mod_l, final_gain, tc, seq, final):
    t, d = x1.shape
    nt = t // tc
    tiles_per_b = seq // tc
    return pl.pallas_call(
        functools.partial(_combine_kernel, final=final),
        out_shape=jax.ShapeDtypeStruct((t, d), F32),
        grid=(nt,),
        in_specs=[pl.BlockSpec((tc, d), lambda i: (i, 0)),
                  pl.BlockSpec((tc, _packed_width(d)), lambda i: (i, 0)),
                  pl.BlockSpec((tc, _packed_width(d)), lambda i: (nt + i, 0)),
                  pl.BlockSpec((1, TOP_K, tc), lambda i: (i, 0, 0)),
                  pl.BlockSpec((1, 6, d), lambda i: (i // tiles_per_b, 0, 0)),
                  pl.BlockSpec((1, d), lambda i: (0, 0))],
        out_specs=pl.BlockSpec((tc, d), lambda i: (i, 0)),
        compiler_params=_cparams(("arbitrary",)),
        name="moe_combine",
    )(x1, y12, y12, gates, mod_l, final_gain)


ROUTE_TILE = 512


def _moe(x1, h2, mod_l, rw_t, rb_b, w_gate, w_up, w_down, layer, final_gain, seq, final):
    t, d = x1.shape
    ne = w_gate.shape[1]
    nb = (t * TOP_K + ne * (MOE_BLOCK - 1) + MOE_BLOCK - 1) // MOE_BLOCK
    nbpad = -(-nb // LANES) * LANES
    tm = ROUTE_TILE
    dest, gates, blk = _route(h2, rw_t, rb_b, tm, nbpad)
    slot = jnp.transpose(dest, (1, 0, 2)).reshape(TOP_K * t)
    xs = _sc_scatter_rows(h2, slot, nb * MOE_BLOCK)
    ys = _experts(xs, blk[0, :nb], blk[1, :1], w_gate, w_up, w_down, layer)
    y12 = _sc_gather_rows(ys, slot)
    return _combine(x1, y12, gates, mod_l, final_gain, tm, seq, final)


def kernel(x, c, positions, mod_w, mod_b, norm_mix, norm_ffn, ev_w_in, ev_ret_norm, ev_gla_wa2, ev_gla_ba,
           ev_gla_norm, ev_w_out, od_w_in, od_conv_w, od_conv_b, od_dt_bias, od_a_log, od_d, od_norm, od_w_out,
           router_w, router_b, exp_w_gate, exp_w_up, exp_w_down, final_norm):
    bsz, seq, d = x.shape
    depth = mod_w.shape[0]
    t = bsz * seq
    ne = router_w.shape[1]
    epg = ne // N_GROUPS
    mod = _modulation(c, mod_w, mod_b)
    x2d = x.reshape(t, d)

    perm = np.array([g * epg + j for j in range(epg) for g in range(N_GROUPS)])
    rw_t = router_w.T[perm].astype(BF16)
    rb_b = jnp.broadcast_to(router_b.astype(F32)[perm][:, None], (ne, LANES))
    final_gain = final_norm.reshape(1, d)

    half = RET_DK // 2
    inv = ROPE_BASE ** (-jnp.arange(half, dtype=F32) / half)
    inv2 = jnp.concatenate([inv, inv]).reshape(1, LANES)
    pos_col = positions.reshape(t, 1)

    for layer in range(depth):
        i = layer // 2
        mod_l = mod[layer]
        gain1 = norm_mix[layer].reshape(1, d)
        gain2 = norm_ffn[layer].reshape(1, d)
        if layer % 2 == 0:
            w_in = ev_w_in[i]
            nmain = w_in.shape[1] - GLA_RANK
            w_main = w_in[:, :nmain].astype(BF16)
            w_ga = jnp.pad(w_in[:, nmain:], ((0, 0), (0, LANES - GLA_RANK))).astype(BF16)
            wa2 = jnp.pad(ev_gla_wa2[i], ((0, LANES - GLA_RANK), (0, 0))).astype(BF16)
            y, glog = _even_proj(x2d, mod_l, gain1, pos_col, inv2, w_main, w_ga, wa2,
                                 ev_gla_ba[i].reshape(1, -1), 512, seq)
            o = _ret_gla(y, glog, ev_ret_norm[i].reshape(1, -1), ev_gla_norm[i].reshape(1, -1), bsz, seq, 512)
            w_out = ev_w_out[i].astype(BF16)
        else:
            w_in = od_w_in[i]
            nheads = od_a_log.shape[1]
            nmain = w_in.shape[1] - nheads
            dinner = od_norm.shape[1]
            gw, ns_ = dinner // SSD_GROUPS, SSD_STATE
            bc0 = dinner + SSD_GROUPS * ns_
            cperm = np.concatenate([np.concatenate([np.arange(g * gw, (g + 1) * gw),
                                                    dinner + np.arange(g * ns_, (g + 1) * ns_),
                                                    bc0 + np.arange(g * ns_, (g + 1) * ns_)])
                                    for g in range(SSD_GROUPS)])
            conv_w = od_conv_w[i][:, cperm]
            conv_b = od_conv_b[i][cperm].reshape(1, -1)
            w_main = jnp.concatenate([w_in[:, dinner + cperm], w_in[:, :dinner]], axis=1).astype(BF16)
            w_dt = jnp.pad(w_in[:, nmain:], ((0, 0), (0, LANES - nheads))).astype(BF16)
            dtb = jnp.pad(od_dt_bias[i], (0, LANES - nheads)).reshape(1, LANES)
            y, dt = _odd_proj(x2d, mod_l, gain1, w_main, w_dt, dtb, od_conv_w.shape[2], 512, seq)
            alog = jnp.pad(od_a_log[i], (0, LANES - nheads)).reshape(1, LANES)
            d_row = jnp.repeat(od_d[i], SSD_HEADDIM).reshape(1, dinner)
            o = _ssd(y, dt, conv_w, conv_b, alog, d_row, od_norm[i].reshape(1, dinner), bsz, seq)
            w_out = od_w_out[i].astype(BF16)
        x1, h2 = _outproj(o, x2d, mod_l, gain2, w_out, 512, seq)
        x2d = _moe(x1, h2, mod_l, rw_t, rb_b, exp_w_gate, exp_w_up, exp_w_down, layer,
                   final_gain, seq, final=(layer == depth - 1))
    return x2d.reshape(bsz, seq, d)
```

```python
import functools

import jax
import jax.numpy as jnp
import numpy as np
from jax import lax
from jax.experimental import pallas as pl
from jax.experimental.pallas import tpu as pltpu
from jax.experimental.pallas import tpu_sc as plsc

F32 = jnp.float32
BF16 = jnp.bfloat16
I32 = jnp.int32

RET_HEADS, RET_DK, RET_DV, RET_CHUNK = 4, 128, 256, 128
ROPE_BASE = 10000.0
GLA_HEADS, GLA_DK, GLA_DV, GLA_RANK, GLA_TAU, GLA_CHUNK = 4, 128, 256, 16, 16.0, 64
SSD_HEADDIM, SSD_GROUPS, SSD_STATE, SSD_CONV, SSD_CHUNK = 64, 8, 128, 4, 128
N_GROUPS, TOP_K, GROUP_SCORE_K, MOE_BLOCK = 8, 2, 2, 256
NORM_EPS = 1e-6

PROJ_TILE = 512
LANES = 128
VMEM_LIMIT = 56 * 1024 * 1024


def _cparams(sem, vmem=VMEM_LIMIT):
    return pltpu.CompilerParams(dimension_semantics=sem, vmem_limit_bytes=vmem)


def _dot(a, b):
    return jnp.dot(a, b, preferred_element_type=F32)


def _dot_nt(a, b):
    return lax.dot_general(a, b, (((1,), (1,)), ((), ())), preferred_element_type=F32)


def _dot_tn(a, b):
    return lax.dot_general(a, b, (((0,), (0,)), ((), ())), preferred_element_type=F32)


def _split3(x):
    a = x.astype(BF16)
    r = x - a.astype(F32)
    b = r.astype(BF16)
    c = (r - b.astype(F32)).astype(BF16)
    return a, b, c


def _dot_exact_rhs(m01, x):
    a, b, c = _split3(x)
    return _dot(m01, a) + _dot(m01, b) + _dot(m01, c)


def _dot_exact_lhs(x, m01):
    a, b, c = _split3(x)
    return _dot(a, m01) + _dot(b, m01) + _dot(c, m01)


def _dot_tn_exact(x, m01):
    a, b, c = _split3(x)
    return _dot_tn(a, m01) + _dot_tn(b, m01) + _dot_tn(c, m01)


def _silu(x):
    return x * jax.nn.sigmoid(x)


def _rms(x, eps=NORM_EPS):
    return x * lax.rsqrt(jnp.mean(x * x, axis=-1, keepdims=True) + eps)


def _iota(shape, dim):
    return lax.broadcasted_iota(I32, shape, dim)


PACKED = jnp.uint32
_HI16 = 0xFFFF0000


def _packed_width(n):
    return n // 2


def _pack_pair(x):
    n = x.shape[1] // 2
    lo = lax.bitcast_convert_type(x[:, :n].astype(BF16).astype(F32), jnp.uint32)
    hi = lax.bitcast_convert_type(x[:, n:].astype(BF16).astype(F32), jnp.uint32)
    return (hi & jnp.uint32(_HI16)) | (lo >> 16)


def _unpack_pair(w):
    lo = lax.bitcast_convert_type(w << 16, F32)
    hi = lax.bitcast_convert_type(w & jnp.uint32(_HI16), F32)
    return jnp.concatenate([lo, hi], axis=1)


def _mod_kernel(c_ref, w_ref, b_ref, o_ref):
    sc = _silu(c_ref[...])
    o_ref[0] = _dot(sc.astype(BF16), w_ref[0].astype(BF16)) + b_ref[0]


def _modulation(c, mod_w, mod_b):
    depth, d, d6 = mod_w.shape
    bsz = c.shape[0]
    nb = d6 // d
    out = pl.pallas_call(
        _mod_kernel,
        out_shape=jax.ShapeDtypeStruct((depth, bsz, d6), F32),
        grid=(depth, nb),
        in_specs=[pl.BlockSpec((bsz, d), lambda l, j: (0, 0)),
                  pl.BlockSpec((1, d, d), lambda l, j: (l, 0, j)),
                  pl.BlockSpec((1, 1, d), lambda l, j: (l, 0, j))],
        out_specs=pl.BlockSpec((1, bsz, d), lambda l, j: (l, 0, j)),
        compiler_params=_cparams(("arbitrary", "arbitrary")),
        name="adaln_mod",
    )(c, mod_w, mod_b.reshape(depth, 1, d6))
    return out.reshape(depth, bsz, nb, d)


def _norm_mod(x, gain, mod, shift_row, scale_row):
    y = _rms(x) * gain
    return y * (1.0 + mod[scale_row:scale_row + 1, :]) + mod[shift_row:shift_row + 1, :]


def _even_proj_kernel(x_ref, mod_ref, gain_ref, pos_ref, inv_ref, w_ref, wga_ref, wa2_ref, ba_ref,
                      y_ref, glog_ref, *, ncol):
    h = _norm_mod(x_ref[...], gain_ref[...], mod_ref[0], 0, 1)
    hb = h.astype(BF16)
    tm = hb.shape[0]
    ang = pos_ref[...].astype(F32) * inv_ref[...]
    cos = jnp.cos(ang)
    lane = _iota((tm, LANES), 1)
    sin = jnp.where(lane < LANES // 2, -jnp.sin(ang), jnp.sin(ang))
    qk_scale = RET_DK ** -0.5
    rqk = RET_HEADS * RET_DK
    rv = RET_HEADS * RET_DV
    gq0 = 2 * rqk + 2 * rv
    width = w_ref.shape[1] // ncol
    for j in range(ncol):
        c0 = j * width
        acc = _dot(hb, w_ref[:, c0:c0 + width])
        for s in range(width // LANES):
            col = c0 + s * LANES
            blk = acc[:, s * LANES:(s + 1) * LANES]
            if col < 2 * rqk:
                blk = blk * cos + pltpu.roll(blk, LANES // 2, 1) * sin
                if col >= rqk:
                    blk = blk * qk_scale
            elif gq0 <= col < gq0 + GLA_HEADS * GLA_DK:
                blk = blk * (GLA_DK ** -0.5)
            y_ref[:, col:col + LANES] = blk.astype(y_ref.dtype)
    ga = _dot(hb, wga_ref[...])
    z = _dot(ga.astype(BF16), wa2_ref[...]) + ba_ref[...]
    logsig = jnp.minimum(z, 0.0) - jnp.log1p(jnp.exp(-jnp.abs(z)))
    glog_ref[...] = logsig * (1.0 / GLA_TAU)


def _even_proj(x2d, mod_l, gain, pos_col, inv2, w_main, w_ga, wa2, ba, tm, seq):
    t, d = x2d.shape
    n = w_main.shape[1]
    tiles_per_b = seq // tm
    gk = wa2.shape[1]
    const = lambda i: (0, 0)
    return pl.pallas_call(
        functools.partial(_even_proj_kernel, ncol=n // 512),
        out_shape=(jax.ShapeDtypeStruct((t, n), BF16), jax.ShapeDtypeStruct((t, gk), F32)),
        grid=(t // tm,),
        in_specs=[pl.BlockSpec((tm, d), lambda i: (i, 0)),
                  pl.BlockSpec((1, 6, d), lambda i: (i // tiles_per_b, 0, 0)),
                  pl.BlockSpec((1, d), const),
                  pl.BlockSpec((tm, 1), lambda i: (i, 0)),
                  pl.BlockSpec((1, LANES), const),
                  pl.BlockSpec((d, n), const, pipeline_mode=pl.Buffered(1)),
                  pl.BlockSpec((d, LANES), const),
                  pl.BlockSpec((LANES, gk), const),
                  pl.BlockSpec((1, gk), const)],
        out_specs=(pl.BlockSpec((tm, n), lambda i: (i, 0)), pl.BlockSpec((tm, gk), lambda i: (i, 0))),
        compiler_params=_cparams(("arbitrary",)),
        name="even_proj",
    )(x2d, mod_l, gain, pos_col, inv2, w_main, w_ga, wa2, ba)


RET_GLA_UNROLL = 4


def _ret_gla_kernel(rq_ref, rk_ref, rv_ref, rg_ref, gq_ref, gk_ref, gv_ref, gg_ref, gl_ref,
                    rn_ref, gn_ref, dec_ref, zeta_ref, xi_ref, o_ref, rst_ref, gst_ref, *, chunk_decay):
    @pl.when(pl.program_id(1) == 0)
    def _():
        rst_ref[...] = jnp.zeros_like(rst_ref)
        gst_ref[...] = jnp.zeros_like(gst_ref)

    ls = rq_ref.shape[0]
    lr, lg = RET_CHUNK, GLA_CHUNK
    trow = _iota((lg, lg), 0)
    tcol = _iota((lg, lg), 1)
    causal = trow >= tcol
    tri = causal.astype(BF16)
    base = RET_HEADS * RET_DV

    def ret_chunk(r0, h):
        q = rq_ref[pl.ds(r0, lr), h * RET_DK:(h + 1) * RET_DK]
        k = rk_ref[pl.ds(r0, lr), h * RET_DK:(h + 1) * RET_DK]
        v = rv_ref[pl.ds(r0, lr), h * RET_DV:(h + 1) * RET_DV]
        g = rg_ref[pl.ds(r0, lr), h * RET_DV:(h + 1) * RET_DV].astype(F32)
        st = rst_ref[h]
        s = _dot_nt(q, k) * dec_ref[h]
        inner = _dot(s.astype(BF16), v)
        cross = _dot((q.astype(F32) * xi_ref[h]).astype(BF16), st.astype(BF16))
        kv = _dot_tn((k.astype(F32) * zeta_ref[h]).astype(BF16), v)
        rst_ref[h] = st * chunk_decay[h] + kv
        o = _rms(inner + cross) * rn_ref[:, h * RET_DV:(h + 1) * RET_DV] * _silu(g)
        o_ref[pl.ds(r0, lr), h * RET_DV:(h + 1) * RET_DV] = o.astype(o_ref.dtype)

    def gla_chunk(r0, h):
        q = gq_ref[pl.ds(r0, lg), h * GLA_DK:(h + 1) * GLA_DK].astype(F32)
        k = gk_ref[pl.ds(r0, lg), h * GLA_DK:(h + 1) * GLA_DK].astype(F32)
        v = gv_ref[pl.ds(r0, lg), h * GLA_DV:(h + 1) * GLA_DV]
        g = gg_ref[pl.ds(r0, lg), h * GLA_DV:(h + 1) * GLA_DV].astype(F32)
        gl = gl_ref[pl.ds(r0, lg), h * GLA_DK:(h + 1) * GLA_DK]
        b = _dot_exact_rhs(tri, gl)
        bref = b[lg // 2:lg // 2 + 1, :]
        blast = b[lg - 1:lg, :]
        st = gst_ref[h]
        att = _dot_nt((q * jnp.exp(b - bref)).astype(BF16), (k * jnp.exp(bref - b)).astype(BF16))
        att = jnp.where(causal, att, 0.0)
        inner = _dot(att.astype(BF16), v)
        cross = _dot_nt((q * jnp.exp(b)).astype(BF16), st.astype(BF16))
        kvt = _dot_tn(v, (k * jnp.exp(blast - b)).astype(BF16))
        gst_ref[h] = st * jnp.exp(blast) + kvt
        o = _rms(inner + cross) * gn_ref[:, h * GLA_DV:(h + 1) * GLA_DV] * _silu(g)
        o_ref[pl.ds(r0, lg), base + h * GLA_DV:base + (h + 1) * GLA_DV] = o.astype(o_ref.dtype)

    def step(c, carry):
        for u in range(RET_GLA_UNROLL):
            r0 = pl.multiple_of((c * RET_GLA_UNROLL + u) * lr, lr)
            for h in range(RET_HEADS):
                ret_chunk(r0, h)
            for j in range(lr // lg):
                for h in range(GLA_HEADS):
                    gla_chunk(pl.multiple_of(r0 + j * lg, lg), h)
        return carry

    lax.fori_loop(0, ls // (lr * RET_GLA_UNROLL), step, 0)


def _ret_gla(y, glog, ret_norm, gla_norm, bsz, seq, ls):
    t = y.shape[0]
    ns = seq // ls
    qw, vw = RET_HEADS * RET_DK, RET_HEADS * RET_DV
    L = RET_CHUNK
    log_gamma = jnp.log1p(-jnp.exp2(-5.0 - jnp.arange(RET_HEADS, dtype=F32)))
    idx = jnp.arange(L, dtype=F32)
    diff = idx[:, None] - idx[None, :]
    decay = jnp.where(diff >= 0, jnp.exp(log_gamma[:, None, None] * jnp.maximum(diff, 0.0)), 0.0)
    zeta = jnp.broadcast_to(jnp.exp(log_gamma[:, None] * (L - 1 - idx)[None, :])[:, :, None], (RET_HEADS, L, RET_DK))
    xi = jnp.broadcast_to(jnp.exp(log_gamma[:, None] * (idx + 1.0)[None, :])[:, :, None], (RET_HEADS, L, RET_DK))
    chunk_decay = tuple(float(np.exp(np.float32(np.log1p(-np.exp2(np.float32(-5.0 - i)))) * np.float32(L)))
                        for i in range(RET_HEADS))
    rowmap = lambda j: (lambda b, i: (b * ns + i, j))
    const = lambda b, i: (0, 0)
    const3 = lambda b, i: (0, 0, 0)
    return pl.pallas_call(
        functools.partial(_ret_gla_kernel, chunk_decay=chunk_decay),
        out_shape=jax.ShapeDtypeStruct((t, 2 * vw), BF16),
        grid=(bsz, ns),
        in_specs=[pl.BlockSpec((ls, qw), rowmap(0)),
                  pl.BlockSpec((ls, qw), rowmap(1)),
                  pl.BlockSpec((ls, vw), rowmap(1)),
                  pl.BlockSpec((ls, vw), rowmap(2)),
                  pl.BlockSpec((ls, qw), rowmap(6)),
                  pl.BlockSpec((ls, qw), rowmap(7)),
                  pl.BlockSpec((ls, vw), rowmap(4)),
                  pl.BlockSpec((ls, vw), rowmap(5)),
                  pl.BlockSpec((ls, qw), rowmap(0)),
                  pl.BlockSpec((1, vw), const),
                  pl.BlockSpec((1, vw), const),
                  pl.BlockSpec((RET_HEADS, L, L), const3),
                  pl.BlockSpec((RET_HEADS, L, RET_DK), const3),
                  pl.BlockSpec((RET_HEADS, L, RET_DK), const3)],
        out_specs=pl.BlockSpec((ls, 2 * vw), lambda b, i: (b * ns + i, 0)),
        scratch_shapes=[pltpu.VMEM((RET_HEADS, RET_DK, RET_DV), F32),
                        pltpu.VMEM((GLA_HEADS, GLA_DV, GLA_DK), F32)],
        compiler_params=_cparams(("arbitrary", "arbitrary")),
        name="ret_gla_mixer",
    )(y, y, y, y, y, y, y, y, glog, ret_norm, gla_norm, decay, zeta, xi)


def _odd_proj_kernel(x1_ref, y1_ref, y2_ref, gate_ref, modp_ref, mod_ref, gain_ref, w_ref, wdt_ref, dtb_ref,
                     y_ref, dt_ref, x_ref, *, ncol, cch):
    x = _moe_residual(x1_ref[...], y1_ref, y2_ref, gate_ref[0], modp_ref[0][5:6, :])
    x_ref[...] = x
    h = _norm_mod(x, gain_ref[...], mod_ref[0], 0, 1)
    hb = h.astype(BF16)
    width = w_ref.shape[1] // ncol
    for j in range(ncol):
        c0 = j * width
        acc = _dot(hb, w_ref[:, c0:c0 + width])
        if c0 >= cch:
            acc = _silu(acc)
        y_ref[:, c0:c0 + width] = acc.astype(y_ref.dtype)
    z = _dot(hb, wdt_ref[...]) + dtb_ref[...]
    dt_ref[...] = jnp.maximum(z, 0.0) + jnp.log1p(jnp.exp(-jnp.abs(z)))


def _odd_proj(x1, y12, gates, mod_prev, mod_l, gain, w_main, w_dt, dt_bias, cch, tm, seq):
    t, d = x1.shape
    n = w_main.shape[1]
    nt = t // tm
    tiles_per_b = seq // tm
    const = lambda i: (0, 0)
    batch = lambda i: (i // tiles_per_b, 0, 0)
    return pl.pallas_call(
        functools.partial(_odd_proj_kernel, ncol=n // 512, cch=cch),
        out_shape=(jax.ShapeDtypeStruct((t, n), BF16), jax.ShapeDtypeStruct((t, LANES), F32),
                   jax.ShapeDtypeStruct((t, d), F32)),
        grid=(nt,),
        in_specs=[pl.BlockSpec((tm, d), lambda i: (i, 0)),
                  pl.BlockSpec((tm, _packed_width(d)), lambda i: (i, 0)),
                  pl.BlockSpec((tm, _packed_width(d)), lambda i: (nt + i, 0)),
                  pl.BlockSpec((1, TOP_K, tm), lambda i: (i, 0, 0)),
                  pl.BlockSpec((1, 6, d), batch),
                  pl.BlockSpec((1, 6, d), batch),
                  pl.BlockSpec((1, d), const),
                  pl.BlockSpec((d, n), const, pipeline_mode=pl.Buffered(1)),
                  pl.BlockSpec((d, LANES), const),
                  pl.BlockSpec((1, LANES), const)],
        out_specs=(pl.BlockSpec((tm, n), lambda i: (i, 0)), pl.BlockSpec((tm, LANES), lambda i: (i, 0)),
                   pl.BlockSpec((tm, d), lambda i: (i, 0))),
        compiler_params=_cparams(("arbitrary",)),
        name="odd_proj",
    )(x1, y12, y12, gates, mod_prev, mod_l, gain, w_main, w_dt, dt_bias)


def _ssd_kernel(z_ref, xbc_ref, dt_ref, cw_ref, cb_ref, alog_ref, dsk_ref, ng_ref, o_ref, hist_ref, st_ref):
    @pl.when(pl.program_id(1) == 0)
    def _():
        hist_ref[...] = jnp.zeros_like(hist_ref)
        st_ref[...] = jnp.zeros_like(st_ref)

    L = xbc_ref.shape[0]
    P, N, G = SSD_HEADDIM, SSD_STATE, SSD_GROUPS
    dinner = z_ref.shape[1]
    gw = dinner // G
    hpg = gw // P
    slab = gw + 2 * N
    hrows = hist_ref.shape[0]

    dt = dt_ref[...]
    da = dt * (-jnp.exp(alog_ref[...]))
    trow = _iota((L, L), 0)
    tcol = _iota((L, L), 1)
    causal = trow >= tcol
    cum = _dot_exact_rhs(causal.astype(BF16), da)
    cum_t = jnp.transpose(cum)
    dt_t = jnp.transpose(dt)
    ecum = jnp.exp(cum)
    wend = dt * jnp.exp(cum[L - 1:L, :] - cum)
    lane = _iota((L, LANES), 1)
    lo_half = lane < P

    def pair_cols(a, ha):
        return jnp.where(lo_half, jnp.broadcast_to(a[:, ha:ha + 1], (L, LANES)),
                         jnp.broadcast_to(a[:, ha + 1:ha + 2], (L, LANES)))

    srow = _iota((L, hrows + L), 0)
    scol = _iota((L, hrows + L), 1)
    shift = jnp.concatenate([(scol == srow + (hrows - s)).astype(BF16) for s in range(1, SSD_CONV)], axis=0)

    for g in range(G):
        c0 = g * slab
        xin = xbc_ref[:, c0:c0 + slab]
        xcat = jnp.concatenate([hist_ref[:, c0:c0 + slab], xin], axis=0)
        hist_ref[:, c0:c0 + slab] = xin[L - hrows:L]
        sh = _dot(shift, xcat)
        conv = xin.astype(F32) * cw_ref[SSD_CONV - 1:SSD_CONV, c0:c0 + slab] + cb_ref[:, c0:c0 + slab]
        for s in range(1, SSD_CONV):
            conv = conv + sh[(s - 1) * L:s * L] * cw_ref[SSD_CONV - 1 - s:SSD_CONV - s, c0:c0 + slab]
        act = _silu(conv)
        xs = act[:, :gw]
        bm_g = act[:, gw:gw + N].astype(BF16)
        cm_g = act[:, gw + N:].astype(BF16)

        cb = _dot_nt(cm_g, bm_g)
        xs_b = xs.astype(BF16)
        pieces, xw, ec = [], [], []
        for pr in range(hpg // 2):
            ha = g * hpg + 2 * pr
            ms = []
            for hh in (ha, ha + 1):
                seg = jnp.broadcast_to(cum[:, hh:hh + 1], (L, L)) - cum_t[hh:hh + 1, :]
                ms.append((cb * jnp.where(causal, jnp.exp(seg), 0.0) * dt_t[hh:hh + 1, :]).astype(BF16))
            lhs = jnp.concatenate(ms, axis=1)
            blk = xs_b[:, pr * 2 * P:(pr + 1) * 2 * P]
            zero = jnp.zeros_like(blk)
            rhs = jnp.concatenate([jnp.where(lo_half, blk, zero), jnp.where(lo_half, zero, blk)], axis=0)
            pieces.append(_dot(lhs, rhs))
            xw.append((xs[:, pr * 2 * P:(pr + 1) * 2 * P] * pair_cols(wend, ha)).astype(BF16))
            ec.append(pair_cols(ecum, ha))
        inner = jnp.concatenate(pieces, axis=1)
        ecum_g = jnp.concatenate(ec, axis=1)
        st = st_ref[g]
        cross = _dot(cm_g, st.astype(BF16)) * ecum_g
        st_ref[g] = st * ecum_g[L - 1:L, :] + _dot_tn(bm_g, jnp.concatenate(xw, axis=1))
        y = inner + cross + dsk_ref[:, g * gw:(g + 1) * gw] * xs
        y = y * z_ref[:, g * gw:(g + 1) * gw].astype(F32)
        o_ref[:, g * gw:(g + 1) * gw] = (_rms(y) * ng_ref[:, g * gw:(g + 1) * gw]).astype(o_ref.dtype)


SSD_HIST_ROWS = 16


def _ssd(y, dt, conv_w, conv_b, a_log_row, d_row, norm_g, bsz, seq):
    t = y.shape[0]
    L = SSD_CHUNK
    ns = seq // L
    dinner = norm_g.shape[1]
    cch = conv_w.shape[1]
    gw = dinner // SSD_GROUPS
    const = lambda b, i: (0, 0)
    assert cch % dinner == 0
    return pl.pallas_call(
        _ssd_kernel,
        out_shape=jax.ShapeDtypeStruct((t, dinner), BF16),
        grid=(bsz, ns),
        in_specs=[pl.BlockSpec((L, dinner), lambda b, i: (b * ns + i, cch // dinner)),
                  pl.BlockSpec((L, cch), lambda b, i: (b * ns + i, 0)),
                  pl.BlockSpec((L, LANES), lambda b, i: (b * ns + i, 0)),
                  pl.BlockSpec((SSD_CONV, cch), const),
                  pl.BlockSpec((1, cch), const),
                  pl.BlockSpec((1, LANES), const),
                  pl.BlockSpec((1, dinner), const),
                  pl.BlockSpec((1, dinner), const)],
        out_specs=pl.BlockSpec((L, dinner), lambda b, i: (b * ns + i, 0)),
        scratch_shapes=[pltpu.VMEM((SSD_HIST_ROWS, cch), BF16),
                        pltpu.VMEM((SSD_GROUPS, SSD_STATE, gw), F32)],
        compiler_params=_cparams(("arbitrary", "arbitrary")),
        name="ssd_mixer",
    )(y, y, dt, conv_w, conv_b, a_log_row, d_row, norm_g)


def _outproj_kernel(o_ref, x_ref, mod_ref, gain_ref, w_ref, x1_ref, h2_ref):
    mod = mod_ref[0]
    x1 = x_ref[...] + mod[2:3, :] * _dot(o_ref[...], w_ref[...])
    x1_ref[...] = x1
    h2_ref[...] = _pack_pair(_norm_mod(x1, gain_ref[...], mod, 3, 4))


def _outproj(o, x2d, mod_l, gain2, w_out, tm, seq):
    t, d = x2d.shape
    kin = o.shape[1]
    tiles_per_b = seq // tm
    const = lambda i: (0, 0)
    return pl.pallas_call(
        _outproj_kernel,
        out_shape=(jax.ShapeDtypeStruct((t, d), F32), jax.ShapeDtypeStruct((t, _packed_width(d)), PACKED)),
        grid=(t // tm,),
        in_specs=[pl.BlockSpec((tm, kin), lambda i: (i, 0)),
                  pl.BlockSpec((tm, d), lambda i: (i, 0)),
                  pl.BlockSpec((1, 6, d), lambda i: (i // tiles_per_b, 0, 0)),
                  pl.BlockSpec((1, d), const),
                  pl.BlockSpec((kin, d), const)],
        out_specs=(pl.BlockSpec((tm, d), lambda i: (i, 0)), pl.BlockSpec((tm, _packed_width(d)), lambda i: (i, 0))),
        compiler_params=_cparams(("arbitrary",)),
        name="out_proj",
    )(o, x2d, mod_l, gain2, w_out)


def _route_kernel(h_ref, rw_ref, rb_ref, dest_ref, gate_ref, blk_ref, cnt_ref, pst_ref, sel_ref, *, nbpad):
    phase = pl.program_id(0)
    step = pl.program_id(1)
    tm = h_ref.shape[0]
    ne = rw_ref.shape[0]
    epg = ne // N_GROUPS
    ei = _iota((ne, tm), 0).astype(F32)

    @pl.when(jnp.logical_and(phase == 0, step == 0))
    def _():
        cnt_ref[...] = jnp.zeros_like(cnt_ref)

    @pl.when(phase == 0)
    def _():
        logits = _dot_nt(rw_ref[...], _unpack_pair(h_ref[...]).astype(BF16))
        s = jax.nn.sigmoid(logits)
        sb = s + jnp.concatenate([rb_ref[...]] * (tm // LANES), axis=1)
        G = N_GROUPS
        v = [sb[j * G:(j + 1) * G, :] for j in range(epg)]
        m01, n01 = jnp.maximum(v[0], v[1]), jnp.minimum(v[0], v[1])
        m23, n23 = jnp.maximum(v[2], v[3]), jnp.minimum(v[2], v[3])
        top1 = jnp.maximum(m01, m23)
        top2 = jnp.maximum(jnp.minimum(m01, m23), jnp.maximum(n01, n23))
        gscore = top1 + top2
        gi = _iota((G, tm), 0)
        gmax = jnp.max(gscore, axis=0, keepdims=True)
        gidx = jnp.min(jnp.where(gscore == gmax, gi, G), axis=0, keepdims=True)
        sel = gi == gidx
        vb = [jnp.sum(jnp.where(sel, v[j], 0.0), axis=0, keepdims=True) for j in range(epg)]
        vs = [jnp.sum(jnp.where(sel, s[j * G:(j + 1) * G, :], 0.0), axis=0, keepdims=True) for j in range(epg)]
        zero = jnp.zeros_like(vb[0])
        e1 = zero
        e2 = zero
        w1 = zero
        w2 = zero
        for j in range(epg):
            rank = zero
            for i in range(epg):
                if i == j:
                    continue
                ahead = (vb[i] >= vb[j]) if i < j else (vb[i] > vb[j])
                rank = rank + jnp.where(ahead, 1.0, 0.0)
            first = rank == 0.0
            second = rank == 1.0
            e1 = e1 + jnp.where(first, float(j), 0.0)
            e2 = e2 + jnp.where(second, float(j), 0.0)
            w1 = w1 + jnp.where(first, vs[j], 0.0)
            w2 = w2 + jnp.where(second, vs[j], 0.0)
        gf = gidx.astype(F32) * float(epg)
        e1 = e1 + gf
        e2 = e2 + gf
        wsum = w1 + w2
        sel_ref[step] = jnp.concatenate([e1, e2, w1 / wsum, w2 / wsum], axis=0)
        onehot = jnp.where((ei == e1) | (ei == e2), 1.0, 0.0)
        cnt_ref[...] = cnt_ref[...] + jnp.sum(onehot, axis=1, keepdims=True)

    @pl.when(jnp.logical_and(phase == 1, step == 0))
    def _():
        nblk = jnp.floor((cnt_ref[...] + (MOE_BLOCK - 1.0)) * (1.0 / MOE_BLOCK))
        hi = jnp.floor(nblk * (1.0 / 16.0))
        lo = nblk - hi * 16.0
        er = _iota((ne, ne), 0)
        ec = _iota((ne, ne), 1)
        lower = (ec < er).astype(BF16)
        pst = _dot(lower, hi.astype(BF16)) * 16.0 + _dot(lower, lo.astype(BF16))
        pst_ref[...] = pst
        pend = jnp.concatenate([pst + nblk] * (nbpad // LANES), axis=1)
        bidx = _iota((ne, nbpad), 1).astype(F32)
        be = jnp.sum(jnp.where(pend <= bidx, 1.0, 0.0), axis=0, keepdims=True)
        total = jnp.max(pend, axis=0, keepdims=True)
        blk_ref[0:1, :] = jnp.minimum(be, ne - 1.0).astype(I32)
        blk_ref[1:2, :] = total.astype(I32)
        cnt_ref[...] = jnp.zeros_like(cnt_ref)

    @pl.when(phase == 1)
    def _():
        choice = sel_ref[step]
        oh1 = ei == choice[0:1, :]
        oh2 = ei == choice[1:2, :]
        onehot = jnp.where(oh1 | oh2, 1.0, 0.0)
        upper = (_iota((tm, tm), 0) < _iota((tm, tm), 1)).astype(BF16)
        prefix = _dot(onehot.astype(BF16), upper)
        cnt = cnt_ref[...]
        base = jnp.concatenate([cnt] * (tm // LANES), axis=1) + prefix
        pst = jnp.concatenate([pst_ref[...]] * (tm // LANES), axis=1) * float(MOE_BLOCK)
        slot = base + pst
        d1 = jnp.sum(jnp.where(oh1, slot, 0.0), axis=0, keepdims=True)
        d2 = jnp.sum(jnp.where(oh2, slot, 0.0), axis=0, keepdims=True)
        cnt_ref[...] = cnt + jnp.sum(onehot, axis=1, keepdims=True)
        dest_ref[0, 0:1, :] = d1.astype(I32)
        dest_ref[0, 1:2, :] = d2.astype(I32)
        gate_ref[0] = choice[2:4, :]


def _route(h2, rw_t, rb_b, tm, nbpad):
    t, dp = h2.shape
    ne, d = rw_t.shape
    nt = t // tm
    return pl.pallas_call(
        functools.partial(_route_kernel, nbpad=nbpad),
        out_shape=(jax.ShapeDtypeStruct((nt, 2, tm), I32), jax.ShapeDtypeStruct((nt, 2, tm), F32),
                   jax.ShapeDtypeStruct((2, nbpad), I32)),
        grid=(2, nt),
        in_specs=[pl.BlockSpec((tm, dp), lambda p, i: (i * (1 - p) + (nt - 1) * p, 0)),
                  pl.BlockSpec((ne, d), lambda p, i: (0, 0)),
                  pl.BlockSpec((ne, LANES), lambda p, i: (0, 0))],
        out_specs=(pl.BlockSpec((1, 2, tm), lambda p, i: (i * p, 0, 0)),
                   pl.BlockSpec((1, 2, tm), lambda p, i: (i * p, 0, 0)),
                   pl.BlockSpec((2, nbpad), lambda p, i: (0, 0))),
        scratch_shapes=[pltpu.VMEM((ne, LANES), F32), pltpu.VMEM((ne, LANES), F32),
                        pltpu.VMEM((nt, 4, tm), F32)],
        compiler_params=_cparams(("arbitrary", "arbitrary")),
        name="router",
    )(h2, rw_t, rb_b)


SC_CORES, SC_SUBCORES = 2, 16
SC_ROWS = 64


def _sc_mesh():
    return plsc.VectorSubcoreMesh(core_axis_name="c", subcore_axis_name="s",
                                  num_cores=SC_CORES, num_subcores=SC_SUBCORES)


def _sc_worker_base(rows_per_worker):
    return (lax.axis_index("s") * SC_CORES + lax.axis_index("c")) * rows_per_worker


def _sc_scatter_rows(src, idx, n_out):
    t, d = src.shape
    assert idx.shape[0] == TOP_K * t
    per_w = t // (SC_CORES * SC_SUBCORES)
    assert per_w % (2 * SC_ROWS) == 0
    rows = pltpu.VMEM((SC_ROWS, d), src.dtype)
    ids = pltpu.VMEM((SC_ROWS,), I32)

    @functools.partial(
        pl.kernel, mesh=_sc_mesh(), out_type=jax.ShapeDtypeStruct((n_out, d), src.dtype),
        scratch_types=[rows, rows, ids, ids, ids, ids] + [pltpu.SemaphoreType.DMA] * 4,
        compiler_params=pltpu.CompilerParams(use_tc_tiling_on_sc=True), name="moe_dispatch_sc")
    def k(src_hbm, idx_hbm, out_hbm, rows_a, rows_b, ia0, ia1, ib0, ib1, s0, s1, s2, s3):
        base = _sc_worker_base(per_w)

        @pl.loop(0, per_w // (2 * SC_ROWS))
        def _(j):
            off_a = pl.multiple_of(base + 2 * j * SC_ROWS, SC_ROWS)
            off_b = pl.multiple_of(off_a + SC_ROWS, SC_ROWS)
            load_a = pltpu.async_copy(src_hbm.at[pl.ds(off_a, SC_ROWS)], rows_a, s0)
            load_b = pltpu.async_copy(src_hbm.at[pl.ds(off_b, SC_ROWS)], rows_b, s1)
            pltpu.sync_copy(idx_hbm.at[pl.ds(off_a, SC_ROWS)], ia0)
            pltpu.sync_copy(idx_hbm.at[pl.ds(t + off_a, SC_ROWS)], ia1)
            pltpu.sync_copy(idx_hbm.at[pl.ds(off_b, SC_ROWS)], ib0)
            pltpu.sync_copy(idx_hbm.at[pl.ds(t + off_b, SC_ROWS)], ib1)
            load_a.wait()
            put_a0 = pltpu.async_copy(rows_a, out_hbm.at[ia0], s0)
            put_a1 = pltpu.async_copy(rows_a, out_hbm.at[ia1], s2)
            load_b.wait()
            put_b0 = pltpu.async_copy(rows_b, out_hbm.at[ib0], s1)
            put_b1 = pltpu.async_copy(rows_b, out_hbm.at[ib1], s3)
            put_a0.wait()
            put_a1.wait()
            put_b0.wait()
            put_b1.wait()

    return k(src, idx)


def _sc_gather_rows(table, idx):
    _, d = table.shape
    b = idx.shape[0]
    per_w = b // (SC_CORES * SC_SUBCORES)
    assert per_w % (2 * SC_ROWS) == 0
    rows = pltpu.VMEM((SC_ROWS, d), table.dtype)
    ids = pltpu.VMEM((SC_ROWS,), I32)

    @functools.partial(
        pl.kernel, mesh=_sc_mesh(), out_type=jax.ShapeDtypeStruct((b, d), table.dtype),
        scratch_types=[rows, rows, ids, ids] + [pltpu.SemaphoreType.DMA] * 2,
        compiler_params=pltpu.CompilerParams(use_tc_tiling_on_sc=True), name="moe_gather_sc")
    def k(table_hbm, idx_hbm, out_hbm, rows_a, rows_b, ia, ib, s0, s1):
        base = _sc_worker_base(per_w)

        @pl.loop(0, per_w // (2 * SC_ROWS))
        def _(j):
            off_a = pl.multiple_of(base + 2 * j * SC_ROWS, SC_ROWS)
            off_b = pl.multiple_of(off_a + SC_ROWS, SC_ROWS)
            pltpu.sync_copy(idx_hbm.at[pl.ds(off_a, SC_ROWS)], ia)
            pltpu.sync_copy(idx_hbm.at[pl.ds(off_b, SC_ROWS)], ib)
            get_a = pltpu.async_copy(table_hbm.at[ia], rows_a, s0)
            get_b = pltpu.async_copy(table_hbm.at[ib], rows_b, s1)
            get_a.wait()
            put_a = pltpu.async_copy(rows_a, out_hbm.at[pl.ds(off_a, SC_ROWS)], s0)
            get_b.wait()
            put_b = pltpu.async_copy(rows_b, out_hbm.at[pl.ds(off_b, SC_ROWS)], s1)
            put_a.wait()
            put_b.wait()

    return k(table, idx)


def _expert_kernel(blk_ref, nused_ref, xs_ref, wg_ref, wu_ref, wd_ref, y_ref, wgb, wub, wdb):
    b = pl.program_id(0)
    prev = blk_ref[jnp.maximum(b - 1, 0)]
    changed = jnp.logical_or(b == 0, blk_ref[b] != prev)

    @pl.when(changed)
    def _():
        wgb[...] = wg_ref[0, 0].astype(BF16)
        wub[...] = wu_ref[0, 0].astype(BF16)
        wdb[...] = wd_ref[0, 0].astype(BF16)

    @pl.when(b < nused_ref[0])
    def _():
        xb = _unpack_pair(xs_ref[...]).astype(BF16)
        g = _dot(xb, wgb[...])
        u = _dot(xb, wub[...])
        a = (_silu(g) * u).astype(BF16)
        y_ref[...] = _pack_pair(_dot(a, wdb[...]))

    @pl.when(b >= nused_ref[0])
    def _():
        y_ref[...] = jnp.zeros_like(y_ref)


def _experts(xs, blk_e, nused, w_gate, w_up, w_down, layer):
    np_rows, dp = xs.shape
    _, ne, d, ff = w_gate.shape
    nb = np_rows // MOE_BLOCK
    grid_spec = pltpu.PrefetchScalarGridSpec(
        num_scalar_prefetch=2,
        grid=(nb,),
        in_specs=[pl.BlockSpec((MOE_BLOCK, dp), lambda b, be, nu: (b, 0)),
                  pl.BlockSpec((1, 1, d, ff), lambda b, be, nu: (layer, be[b], 0, 0)),
                  pl.BlockSpec((1, 1, d, ff), lambda b, be, nu: (layer, be[b], 0, 0)),
                  pl.BlockSpec((1, 1, ff, d), lambda b, be, nu: (layer, be[b], 0, 0))],
        out_specs=pl.BlockSpec((MOE_BLOCK, dp), lambda b, be, nu: (b, 0)),
        scratch_shapes=[pltpu.VMEM((d, ff), BF16), pltpu.VMEM((d, ff), BF16), pltpu.VMEM((ff, d), BF16)],
    )
    return pl.pallas_call(
        _expert_kernel,
        out_shape=jax.ShapeDtypeStruct((np_rows, dp), PACKED),
        grid_spec=grid_spec,
        compiler_params=_cparams(("arbitrary",)),
        name="moe_experts",
    )(blk_e, nused, xs, w_gate, w_up, w_down)


def _moe_residual(x1, y1_ref, y2_ref, gates, out_gate):
    tc, d = x1.shape
    acc = jnp.zeros((tc, d), F32)
    for k, y_ref in enumerate((y1_ref, y2_ref)):
        gcol = jnp.transpose(jnp.broadcast_to(gates[k:k + 1, :], (LANES, tc)))
        acc = acc + _unpack_pair(y_ref[...]) * jnp.concatenate([gcol] * (d // LANES), axis=1)
    return x1 + out_gate * acc


def _combine_kernel(x1_ref, y1_ref, y2_ref, gate_ref, mod_ref, fg_ref, o_ref, *, final):
    x2 = _moe_residual(x1_ref[...], y1_ref, y2_ref, gate_ref[0], mod_ref[0][5:6, :])
    if final:
        x2 = _rms(x2) * fg_ref[...]
    o_ref[...] = x2


def _combine(x1, y12, gates, mod_l, final_gain, tc, seq, final):
    t, d = x1.shape
    nt = t // tc
    tiles_per_b = seq // tc
    return pl.pallas_call(
        functools.partial(_combine_kernel, final=final),
        out_shape=jax.ShapeDtypeStruct((t, d), F32),
        grid=(nt,),
        in_specs=[pl.BlockSpec((tc, d), lambda i: (i, 0)),
                  pl.BlockSpec((tc, _packed_width(d)), lambda i: (i, 0)),
                  pl.BlockSpec((tc, _packed_width(d)), lambda i: (nt + i, 0)),
                  pl.BlockSpec((1, TOP_K, tc), lambda i: (i, 0, 0)),
                  pl.BlockSpec((1, 6, d), lambda i: (i // tiles_per_b, 0, 0)),
                  pl.BlockSpec((1, d), lambda i: (0, 0))],
        out_specs=pl.BlockSpec((tc, d), lambda i: (i, 0)),
        compiler_params=_cparams(("arbitrary",)),
        name="moe_combine",
    )(x1, y12, y12, gates, mod_l, final_gain)


ROUTE_TILE = 512


def _moe(x1, h2, rw_t, rb_b, w_gate, w_up, w_down, layer):
    t, d = x1.shape
    ne = w_gate.shape[1]
    nb = (t * TOP_K + ne * (MOE_BLOCK - 1) + MOE_BLOCK - 1) // MOE_BLOCK
    nbpad = -(-nb // LANES) * LANES
    tm = ROUTE_TILE
    dest, gates, blk = _route(h2, rw_t, rb_b, tm, nbpad)
    slot = jnp.transpose(dest, (1, 0, 2)).reshape(TOP_K * t)
    xs = _sc_scatter_rows(h2, slot, nb * MOE_BLOCK)
    ys = _experts(xs, blk[0, :nb], blk[1, :1], w_gate, w_up, w_down, layer)
    return _sc_gather_rows(ys, slot), gates


def kernel(x, c, positions, mod_w, mod_b, norm_mix, norm_ffn, ev_w_in, ev_ret_norm, ev_gla_wa2, ev_gla_ba,
           ev_gla_norm, ev_w_out, od_w_in, od_conv_w, od_conv_b, od_dt_bias, od_a_log, od_d, od_norm, od_w_out,
           router_w, router_b, exp_w_gate, exp_w_up, exp_w_down, final_norm):
    bsz, seq, d = x.shape
    depth = mod_w.shape[0]
    t = bsz * seq
    ne = router_w.shape[1]
    epg = ne // N_GROUPS
    mod = _modulation(c, mod_w, mod_b)
    x2d = x.reshape(t, d)

    perm = np.array([g * epg + j for j in range(epg) for g in range(N_GROUPS)])
    rw_t = router_w.T[perm].astype(BF16)
    rb_b = jnp.broadcast_to(router_b.astype(F32)[perm][:, None], (ne, LANES))
    final_gain = final_norm.reshape(1, d)

    half = RET_DK // 2
    inv = ROPE_BASE ** (-jnp.arange(half, dtype=F32) / half)
    inv2 = jnp.concatenate([inv, inv]).reshape(1, LANES)
    pos_col = positions.reshape(t, 1)

    pending = None
    for layer in range(depth):
        i = layer // 2
        mod_l = mod[layer]
        gain1 = norm_mix[layer].reshape(1, d)
        gain2 = norm_ffn[layer].reshape(1, d)
        if layer % 2 == 0:
            if pending is not None:
                x2d = _combine(*pending, final_gain, ROUTE_TILE, seq, final=False)
            w_in = ev_w_in[i]
            nmain = w_in.shape[1] - GLA_RANK
            w_main = w_in[:, :nmain].astype(BF16)
            w_ga = jnp.pad(w_in[:, nmain:], ((0, 0), (0, LANES - GLA_RANK))).astype(BF16)
            wa2 = jnp.pad(ev_gla_wa2[i], ((0, LANES - GLA_RANK), (0, 0))).astype(BF16)
            y, glog = _even_proj(x2d, mod_l, gain1, pos_col, inv2, w_main, w_ga, wa2,
                                 ev_gla_ba[i].reshape(1, -1), PROJ_TILE, seq)
            o = _ret_gla(y, glog, ev_ret_norm[i].reshape(1, -1), ev_gla_norm[i].reshape(1, -1), bsz, seq, 512)
            w_out = ev_w_out[i].astype(BF16)
        else:
            assert pending is not None
            w_in = od_w_in[i]
            nheads = od_a_log.shape[1]
            nmain = w_in.shape[1] - nheads
            dinner = od_norm.shape[1]
            gw, ns_ = dinner // SSD_GROUPS, SSD_STATE
            bc0 = dinner + SSD_GROUPS * ns_
            cperm = np.concatenate([np.concatenate([np.arange(g * gw, (g + 1) * gw),
                                                    dinner + np.arange(g * ns_, (g + 1) * ns_),
                                                    bc0 + np.arange(g * ns_, (g + 1) * ns_)])
                                    for g in range(SSD_GROUPS)])
            conv_w = od_conv_w[i][:, cperm]
            conv_b = od_conv_b[i][cperm].reshape(1, -1)
            w_main = jnp.concatenate([w_in[:, dinner + cperm], w_in[:, :dinner]], axis=1).astype(BF16)
            w_dt = jnp.pad(w_in[:, nmain:], ((0, 0), (0, LANES - nheads))).astype(BF16)
            dtb = jnp.pad(od_dt_bias[i], (0, LANES - nheads)).reshape(1, LANES)
            y, dt, x2d = _odd_proj(*pending, mod_l, gain1, w_main, w_dt, dtb, od_conv_w.shape[2], ROUTE_TILE, seq)
            alog = jnp.pad(od_a_log[i], (0, LANES - nheads)).reshape(1, LANES)
            d_row = jnp.repeat(od_d[i], SSD_HEADDIM).reshape(1, dinner)
            o = _ssd(y, dt, conv_w, conv_b, alog, d_row, od_norm[i].reshape(1, dinner), bsz, seq)
            w_out = od_w_out[i].astype(BF16)
        x1, h2 = _outproj(o, x2d, mod_l, gain2, w_out, 512, seq)
        y12, gates = _moe(x1, h2, rw_t, rb_b, exp_w_gate, exp_w_up, exp_w_down, layer)
        pending = (x1, y12, gates, mod_l)
    out = _combine(*pending, final_gain, ROUTE_TILE, seq, final=True)
    return out.reshape(bsz, seq, d)
```

```python
import functools

import jax
import jax.numpy as jnp
import numpy as np
from jax import lax
from jax.experimental import pallas as pl
from jax.experimental.pallas import tpu as pltpu
from jax.experimental.pallas import tpu_sc as plsc

F32 = jnp.float32
BF16 = jnp.bfloat16
I32 = jnp.int32

RET_HEADS, RET_DK, RET_DV, RET_CHUNK = 4, 128, 256, 128
ROPE_BASE = 10000.0
GLA_HEADS, GLA_DK, GLA_DV, GLA_RANK, GLA_TAU, GLA_CHUNK = 4, 128, 256, 16, 16.0, 64
SSD_HEADDIM, SSD_GROUPS, SSD_STATE, SSD_CONV, SSD_CHUNK = 64, 8, 128, 4, 128
N_GROUPS, TOP_K, GROUP_SCORE_K, MOE_BLOCK = 8, 2, 2, 256
NORM_EPS = 1e-6

PROJ_TILE = 512
LANES = 128
VMEM_LIMIT = 56 * 1024 * 1024


def _cparams(sem, vmem=VMEM_LIMIT):
    return pltpu.CompilerParams(dimension_semantics=sem, vmem_limit_bytes=vmem)


def _dot(a, b):
    return jnp.dot(a, b, preferred_element_type=F32)


def _dot_nt(a, b):
    return lax.dot_general(a, b, (((1,), (1,)), ((), ())), preferred_element_type=F32)


def _dot_tn(a, b):
    return lax.dot_general(a, b, (((0,), (0,)), ((), ())), preferred_element_type=F32)


def _split3(x):
    a = x.astype(BF16)
    r = x - a.astype(F32)
    b = r.astype(BF16)
    c = (r - b.astype(F32)).astype(BF16)
    return a, b, c


def _dot_exact_rhs(m01, x):
    a, b, c = _split3(x)
    return _dot(m01, a) + _dot(m01, b) + _dot(m01, c)


def _dot_exact_lhs(x, m01):
    a, b, c = _split3(x)
    return _dot(a, m01) + _dot(b, m01) + _dot(c, m01)


def _dot_tn_exact(x, m01):
    a, b, c = _split3(x)
    return _dot_tn(a, m01) + _dot_tn(b, m01) + _dot_tn(c, m01)


def _silu(x):
    return x * jax.nn.sigmoid(x)


def _rms(x, eps=NORM_EPS):
    return x * lax.rsqrt(jnp.mean(x * x, axis=-1, keepdims=True) + eps)


def _iota(shape, dim):
    return lax.broadcasted_iota(I32, shape, dim)


def _interleave(items, starts_per_round):
    waiting, running = list(items), []
    while waiting or running:
        running += [waiting.pop(0) for _ in range(min(starts_per_round, len(waiting)))]
        for item in reversed(list(running)):
            if next(item, StopIteration) is StopIteration:
                running.remove(item)


PACKED = jnp.uint32
_HI16 = 0xFFFF0000


def _packed_width(n):
    return n // 2


def _pack_pair(x):
    n = x.shape[1] // 2
    lo = lax.bitcast_convert_type(x[:, :n].astype(BF16).astype(F32), jnp.uint32)
    hi = lax.bitcast_convert_type(x[:, n:].astype(BF16).astype(F32), jnp.uint32)
    return (hi & jnp.uint32(_HI16)) | (lo >> 16)


def _unpack_pair(w):
    lo = lax.bitcast_convert_type(w << 16, F32)
    hi = lax.bitcast_convert_type(w & jnp.uint32(_HI16), F32)
    return jnp.concatenate([lo, hi], axis=1)


def _mod_kernel(c_ref, w_ref, b_ref, o_ref):
    sc = _silu(c_ref[...])
    o_ref[0] = _dot(sc.astype(BF16), w_ref[0].astype(BF16)) + b_ref[0]


def _modulation(c, mod_w, mod_b):
    depth, d, d6 = mod_w.shape
    bsz = c.shape[0]
    nb = d6 // d
    out = pl.pallas_call(
        _mod_kernel,
        out_shape=jax.ShapeDtypeStruct((depth, bsz, d6), F32),
        grid=(depth, nb),
        in_specs=[pl.BlockSpec((bsz, d), lambda l, j: (0, 0)),
                  pl.BlockSpec((1, d, d), lambda l, j: (l, 0, j)),
                  pl.BlockSpec((1, 1, d), lambda l, j: (l, 0, j))],
        out_specs=pl.BlockSpec((1, bsz, d), lambda l, j: (l, 0, j)),
        compiler_params=_cparams(("arbitrary", "arbitrary")),
        name="adaln_mod",
    )(c, mod_w, mod_b.reshape(depth, 1, d6))
    return out.reshape(depth, bsz, nb, d)


def _norm_mod(x, gain, mod, shift_row, scale_row):
    y = _rms(x) * gain
    return y * (1.0 + mod[scale_row:scale_row + 1, :]) + mod[shift_row:shift_row + 1, :]


def _even_proj_kernel(x_ref, mod_ref, gain_ref, pos_ref, inv_ref, w_ref, wga_ref, wa2_ref, ba_ref,
                      y_ref, glog_ref, *, ncol):
    h = _norm_mod(x_ref[...], gain_ref[...], mod_ref[0], 0, 1)
    hb = h.astype(BF16)
    tm = hb.shape[0]
    ang = pos_ref[...].astype(F32) * inv_ref[...]
    cos = jnp.cos(ang)
    lane = _iota((tm, LANES), 1)
    sin = jnp.where(lane < LANES // 2, -jnp.sin(ang), jnp.sin(ang))
    qk_scale = RET_DK ** -0.5
    rqk = RET_HEADS * RET_DK
    rv = RET_HEADS * RET_DV
    gq0 = 2 * rqk + 2 * rv
    width = w_ref.shape[1] // ncol
    for j in range(ncol):
        c0 = j * width
        acc = _dot(hb, w_ref[:, c0:c0 + width])
        for s in range(width // LANES):
            col = c0 + s * LANES
            blk = acc[:, s * LANES:(s + 1) * LANES]
            if col < 2 * rqk:
                blk = blk * cos + pltpu.roll(blk, LANES // 2, 1) * sin
                if col >= rqk:
                    blk = blk * qk_scale
            elif gq0 <= col < gq0 + GLA_HEADS * GLA_DK:
                blk = blk * (GLA_DK ** -0.5)
            y_ref[:, col:col + LANES] = blk.astype(y_ref.dtype)
    ga = _dot(hb, wga_ref[...])
    z = _dot(ga.astype(BF16), wa2_ref[...]) + ba_ref[...]
    logsig = jnp.minimum(z, 0.0) - jnp.log1p(jnp.exp(-jnp.abs(z)))
    glog_ref[...] = logsig * (1.0 / GLA_TAU)


def _even_proj(x2d, mod_l, gain, pos_col, inv2, w_main, w_ga, wa2, ba, tm, seq):
    t, d = x2d.shape
    n = w_main.shape[1]
    tiles_per_b = seq // tm
    gk = wa2.shape[1]
    const = lambda i: (0, 0)
    return pl.pallas_call(
        functools.partial(_even_proj_kernel, ncol=n // 512),
        out_shape=(jax.ShapeDtypeStruct((t, n), BF16), jax.ShapeDtypeStruct((t, gk), F32)),
        grid=(t // tm,),
        in_specs=[pl.BlockSpec((tm, d), lambda i: (i, 0)),
                  pl.BlockSpec((1, 6, d), lambda i: (i // tiles_per_b, 0, 0)),
                  pl.BlockSpec((1, d), const),
                  pl.BlockSpec((tm, 1), lambda i: (i, 0)),
                  pl.BlockSpec((1, LANES), const),
                  pl.BlockSpec((d, n), const, pipeline_mode=pl.Buffered(1)),
                  pl.BlockSpec((d, LANES), const),
                  pl.BlockSpec((LANES, gk), const),
                  pl.BlockSpec((1, gk), const)],
        out_specs=(pl.BlockSpec((tm, n), lambda i: (i, 0)), pl.BlockSpec((tm, gk), lambda i: (i, 0))),
        compiler_params=_cparams(("arbitrary",)),
        name="even_proj",
    )(x2d, mod_l, gain, pos_col, inv2, w_main, w_ga, wa2, ba)


RET_GLA_UNROLL = 4
RET_GLA_STARTS = 4


def _ret_gla_kernel(rq_ref, rk_ref, rv_ref, rg_ref, gq_ref, gk_ref, gv_ref, gg_ref, gl_ref,
                    rn_ref, gn_ref, dec_ref, zeta_ref, xi_ref, o_ref, rst_ref, gst_ref, *, chunk_decay):
    @pl.when(pl.program_id(1) == 0)
    def _():
        rst_ref[...] = jnp.zeros_like(rst_ref)
        gst_ref[...] = jnp.zeros_like(gst_ref)

    ls = rq_ref.shape[0]
    lr, lg = RET_CHUNK, GLA_CHUNK
    trow = _iota((lg, lg), 0)
    tcol = _iota((lg, lg), 1)
    causal = trow >= tcol
    tri = causal.astype(BF16)
    base = RET_HEADS * RET_DV

    def ret_item(r0, h):
        q = rq_ref[pl.ds(r0, lr), h * RET_DK:(h + 1) * RET_DK]
        k = rk_ref[pl.ds(r0, lr), h * RET_DK:(h + 1) * RET_DK]
        v = rv_ref[pl.ds(r0, lr), h * RET_DV:(h + 1) * RET_DV]
        s = _dot_nt(q, k) * dec_ref[h]
        qx = (q.astype(F32) * xi_ref[h]).astype(BF16)
        kz = (k.astype(F32) * zeta_ref[h]).astype(BF16)
        yield
        inner = _dot(s.astype(BF16), v)
        kv = _dot_tn(kz, v)
        yield
        gate = _silu(rg_ref[pl.ds(r0, lr), h * RET_DV:(h + 1) * RET_DV].astype(F32))
        yield
        st = rst_ref[h]
        cross = _dot(qx, st.astype(BF16))
        rst_ref[h] = st * chunk_decay[h] + kv
        o = _rms(inner + cross) * rn_ref[:, h * RET_DV:(h + 1) * RET_DV] * gate
        o_ref[pl.ds(r0, lr), h * RET_DV:(h + 1) * RET_DV] = o.astype(o_ref.dtype)

    def gla_item(r0, h):
        gl = gl_ref[pl.ds(r0, lg), h * GLA_DK:(h + 1) * GLA_DK]
        b = _dot_exact_rhs(tri, gl)
        yield
        q = gq_ref[pl.ds(r0, lg), h * GLA_DK:(h + 1) * GLA_DK].astype(F32)
        k = gk_ref[pl.ds(r0, lg), h * GLA_DK:(h + 1) * GLA_DK].astype(F32)
        bref = b[lg // 2:lg // 2 + 1, :]
        blast = b[lg - 1:lg, :]
        att = _dot_nt((q * jnp.exp(b - bref)).astype(BF16), (k * jnp.exp(bref - b)).astype(BF16))
        qb = (q * jnp.exp(b)).astype(BF16)
        kd = (k * jnp.exp(blast - b)).astype(BF16)
        dec = jnp.exp(blast)
        yield
        v = gv_ref[pl.ds(r0, lg), h * GLA_DV:(h + 1) * GLA_DV]
        inner = _dot(jnp.where(causal, att, 0.0).astype(BF16), v)
        kvt = _dot_tn(v, kd)
        gate = _silu(gg_ref[pl.ds(r0, lg), h * GLA_DV:(h + 1) * GLA_DV].astype(F32))
        yield
        st = gst_ref[h]
        cross = _dot_nt(qb, st.astype(BF16))
        gst_ref[h] = st * dec + kvt
        o = _rms(inner + cross) * gn_ref[:, h * GLA_DV:(h + 1) * GLA_DV] * gate
        o_ref[pl.ds(r0, lg), base + h * GLA_DV:base + (h + 1) * GLA_DV] = o.astype(o_ref.dtype)

    def step(c, carry):
        waiting = []
        for u in range(RET_GLA_UNROLL):
            r0 = pl.multiple_of((c * RET_GLA_UNROLL + u) * lr, lr)
            waiting += [ret_item(r0, h) for h in range(RET_HEADS)]
            for j in range(lr // lg):
                rj = pl.multiple_of(r0 + j * lg, lg)
                waiting += [gla_item(rj, h) for h in range(GLA_HEADS)]
        assert RET_HEADS % RET_GLA_STARTS == 0 and GLA_HEADS % RET_GLA_STARTS == 0
        _interleave(waiting, RET_GLA_STARTS)
        return carry

    lax.fori_loop(0, ls // (lr * RET_GLA_UNROLL), step, 0)


def _ret_gla(y, glog, ret_norm, gla_norm, bsz, seq, ls):
    t = y.shape[0]
    ns = seq // ls
    qw, vw = RET_HEADS * RET_DK, RET_HEADS * RET_DV
    L = RET_CHUNK
    log_gamma = jnp.log1p(-jnp.exp2(-5.0 - jnp.arange(RET_HEADS, dtype=F32)))
    idx = jnp.arange(L, dtype=F32)
    diff = idx[:, None] - idx[None, :]
    decay = jnp.where(diff >= 0, jnp.exp(log_gamma[:, None, None] * jnp.maximum(diff, 0.0)), 0.0)
    zeta = jnp.broadcast_to(jnp.exp(log_gamma[:, None] * (L - 1 - idx)[None, :])[:, :, None], (RET_HEADS, L, RET_DK))
    xi = jnp.broadcast_to(jnp.exp(log_gamma[:, None] * (idx + 1.0)[None, :])[:, :, None], (RET_HEADS, L, RET_DK))
    chunk_decay = tuple(float(np.exp(np.float32(np.log1p(-np.exp2(np.float32(-5.0 - i)))) * np.float32(L)))
                        for i in range(RET_HEADS))
    rowmap = lambda j: (lambda b, i: (b * ns + i, j))
    const = lambda b, i: (0, 0)
    const3 = lambda b, i: (0, 0, 0)
    return pl.pallas_call(
        functools.partial(_ret_gla_kernel, chunk_decay=chunk_decay),
        out_shape=jax.ShapeDtypeStruct((t, 2 * vw), BF16),
        grid=(bsz, ns),
        in_specs=[pl.BlockSpec((ls, qw), rowmap(0)),
                  pl.BlockSpec((ls, qw), rowmap(1)),
                  pl.BlockSpec((ls, vw), rowmap(1)),
                  pl.BlockSpec((ls, vw), rowmap(2)),
                  pl.BlockSpec((ls, qw), rowmap(6)),
                  pl.BlockSpec((ls, qw), rowmap(7)),
                  pl.BlockSpec((ls, vw), rowmap(4)),
                  pl.BlockSpec((ls, vw), rowmap(5)),
                  pl.BlockSpec((ls, qw), rowmap(0)),
                  pl.BlockSpec((1, vw), const),
                  pl.BlockSpec((1, vw), const),
                  pl.BlockSpec((RET_HEADS, L, L), const3),
                  pl.BlockSpec((RET_HEADS, L, RET_DK), const3),
                  pl.BlockSpec((RET_HEADS, L, RET_DK), const3)],
        out_specs=pl.BlockSpec((ls, 2 * vw), lambda b, i: (b * ns + i, 0)),
        scratch_shapes=[pltpu.VMEM((RET_HEADS, RET_DK, RET_DV), F32),
                        pltpu.VMEM((GLA_HEADS, GLA_DV, GLA_DK), F32)],
        compiler_params=_cparams(("arbitrary", "arbitrary")),
        name="ret_gla_mixer",
    )(y, y, y, y, y, y, y, y, glog, ret_norm, gla_norm, decay, zeta, xi)


def _odd_proj_kernel(x1_ref, y1_ref, y2_ref, gate_ref, modp_ref, mod_ref, gain_ref, w_ref, wdt_ref, dtb_ref,
                     y_ref, dt_ref, x_ref, *, ncol, cch):
    x = _moe_residual(x1_ref[...], y1_ref, y2_ref, gate_ref[0], modp_ref[0][5:6, :])
    x_ref[...] = x
    h = _norm_mod(x, gain_ref[...], mod_ref[0], 0, 1)
    hb = h.astype(BF16)
    width = w_ref.shape[1] // ncol
    for j in range(ncol):
        c0 = j * width
        acc = _dot(hb, w_ref[:, c0:c0 + width])
        if c0 >= cch:
            acc = _silu(acc)
        y_ref[:, c0:c0 + width] = acc.astype(y_ref.dtype)
    z = _dot(hb, wdt_ref[...]) + dtb_ref[...]
    dt_ref[...] = jnp.maximum(z, 0.0) + jnp.log1p(jnp.exp(-jnp.abs(z)))


def _odd_proj(x1, y12, gates, mod_prev, mod_l, gain, w_main, w_dt, dt_bias, cch, tm, seq):
    t, d = x1.shape
    n = w_main.shape[1]
    nt = t // tm
    tiles_per_b = seq // tm
    const = lambda i: (0, 0)
    batch = lambda i: (i // tiles_per_b, 0, 0)
    return pl.pallas_call(
        functools.partial(_odd_proj_kernel, ncol=n // 512, cch=cch),
        out_shape=(jax.ShapeDtypeStruct((t, n), BF16), jax.ShapeDtypeStruct((t, LANES), F32),
                   jax.ShapeDtypeStruct((t, d), F32)),
        grid=(nt,),
        in_specs=[pl.BlockSpec((tm, d), lambda i: (i, 0)),
                  pl.BlockSpec((tm, _packed_width(d)), lambda i: (i, 0)),
                  pl.BlockSpec((tm, _packed_width(d)), lambda i: (nt + i, 0)),
                  pl.BlockSpec((1, TOP_K, tm), lambda i: (i, 0, 0)),
                  pl.BlockSpec((1, 6, d), batch),
                  pl.BlockSpec((1, 6, d), batch),
                  pl.BlockSpec((1, d), const),
                  pl.BlockSpec((d, n), const, pipeline_mode=pl.Buffered(1)),
                  pl.BlockSpec((d, LANES), const),
                  pl.BlockSpec((1, LANES), const)],
        out_specs=(pl.BlockSpec((tm, n), lambda i: (i, 0)), pl.BlockSpec((tm, LANES), lambda i: (i, 0)),
                   pl.BlockSpec((tm, d), lambda i: (i, 0))),
        compiler_params=_cparams(("arbitrary",)),
        name="odd_proj",
    )(x1, y12, y12, gates, mod_prev, mod_l, gain, w_main, w_dt, dt_bias)


def _ssd_kernel(z_ref, xbc_ref, dt_ref, cw_ref, cb_ref, alog_ref, dsk_ref, ng_ref, o_ref, hist_ref, st_ref):
    @pl.when(pl.program_id(1) == 0)
    def _():
        hist_ref[...] = jnp.zeros_like(hist_ref)
        st_ref[...] = jnp.zeros_like(st_ref)

    L = xbc_ref.shape[0]
    P, N, G = SSD_HEADDIM, SSD_STATE, SSD_GROUPS
    dinner = z_ref.shape[1]
    gw = dinner // G
    hpg = gw // P
    slab = gw + 2 * N
    hrows = hist_ref.shape[0]

    dt = dt_ref[...]
    da = dt * (-jnp.exp(alog_ref[...]))
    trow = _iota((L, L), 0)
    tcol = _iota((L, L), 1)
    causal = trow >= tcol
    cum = _dot_exact_rhs(causal.astype(BF16), da)
    cum_t = jnp.transpose(cum)
    dt_t = jnp.transpose(dt)
    ecum = jnp.exp(cum)
    wend = dt * jnp.exp(cum[L - 1:L, :] - cum)
    lane = _iota((L, LANES), 1)
    lo_half = lane < P

    def pair_cols(a, ha):
        return jnp.where(lo_half, jnp.broadcast_to(a[:, ha:ha + 1], (L, LANES)),
                         jnp.broadcast_to(a[:, ha + 1:ha + 2], (L, LANES)))

    srow = _iota((L, hrows + L), 0)
    scol = _iota((L, hrows + L), 1)
    shift = jnp.concatenate([(scol == srow + (hrows - s)).astype(BF16) for s in range(1, SSD_CONV)], axis=0)

    def group_item(g):
        c0 = g * slab
        xin = xbc_ref[:, c0:c0 + slab]
        xcat = jnp.concatenate([hist_ref[:, c0:c0 + slab], xin], axis=0)
        hist_ref[:, c0:c0 + slab] = xin[L - hrows:L]
        sh = _dot(shift, xcat)
        conv = xin.astype(F32) * cw_ref[SSD_CONV - 1:SSD_CONV, c0:c0 + slab] + cb_ref[:, c0:c0 + slab]
        for s in range(1, SSD_CONV):
            conv = conv + sh[(s - 1) * L:s * L] * cw_ref[SSD_CONV - 1 - s:SSD_CONV - s, c0:c0 + slab]
        act = _silu(conv)
        xs = act[:, :gw]
        bm_g = act[:, gw:gw + N].astype(BF16)
        cm_g = act[:, gw + N:].astype(BF16)
        yield
        cb = _dot_nt(cm_g, bm_g)
        lhs, xw, ec = [], [], []
        for pr in range(hpg // 2):
            ha = g * hpg + 2 * pr
            ms = []
            for hh in (ha, ha + 1):
                seg = jnp.broadcast_to(cum[:, hh:hh + 1], (L, L)) - cum_t[hh:hh + 1, :]
                ms.append((cb * jnp.where(causal, jnp.exp(seg), 0.0) * dt_t[hh:hh + 1, :]).astype(BF16))
            lhs.append(jnp.concatenate(ms, axis=1))
            xw.append((xs[:, pr * 2 * P:(pr + 1) * 2 * P] * pair_cols(wend, ha)).astype(BF16))
            ec.append(pair_cols(ecum, ha))
        ecum_g = jnp.concatenate(ec, axis=1)
        yield
        xs_b = xs.astype(BF16)
        pieces = []
        for pr in range(hpg // 2):
            blk = xs_b[:, pr * 2 * P:(pr + 1) * 2 * P]
            zero = jnp.zeros_like(blk)
            rhs = jnp.concatenate([jnp.where(lo_half, blk, zero), jnp.where(lo_half, zero, blk)], axis=0)
            pieces.append(_dot(lhs[pr], rhs))
        st = st_ref[g]
        cross = _dot(cm_g, st.astype(BF16)) * ecum_g
        st_ref[g] = st * ecum_g[L - 1:L, :] + _dot_tn(bm_g, jnp.concatenate(xw, axis=1))
        y = jnp.concatenate(pieces, axis=1) + cross + dsk_ref[:, g * gw:(g + 1) * gw] * xs
        y = y * z_ref[:, g * gw:(g + 1) * gw].astype(F32)
        o_ref[:, g * gw:(g + 1) * gw] = (_rms(y) * ng_ref[:, g * gw:(g + 1) * gw]).astype(o_ref.dtype)

    _interleave([group_item(g) for g in range(G)], SSD_STARTS)


SSD_STARTS = 1
SSD_HIST_ROWS = 16


def _ssd(y, dt, conv_w, conv_b, a_log_row, d_row, norm_g, bsz, seq):
    t = y.shape[0]
    L = SSD_CHUNK
    ns = seq // L
    dinner = norm_g.shape[1]
    cch = conv_w.shape[1]
    gw = dinner // SSD_GROUPS
    const = lambda b, i: (0, 0)
    assert cch % dinner == 0
    return pl.pallas_call(
        _ssd_kernel,
        out_shape=jax.ShapeDtypeStruct((t, dinner), BF16),
        grid=(bsz, ns),
        in_specs=[pl.BlockSpec((L, dinner), lambda b, i: (b * ns + i, cch // dinner)),
                  pl.BlockSpec((L, cch), lambda b, i: (b * ns + i, 0)),
                  pl.BlockSpec((L, LANES), lambda b, i: (b * ns + i, 0)),
                  pl.BlockSpec((SSD_CONV, cch), const),
                  pl.BlockSpec((1, cch), const),
                  pl.BlockSpec((1, LANES), const),
                  pl.BlockSpec((1, dinner), const),
                  pl.BlockSpec((1, dinner), const)],
        out_specs=pl.BlockSpec((L, dinner), lambda b, i: (b * ns + i, 0)),
        scratch_shapes=[pltpu.VMEM((SSD_HIST_ROWS, cch), BF16),
                        pltpu.VMEM((SSD_GROUPS, SSD_STATE, gw), F32)],
        compiler_params=_cparams(("arbitrary", "arbitrary")),
        name="ssd_mixer",
    )(y, y, dt, conv_w, conv_b, a_log_row, d_row, norm_g)


def _outproj_kernel(o_ref, x_ref, mod_ref, gain_ref, w_ref, x1_ref, h2_ref):
    mod = mod_ref[0]
    x1 = x_ref[...] + mod[2:3, :] * _dot(o_ref[...], w_ref[...])
    x1_ref[...] = x1
    h2_ref[...] = _pack_pair(_norm_mod(x1, gain_ref[...], mod, 3, 4))


def _outproj(o, x2d, mod_l, gain2, w_out, tm, seq):
    t, d = x2d.shape
    kin = o.shape[1]
    tiles_per_b = seq // tm
    const = lambda i: (0, 0)
    return pl.pallas_call(
        _outproj_kernel,
        out_shape=(jax.ShapeDtypeStruct((t, d), F32), jax.ShapeDtypeStruct((t, _packed_width(d)), PACKED)),
        grid=(t // tm,),
        in_specs=[pl.BlockSpec((tm, kin), lambda i: (i, 0)),
                  pl.BlockSpec((tm, d), lambda i: (i, 0)),
                  pl.BlockSpec((1, 6, d), lambda i: (i // tiles_per_b, 0, 0)),
                  pl.BlockSpec((1, d), const),
                  pl.BlockSpec((kin, d), const)],
        out_specs=(pl.BlockSpec((tm, d), lambda i: (i, 0)), pl.BlockSpec((tm, _packed_width(d)), lambda i: (i, 0))),
        compiler_params=_cparams(("arbitrary",)),
        name="out_proj",
    )(o, x2d, mod_l, gain2, w_out)


def _route_kernel(h_ref, rw_ref, rb_ref, dest_ref, gate_ref, blk_ref, cnt_ref, pst_ref, sel_ref, *, nbpad):
    phase = pl.program_id(0)
    step = pl.program_id(1)
    tm = h_ref.shape[0]
    ne = rw_ref.shape[0]
    epg = ne // N_GROUPS
    ei = _iota((ne, tm), 0).astype(F32)

    @pl.when(jnp.logical_and(phase == 0, step == 0))
    def _():
        cnt_ref[...] = jnp.zeros_like(cnt_ref)

    @pl.when(phase == 0)
    def _():
        logits = _dot_nt(rw_ref[...], _unpack_pair(h_ref[...]).astype(BF16))
        s = jax.nn.sigmoid(logits)
        sb = s + jnp.concatenate([rb_ref[...]] * (tm // LANES), axis=1)
        G = N_GROUPS
        v = [sb[j * G:(j + 1) * G, :] for j in range(epg)]
        m01, n01 = jnp.maximum(v[0], v[1]), jnp.minimum(v[0], v[1])
        m23, n23 = jnp.maximum(v[2], v[3]), jnp.minimum(v[2], v[3])
        top1 = jnp.maximum(m01, m23)
        top2 = jnp.maximum(jnp.minimum(m01, m23), jnp.maximum(n01, n23))
        gscore = top1 + top2
        gi = _iota((G, tm), 0)
        gmax = jnp.max(gscore, axis=0, keepdims=True)
        gidx = jnp.min(jnp.where(gscore == gmax, gi, G), axis=0, keepdims=True)
        sel = gi == gidx
        vb = [jnp.sum(jnp.where(sel, v[j], 0.0), axis=0, keepdims=True) for j in range(epg)]
        vs = [jnp.sum(jnp.where(sel, s[j * G:(j + 1) * G, :], 0.0), axis=0, keepdims=True) for j in range(epg)]
        zero = jnp.zeros_like(vb[0])
        e1 = zero
        e2 = zero
        w1 = zero
        w2 = zero
        for j in range(epg):
            rank = zero
            for i in range(epg):
                if i == j:
                    continue
                ahead = (vb[i] >= vb[j]) if i < j else (vb[i] > vb[j])
                rank = rank + jnp.where(ahead, 1.0, 0.0)
            first = rank == 0.0
            second = rank == 1.0
            e1 = e1 + jnp.where(first, float(j), 0.0)
            e2 = e2 + jnp.where(second, float(j), 0.0)
            w1 = w1 + jnp.where(first, vs[j], 0.0)
            w2 = w2 + jnp.where(second, vs[j], 0.0)
        gf = gidx.astype(F32) * float(epg)
        e1 = e1 + gf
        e2 = e2 + gf
        wsum = w1 + w2
        sel_ref[step] = jnp.concatenate([e1, e2, w1 / wsum, w2 / wsum], axis=0)
        onehot = jnp.where((ei == e1) | (ei == e2), 1.0, 0.0)
        cnt_ref[...] = cnt_ref[...] + jnp.sum(onehot, axis=1, keepdims=True)

    @pl.when(jnp.logical_and(phase == 1, step == 0))
    def _():
        nblk = jnp.floor((cnt_ref[...] + (MOE_BLOCK - 1.0)) * (1.0 / MOE_BLOCK))
        hi = jnp.floor(nblk * (1.0 / 16.0))
        lo = nblk - hi * 16.0
        er = _iota((ne, ne), 0)
        ec = _iota((ne, ne), 1)
        lower = (ec < er).astype(BF16)
        pst = _dot(lower, hi.astype(BF16)) * 16.0 + _dot(lower, lo.astype(BF16))
        pst_ref[...] = pst
        pend = jnp.concatenate([pst + nblk] * (nbpad // LANES), axis=1)
        bidx = _iota((ne, nbpad), 1).astype(F32)
        be = jnp.sum(jnp.where(pend <= bidx, 1.0, 0.0), axis=0, keepdims=True)
        total = jnp.max(pend, axis=0, keepdims=True)
        blk_ref[0:1, :] = jnp.minimum(be, ne - 1.0).astype(I32)
        blk_ref[1:2, :] = total.astype(I32)
        cnt_ref[...] = jnp.zeros_like(cnt_ref)

    @pl.when(phase == 1)
    def _():
        choice = sel_ref[step]
        oh1 = ei == choice[0:1, :]
        oh2 = ei == choice[1:2, :]
        onehot = jnp.where(oh1 | oh2, 1.0, 0.0)
        upper = (_iota((tm, tm), 0) < _iota((tm, tm), 1)).astype(BF16)
        prefix = _dot(onehot.astype(BF16), upper)
        cnt = cnt_ref[...]
        base = jnp.concatenate([cnt] * (tm // LANES), axis=1) + prefix
        pst = jnp.concatenate([pst_ref[...]] * (tm // LANES), axis=1) * float(MOE_BLOCK)
        slot = base + pst
        d1 = jnp.sum(jnp.where(oh1, slot, 0.0), axis=0, keepdims=True)
        d2 = jnp.sum(jnp.where(oh2, slot, 0.0), axis=0, keepdims=True)
        cnt_ref[...] = cnt + jnp.sum(onehot, axis=1, keepdims=True)
        dest_ref[0, 0:1, :] = d1.astype(I32)
        dest_ref[0, 1:2, :] = d2.astype(I32)
        gate_ref[0] = choice[2:4, :]


def _route(h2, rw_t, rb_b, tm, nbpad):
    t, dp = h2.shape
    ne, d = rw_t.shape
    nt = t // tm
    return pl.pallas_call(
        functools.partial(_route_kernel, nbpad=nbpad),
        out_shape=(jax.ShapeDtypeStruct((nt, 2, tm), I32), jax.ShapeDtypeStruct((nt, 2, tm), F32),
                   jax.ShapeDtypeStruct((2, nbpad), I32)),
        grid=(2, nt),
        in_specs=[pl.BlockSpec((tm, dp), lambda p, i: (i * (1 - p) + (nt - 1) * p, 0)),
                  pl.BlockSpec((ne, d), lambda p, i: (0, 0)),
                  pl.BlockSpec((ne, LANES), lambda p, i: (0, 0))],
        out_specs=(pl.BlockSpec((1, 2, tm), lambda p, i: (i * p, 0, 0)),
                   pl.BlockSpec((1, 2, tm), lambda p, i: (i * p, 0, 0)),
                   pl.BlockSpec((2, nbpad), lambda p, i: (0, 0))),
        scratch_shapes=[pltpu.VMEM((ne, LANES), F32), pltpu.VMEM((ne, LANES), F32),
                        pltpu.VMEM((nt, 4, tm), F32)],
        compiler_params=_cparams(("arbitrary", "arbitrary")),
        name="router",
    )(h2, rw_t, rb_b)


SC_CORES, SC_SUBCORES = 2, 16
SC_ROWS = 64


def _sc_mesh():
    return plsc.VectorSubcoreMesh(core_axis_name="c", subcore_axis_name="s",
                                  num_cores=SC_CORES, num_subcores=SC_SUBCORES)


def _sc_worker_base(rows_per_worker):
    return (lax.axis_index("s") * SC_CORES + lax.axis_index("c")) * rows_per_worker


def _sc_scatter_rows(src, idx, n_out):
    t, d = src.shape
    assert idx.shape[0] == TOP_K * t
    per_w = t // (SC_CORES * SC_SUBCORES)
    assert per_w % (2 * SC_ROWS) == 0
    rows = pltpu.VMEM((SC_ROWS, d), src.dtype)
    ids = pltpu.VMEM((SC_ROWS,), I32)

    @functools.partial(
        pl.kernel, mesh=_sc_mesh(), out_type=jax.ShapeDtypeStruct((n_out, d), src.dtype),
        scratch_types=[rows, rows, ids, ids, ids, ids] + [pltpu.SemaphoreType.DMA] * 4,
        compiler_params=pltpu.CompilerParams(use_tc_tiling_on_sc=True), name="moe_dispatch_sc")
    def k(src_hbm, idx_hbm, out_hbm, rows_a, rows_b, ia0, ia1, ib0, ib1, s0, s1, s2, s3):
        base = _sc_worker_base(per_w)

        @pl.loop(0, per_w // (2 * SC_ROWS))
        def _(j):
            off_a = pl.multiple_of(base + 2 * j * SC_ROWS, SC_ROWS)
            off_b = pl.multiple_of(off_a + SC_ROWS, SC_ROWS)
            load_a = pltpu.async_copy(src_hbm.at[pl.ds(off_a, SC_ROWS)], rows_a, s0)
            load_b = pltpu.async_copy(src_hbm.at[pl.ds(off_b, SC_ROWS)], rows_b, s1)
            pltpu.sync_copy(idx_hbm.at[pl.ds(off_a, SC_ROWS)], ia0)
            pltpu.sync_copy(idx_hbm.at[pl.ds(t + off_a, SC_ROWS)], ia1)
            pltpu.sync_copy(idx_hbm.at[pl.ds(off_b, SC_ROWS)], ib0)
            pltpu.sync_copy(idx_hbm.at[pl.ds(t + off_b, SC_ROWS)], ib1)
            load_a.wait()
            put_a0 = pltpu.async_copy(rows_a, out_hbm.at[ia0], s0)
            put_a1 = pltpu.async_copy(rows_a, out_hbm.at[ia1], s2)
            load_b.wait()
            put_b0 = pltpu.async_copy(rows_b, out_hbm.at[ib0], s1)
            put_b1 = pltpu.async_copy(rows_b, out_hbm.at[ib1], s3)
            put_a0.wait()
            put_a1.wait()
            put_b0.wait()
            put_b1.wait()

    return k(src, idx)


def _sc_gather_rows(table, idx):
    _, d = table.shape
    b = idx.shape[0]
    per_w = b // (SC_CORES * SC_SUBCORES)
    assert per_w % (2 * SC_ROWS) == 0
    rows = pltpu.VMEM((SC_ROWS, d), table.dtype)
    ids = pltpu.VMEM((SC_ROWS,), I32)

    @functools.partial(
        pl.kernel, mesh=_sc_mesh(), out_type=jax.ShapeDtypeStruct((b, d), table.dtype),
        scratch_types=[rows, rows, ids, ids] + [pltpu.SemaphoreType.DMA] * 2,
        compiler_params=pltpu.CompilerParams(use_tc_tiling_on_sc=True), name="moe_gather_sc")
    def k(table_hbm, idx_hbm, out_hbm, rows_a, rows_b, ia, ib, s0, s1):
        base = _sc_worker_base(per_w)

        @pl.loop(0, per_w // (2 * SC_ROWS))
        def _(j):
            off_a = pl.multiple_of(base + 2 * j * SC_ROWS, SC_ROWS)
            off_b = pl.multiple_of(off_a + SC_ROWS, SC_ROWS)
            pltpu.sync_copy(idx_hbm.at[pl.ds(off_a, SC_ROWS)], ia)
            pltpu.sync_copy(idx_hbm.at[pl.ds(off_b, SC_ROWS)], ib)
            get_a = pltpu.async_copy(table_hbm.at[ia], rows_a, s0)
            get_b = pltpu.async_copy(table_hbm.at[ib], rows_b, s1)
            get_a.wait()
            put_a = pltpu.async_copy(rows_a, out_hbm.at[pl.ds(off_a, SC_ROWS)], s0)
            get_b.wait()
            put_b = pltpu.async_copy(rows_b, out_hbm.at[pl.ds(off_b, SC_ROWS)], s1)
            put_a.wait()
            put_b.wait()

    return k(table, idx)


def _expert_kernel(blk_ref, nused_ref, xs_ref, wg_ref, wu_ref, wd_ref, y_ref, wgb, wub, wdb):
    b = pl.program_id(0)
    prev = blk_ref[jnp.maximum(b - 1, 0)]
    changed = jnp.logical_or(b == 0, blk_ref[b] != prev)

    @pl.when(changed)
    def _():
        wgb[...] = wg_ref[0, 0].astype(BF16)
        wub[...] = wu_ref[0, 0].astype(BF16)
        wdb[...] = wd_ref[0, 0].astype(BF16)

    @pl.when(b < nused_ref[0])
    def _():
        xb = _unpack_pair(xs_ref[...]).astype(BF16)
        g = _dot(xb, wgb[...])
        u = _dot(xb, wub[...])
        a = (_silu(g) * u).astype(BF16)
        y_ref[...] = _pack_pair(_dot(a, wdb[...]))

    @pl.when(b >= nused_ref[0])
    def _():
        y_ref[...] = jnp.zeros_like(y_ref)


def _experts(xs, blk_e, nused, w_gate, w_up, w_down, layer):
    np_rows, dp = xs.shape
    _, ne, d, ff = w_gate.shape
    nb = np_rows // MOE_BLOCK
    grid_spec = pltpu.PrefetchScalarGridSpec(
        num_scalar_prefetch=2,
        grid=(nb,),
        in_specs=[pl.BlockSpec((MOE_BLOCK, dp), lambda b, be, nu: (b, 0)),
                  pl.BlockSpec((1, 1, d, ff), lambda b, be, nu: (layer, be[b], 0, 0)),
                  pl.BlockSpec((1, 1, d, ff), lambda b, be, nu: (layer, be[b], 0, 0)),
                  pl.BlockSpec((1, 1, ff, d), lambda b, be, nu: (layer, be[b], 0, 0))],
        out_specs=pl.BlockSpec((MOE_BLOCK, dp), lambda b, be, nu: (b, 0)),
        scratch_shapes=[pltpu.VMEM((d, ff), BF16), pltpu.VMEM((d, ff), BF16), pltpu.VMEM((ff, d), BF16)],
    )
    return pl.pallas_call(
        _expert_kernel,
        out_shape=jax.ShapeDtypeStruct((np_rows, dp), PACKED),
        grid_spec=grid_spec,
        compiler_params=_cparams(("arbitrary",)),
        name="moe_experts",
    )(blk_e, nused, xs, w_gate, w_up, w_down)


def _moe_residual(x1, y1_ref, y2_ref, gates, out_gate):
    tc, d = x1.shape
    acc = jnp.zeros((tc, d), F32)
    for k, y_ref in enumerate((y1_ref, y2_ref)):
        gcol = jnp.transpose(jnp.broadcast_to(gates[k:k + 1, :], (LANES, tc)))
        acc = acc + _unpack_pair(y_ref[...]) * jnp.concatenate([gcol] * (d // LANES), axis=1)
    return x1 + out_gate * acc


def _combine_kernel(x1_ref, y1_ref, y2_ref, gate_ref, mod_ref, fg_ref, o_ref, *, final):
    x2 = _moe_residual(x1_ref[...], y1_ref, y2_ref, gate_ref[0], mod_ref[0][5:6, :])
    if final:
        x2 = _rms(x2) * fg_ref[...]
    o_ref[...] = x2


def _combine(x1, y12, gates, mod_l, final_gain, tc, seq, final):
    t, d = x1.shape
    nt = t // tc
    tiles_per_b = seq // tc
    return pl.pallas_call(
        functools.partial(_combine_kernel, final=final),
        out_shape=jax.ShapeDtypeStruct((t, d), F32),
        grid=(nt,),
        in_specs=[pl.BlockSpec((tc, d), lambda i: (i, 0)),
                  pl.BlockSpec((tc, _packed_width(d)), lambda i: (i, 0)),
                  pl.BlockSpec((tc, _packed_width(d)), lambda i: (nt + i, 0)),
                  pl.BlockSpec((1, TOP_K, tc), lambda i: (i, 0, 0)),
                  pl.BlockSpec((1, 6, d), lambda i: (i // tiles_per_b, 0, 0)),
                  pl.BlockSpec((1, d), lambda i: (0, 0))],
        out_specs=pl.BlockSpec((tc, d), lambda i: (i, 0)),
        compiler_params=_cparams(("arbitrary",)),
        name="moe_combine",
    )(x1, y12, y12, gates, mod_l, final_gain)


ROUTE_TILE = 512


def _moe(x1, h2, rw_t, rb_b, w_gate, w_up, w_down, layer):
    t, d = x1.shape
    ne = w_gate.shape[1]
    nb = (t * TOP_K + ne * (MOE_BLOCK - 1) + MOE_BLOCK - 1) // MOE_BLOCK
    nbpad = -(-nb // LANES) * LANES
    tm = ROUTE_TILE
    dest, gates, blk = _route(h2, rw_t, rb_b, tm, nbpad)
    slot = jnp.transpose(dest, (1, 0, 2)).reshape(TOP_K * t)
    xs = _sc_scatter_rows(h2, slot, nb * MOE_BLOCK)
    ys = _experts(xs, blk[0, :nb], blk[1, :1], w_gate, w_up, w_down, layer)
    return _sc_gather_rows(ys, slot), gates


def kernel(x, c, positions, mod_w, mod_b, norm_mix, norm_ffn, ev_w_in, ev_ret_norm, ev_gla_wa2, ev_gla_ba,
           ev_gla_norm, ev_w_out, od_w_in, od_conv_w, od_conv_b, od_dt_bias, od_a_log, od_d, od_norm, od_w_out,
           router_w, router_b, exp_w_gate, exp_w_up, exp_w_down, final_norm):
    bsz, seq, d = x.shape
    depth = mod_w.shape[0]
    t = bsz * seq
    ne = router_w.shape[1]
    epg = ne // N_GROUPS
    mod = _modulation(c, mod_w, mod_b)
    x2d = x.reshape(t, d)

    perm = np.array([g * epg + j for j in range(epg) for g in range(N_GROUPS)])
    rw_t = router_w.T[perm].astype(BF16)
    rb_b = jnp.broadcast_to(router_b.astype(F32)[perm][:, None], (ne, LANES))
    final_gain = final_norm.reshape(1, d)

    half = RET_DK // 2
    inv = ROPE_BASE ** (-jnp.arange(half, dtype=F32) / half)
    inv2 = jnp.concatenate([inv, inv]).reshape(1, LANES)
    pos_col = positions.reshape(t, 1)

    pending = None
    for layer in range(depth):
        i = layer // 2
        mod_l = mod[layer]
        gain1 = norm_mix[layer].reshape(1, d)
        gain2 = norm_ffn[layer].reshape(1, d)
        if layer % 2 == 0:
            if pending is not None:
                x2d = _combine(*pending, final_gain, ROUTE_TILE, seq, final=False)
            w_in = ev_w_in[i]
            nmain = w_in.shape[1] - GLA_RANK
            w_main = w_in[:, :nmain].astype(BF16)
            w_ga = jnp.pad(w_in[:, nmain:], ((0, 0), (0, LANES - GLA_RANK))).astype(BF16)
            wa2 = jnp.pad(ev_gla_wa2[i], ((0, LANES - GLA_RANK), (0, 0))).astype(BF16)
            y, glog = _even_proj(x2d, mod_l, gain1, pos_col, inv2, w_main, w_ga, wa2,
                                 ev_gla_ba[i].reshape(1, -1), PROJ_TILE, seq)
            o = _ret_gla(y, glog, ev_ret_norm[i].reshape(1, -1), ev_gla_norm[i].reshape(1, -1), bsz, seq, 512)
            w_out = ev_w_out[i].astype(BF16)
        else:
            assert pending is not None
            w_in = od_w_in[i]
            nheads = od_a_log.shape[1]
            nmain = w_in.shape[1] - nheads
            dinner = od_norm.shape[1]
            gw, ns_ = dinner // SSD_GROUPS, SSD_STATE
            bc0 = dinner + SSD_GROUPS * ns_
            cperm = np.concatenate([np.concatenate([np.arange(g * gw, (g + 1) * gw),
                                                    dinner + np.arange(g * ns_, (g + 1) * ns_),
                                                    bc0 + np.arange(g * ns_, (g + 1) * ns_)])
                                    for g in range(SSD_GROUPS)])
            conv_w = od_conv_w[i][:, cperm]
            conv_b = od_conv_b[i][cperm].reshape(1, -1)
            w_main = jnp.concatenate([w_in[:, dinner + cperm], w_in[:, :dinner]], axis=1).astype(BF16)
            w_dt = jnp.pad(w_in[:, nmain:], ((0, 0), (0, LANES - nheads))).astype(BF16)
            dtb = jnp.pad(od_dt_bias[i], (0, LANES - nheads)).reshape(1, LANES)
            y, dt, x2d = _odd_proj(*pending, mod_l, gain1, w_main, w_dt, dtb, od_conv_w.shape[2], ROUTE_TILE, seq)
            alog = jnp.pad(od_a_log[i], (0, LANES - nheads)).reshape(1, LANES)
            d_row = jnp.repeat(od_d[i], SSD_HEADDIM).reshape(1, dinner)
            o = _ssd(y, dt, conv_w, conv_b, alog, d_row, od_norm[i].reshape(1, dinner), bsz, seq)
            w_out = od_w_out[i].astype(BF16)
        x1, h2 = _outproj(o, x2d, mod_l, gain2, w_out, 512, seq)
        y12, gates = _moe(x1, h2, rw_t, rb_b, exp_w_gate, exp_w_up, exp_w_down, layer)
        pending = (x1, y12, gates, mod_l)
    out = _combine(*pending, final_gain, ROUTE_TILE, seq, final=True)
    return out.reshape(bsz, seq, d)
```

```python
import functools

import jax
import jax.numpy as jnp
import numpy as np
from jax import lax
from jax.experimental import pallas as pl
from jax.experimental.pallas import tpu as pltpu
from jax.experimental.pallas import tpu_sc as plsc

F32 = jnp.float32
BF16 = jnp.bfloat16
I32 = jnp.int32

RET_HEADS, RET_DK, RET_DV, RET_CHUNK = 4, 128, 256, 128
ROPE_BASE = 10000.0
GLA_HEADS, GLA_DK, GLA_DV, GLA_RANK, GLA_TAU, GLA_CHUNK = 4, 128, 256, 16, 16.0, 64
SSD_HEADDIM, SSD_GROUPS, SSD_STATE, SSD_CONV, SSD_CHUNK = 64, 8, 128, 4, 128
N_GROUPS, TOP_K, GROUP_SCORE_K, MOE_BLOCK = 8, 2, 2, 256
NORM_EPS = 1e-6

PROJ_TILE = 512
LANES = 128
VMEM_LIMIT = 56 * 1024 * 1024


def _cparams(sem, vmem=VMEM_LIMIT):
    return pltpu.CompilerParams(dimension_semantics=sem, vmem_limit_bytes=vmem)


def _dot(a, b):
    return jnp.dot(a, b, preferred_element_type=F32)


def _dot_nt(a, b):
    return lax.dot_general(a, b, (((1,), (1,)), ((), ())), preferred_element_type=F32)


def _dot_tn(a, b):
    return lax.dot_general(a, b, (((0,), (0,)), ((), ())), preferred_element_type=F32)


def _split3(x):
    a = x.astype(BF16)
    r = x - a.astype(F32)
    b = r.astype(BF16)
    c = (r - b.astype(F32)).astype(BF16)
    return a, b, c


def _dot_exact_rhs(m01, x):
    a, b, c = _split3(x)
    return _dot(m01, a) + _dot(m01, b) + _dot(m01, c)


def _dot_exact_lhs(x, m01):
    a, b, c = _split3(x)
    return _dot(a, m01) + _dot(b, m01) + _dot(c, m01)


def _dot_tn_exact(x, m01):
    a, b, c = _split3(x)
    return _dot_tn(a, m01) + _dot_tn(b, m01) + _dot_tn(c, m01)


def _silu(x):
    return x * jax.nn.sigmoid(x)


def _rms(x, eps=NORM_EPS):
    return x * lax.rsqrt(jnp.mean(x * x, axis=-1, keepdims=True) + eps)


def _iota(shape, dim):
    return lax.broadcasted_iota(I32, shape, dim)


def _interleave(items, starts_per_round):
    waiting, running = list(items), []
    while waiting or running:
        running += [waiting.pop(0) for _ in range(min(starts_per_round, len(waiting)))]
        for item in reversed(list(running)):
            if next(item, StopIteration) is StopIteration:
                running.remove(item)


PACKED = jnp.uint32
_HI16 = 0xFFFF0000


def _packed_width(n):
    return n // 2


def _pack_pair(x):
    n = x.shape[1] // 2
    lo = lax.bitcast_convert_type(x[:, :n].astype(BF16).astype(F32), jnp.uint32)
    hi = lax.bitcast_convert_type(x[:, n:].astype(BF16).astype(F32), jnp.uint32)
    return (hi & jnp.uint32(_HI16)) | (lo >> 16)


def _unpack_pair(w):
    lo = lax.bitcast_convert_type(w << 16, F32)
    hi = lax.bitcast_convert_type(w & jnp.uint32(_HI16), F32)
    return jnp.concatenate([lo, hi], axis=1)


def _mod_kernel(c_ref, w_ref, b_ref, o_ref):
    sc = _silu(c_ref[...])
    o_ref[0] = _dot(sc.astype(BF16), w_ref[0].astype(BF16)) + b_ref[0]


def _modulation(c, mod_w, mod_b):
    depth, d, d6 = mod_w.shape
    bsz = c.shape[0]
    nb = d6 // d
    out = pl.pallas_call(
        _mod_kernel,
        out_shape=jax.ShapeDtypeStruct((depth, bsz, d6), F32),
        grid=(depth, nb),
        in_specs=[pl.BlockSpec((bsz, d), lambda l, j: (0, 0)),
                  pl.BlockSpec((1, d, d), lambda l, j: (l, 0, j)),
                  pl.BlockSpec((1, 1, d), lambda l, j: (l, 0, j))],
        out_specs=pl.BlockSpec((1, bsz, d), lambda l, j: (l, 0, j)),
        compiler_params=_cparams(("arbitrary", "arbitrary")),
        name="adaln_mod",
    )(c, mod_w, mod_b.reshape(depth, 1, d6))
    return out.reshape(depth, bsz, nb, d)


def _norm_mod(x, gain, mod, shift_row, scale_row):
    y = _rms(x) * gain
    return y * (1.0 + mod[scale_row:scale_row + 1, :]) + mod[shift_row:shift_row + 1, :]


def _even_proj_kernel(x_ref, mod_ref, gain_ref, pos_ref, inv_ref, w_ref, wga_ref, wa2_ref, ba_ref,
                      y_ref, glog_ref, *, ncol):
    h = _norm_mod(x_ref[...], gain_ref[...], mod_ref[0], 0, 1)
    hb = h.astype(BF16)
    tm = hb.shape[0]
    ang = pos_ref[...].astype(F32) * inv_ref[...]
    cos = jnp.cos(ang)
    lane = _iota((tm, LANES), 1)
    sin = jnp.where(lane < LANES // 2, -jnp.sin(ang), jnp.sin(ang))
    qk_scale = RET_DK ** -0.5
    rqk = RET_HEADS * RET_DK
    rv = RET_HEADS * RET_DV
    gq0 = 2 * rqk + 2 * rv
    width = w_ref.shape[1] // ncol
    for j in range(ncol):
        c0 = j * width
        acc = _dot(hb, w_ref[:, c0:c0 + width])
        for s in range(width // LANES):
            col = c0 + s * LANES
            blk = acc[:, s * LANES:(s + 1) * LANES]
            if col < 2 * rqk:
                blk = blk * cos + pltpu.roll(blk, LANES // 2, 1) * sin
                if col >= rqk:
                    blk = blk * qk_scale
            elif gq0 <= col < gq0 + GLA_HEADS * GLA_DK:
                blk = blk * (GLA_DK ** -0.5)
            y_ref[:, col:col + LANES] = blk.astype(y_ref.dtype)
    ga = _dot(hb, wga_ref[...])
    z = _dot(ga.astype(BF16), wa2_ref[...]) + ba_ref[...]
    logsig = jnp.minimum(z, 0.0) - jnp.log1p(jnp.exp(-jnp.abs(z)))
    glog_ref[...] = logsig * (1.0 / GLA_TAU)


def _even_proj(x2d, mod_l, gain, pos_col, inv2, w_main, w_ga, wa2, ba, tm, seq):
    t, d = x2d.shape
    n = w_main.shape[1]
    tiles_per_b = seq // tm
    gk = wa2.shape[1]
    const = lambda i: (0, 0)
    return pl.pallas_call(
        functools.partial(_even_proj_kernel, ncol=n // 512),
        out_shape=(jax.ShapeDtypeStruct((t, n), BF16), jax.ShapeDtypeStruct((t, gk), F32)),
        grid=(t // tm,),
        in_specs=[pl.BlockSpec((tm, d), lambda i: (i, 0)),
                  pl.BlockSpec((1, 6, d), lambda i: (i // tiles_per_b, 0, 0)),
                  pl.BlockSpec((1, d), const),
                  pl.BlockSpec((tm, 1), lambda i: (i, 0)),
                  pl.BlockSpec((1, LANES), const),
                  pl.BlockSpec((d, n), const, pipeline_mode=pl.Buffered(1)),
                  pl.BlockSpec((d, LANES), const),
                  pl.BlockSpec((LANES, gk), const),
                  pl.BlockSpec((1, gk), const)],
        out_specs=(pl.BlockSpec((tm, n), lambda i: (i, 0)), pl.BlockSpec((tm, gk), lambda i: (i, 0))),
        compiler_params=_cparams(("arbitrary",)),
        name="even_proj",
    )(x2d, mod_l, gain, pos_col, inv2, w_main, w_ga, wa2, ba)


RET_GLA_UNROLL = 4
RET_GLA_STARTS = 4


def _ret_gla_kernel(rq_ref, rk_ref, rv_ref, rg_ref, gq_ref, gk_ref, gv_ref, gg_ref, gl_ref,
                    rn_ref, gn_ref, dec_ref, zeta_ref, xi_ref, o_ref, rst_ref, gst_ref, *, chunk_decay):
    @pl.when(pl.program_id(1) == 0)
    def _():
        rst_ref[...] = jnp.zeros_like(rst_ref)
        gst_ref[...] = jnp.zeros_like(gst_ref)

    ls = rq_ref.shape[0]
    lr, lg = RET_CHUNK, GLA_CHUNK
    trow = _iota((lg, lg), 0)
    tcol = _iota((lg, lg), 1)
    causal = trow >= tcol
    tri = causal.astype(BF16)
    base = RET_HEADS * RET_DV

    def ret_item(r0, h):
        q = rq_ref[pl.ds(r0, lr), h * RET_DK:(h + 1) * RET_DK]
        k = rk_ref[pl.ds(r0, lr), h * RET_DK:(h + 1) * RET_DK]
        v = rv_ref[pl.ds(r0, lr), h * RET_DV:(h + 1) * RET_DV]
        s = _dot_nt(q, k) * dec_ref[h]
        qx = (q.astype(F32) * xi_ref[h]).astype(BF16)
        kz = (k.astype(F32) * zeta_ref[h]).astype(BF16)
        yield
        inner = _dot(s.astype(BF16), v)
        kv = _dot_tn(kz, v)
        yield
        gate = _silu(rg_ref[pl.ds(r0, lr), h * RET_DV:(h + 1) * RET_DV].astype(F32))
        yield
        st = rst_ref[h]
        cross = _dot(qx, st.astype(BF16))
        rst_ref[h] = st * chunk_decay[h] + kv
        o = _rms(inner + cross) * rn_ref[:, h * RET_DV:(h + 1) * RET_DV] * gate
        o_ref[pl.ds(r0, lr), h * RET_DV:(h + 1) * RET_DV] = o.astype(o_ref.dtype)

    def gla_item(r0, h):
        gl = gl_ref[pl.ds(r0, lg), h * GLA_DK:(h + 1) * GLA_DK]
        b = _dot_exact_rhs(tri, gl)
        yield
        q = gq_ref[pl.ds(r0, lg), h * GLA_DK:(h + 1) * GLA_DK].astype(F32)
        k = gk_ref[pl.ds(r0, lg), h * GLA_DK:(h + 1) * GLA_DK].astype(F32)
        bref = b[lg // 2:lg // 2 + 1, :]
        blast = b[lg - 1:lg, :]
        att = _dot_nt((q * jnp.exp(b - bref)).astype(BF16), (k * jnp.exp(bref - b)).astype(BF16))
        qb = (q * jnp.exp(b)).astype(BF16)
        kd = (k * jnp.exp(blast - b)).astype(BF16)
        dec = jnp.exp(blast)
        yield
        v = gv_ref[pl.ds(r0, lg), h * GLA_DV:(h + 1) * GLA_DV]
        inner = _dot(jnp.where(causal, att, 0.0).astype(BF16), v)
        kvt = _dot_tn(v, kd)
        gate = _silu(gg_ref[pl.ds(r0, lg), h * GLA_DV:(h + 1) * GLA_DV].astype(F32))
        yield
        st = gst_ref[h]
        cross = _dot_nt(qb, st.astype(BF16))
        gst_ref[h] = st * dec + kvt
        o = _rms(inner + cross) * gn_ref[:, h * GLA_DV:(h + 1) * GLA_DV] * gate
        o_ref[pl.ds(r0, lg), base + h * GLA_DV:base + (h + 1) * GLA_DV] = o.astype(o_ref.dtype)

    def step(c, carry):
        waiting = []
        for u in range(RET_GLA_UNROLL):
            r0 = pl.multiple_of((c * RET_GLA_UNROLL + u) * lr, lr)
            waiting += [ret_item(r0, h) for h in range(RET_HEADS)]
            for j in range(lr // lg):
                rj = pl.multiple_of(r0 + j * lg, lg)
                waiting += [gla_item(rj, h) for h in range(GLA_HEADS)]
        assert RET_HEADS % RET_GLA_STARTS == 0 and GLA_HEADS % RET_GLA_STARTS == 0
        _interleave(waiting, RET_GLA_STARTS)
        return carry

    lax.fori_loop(0, ls // (lr * RET_GLA_UNROLL), step, 0)


def _ret_gla(y, glog, ret_norm, gla_norm, bsz, seq, ls):
    t = y.shape[0]
    ns = seq // ls
    qw, vw = RET_HEADS * RET_DK, RET_HEADS * RET_DV
    L = RET_CHUNK
    log_gamma = jnp.log1p(-jnp.exp2(-5.0 - jnp.arange(RET_HEADS, dtype=F32)))
    idx = jnp.arange(L, dtype=F32)
    diff = idx[:, None] - idx[None, :]
    decay = jnp.where(diff >= 0, jnp.exp(log_gamma[:, None, None] * jnp.maximum(diff, 0.0)), 0.0)
    zeta = jnp.broadcast_to(jnp.exp(log_gamma[:, None] * (L - 1 - idx)[None, :])[:, :, None], (RET_HEADS, L, RET_DK))
    xi = jnp.broadcast_to(jnp.exp(log_gamma[:, None] * (idx + 1.0)[None, :])[:, :, None], (RET_HEADS, L, RET_DK))
    chunk_decay = tuple(float(np.exp(np.float32(np.log1p(-np.exp2(np.float32(-5.0 - i)))) * np.float32(L)))
                        for i in range(RET_HEADS))
    rowmap = lambda j: (lambda b, i: (b * ns + i, j))
    const = lambda b, i: (0, 0)
    const3 = lambda b, i: (0, 0, 0)
    return pl.pallas_call(
        functools.partial(_ret_gla_kernel, chunk_decay=chunk_decay),
        out_shape=jax.ShapeDtypeStruct((t, 2 * vw), BF16),
        grid=(bsz, ns),
        in_specs=[pl.BlockSpec((ls, qw), rowmap(0)),
                  pl.BlockSpec((ls, qw), rowmap(1)),
                  pl.BlockSpec((ls, vw), rowmap(1)),
                  pl.BlockSpec((ls, vw), rowmap(2)),
                  pl.BlockSpec((ls, qw), rowmap(6)),
                  pl.BlockSpec((ls, qw), rowmap(7)),
                  pl.BlockSpec((ls, vw), rowmap(4)),
                  pl.BlockSpec((ls, vw), rowmap(5)),
                  pl.BlockSpec((ls, qw), rowmap(0)),
                  pl.BlockSpec((1, vw), const),
                  pl.BlockSpec((1, vw), const),
                  pl.BlockSpec((RET_HEADS, L, L), const3),
                  pl.BlockSpec((RET_HEADS, L, RET_DK), const3),
                  pl.BlockSpec((RET_HEADS, L, RET_DK), const3)],
        out_specs=pl.BlockSpec((ls, 2 * vw), lambda b, i: (b * ns + i, 0)),
        scratch_shapes=[pltpu.VMEM((RET_HEADS, RET_DK, RET_DV), F32),
                        pltpu.VMEM((GLA_HEADS, GLA_DV, GLA_DK), F32)],
        compiler_params=_cparams(("arbitrary", "arbitrary")),
        name="ret_gla_mixer",
    )(y, y, y, y, y, y, y, y, glog, ret_norm, gla_norm, decay, zeta, xi)


def _odd_proj_kernel(x1_ref, y1_ref, y2_ref, gate_ref, modp_ref, mod_ref, gain_ref, w_ref, wdt_ref, dtb_ref,
                     y_ref, dt_ref, x_ref, *, ncol, cch):
    x = _moe_residual(x1_ref[...], y1_ref, y2_ref, gate_ref[0], modp_ref[0][5:6, :])
    x_ref[...] = x
    h = _norm_mod(x, gain_ref[...], mod_ref[0], 0, 1)
    hb = h.astype(BF16)
    width = w_ref.shape[1] // ncol
    for j in range(ncol):
        c0 = j * width
        acc = _dot(hb, w_ref[:, c0:c0 + width])
        if c0 >= cch:
            acc = _silu(acc)
        y_ref[:, c0:c0 + width] = acc.astype(y_ref.dtype)
    z = _dot(hb, wdt_ref[...]) + dtb_ref[...]
    dt_ref[...] = jnp.maximum(z, 0.0) + jnp.log1p(jnp.exp(-jnp.abs(z)))


def _odd_proj(x1, y12, gates, mod_prev, mod_l, gain, w_main, w_dt, dt_bias, cch, tm, seq):
    t, d = x1.shape
    n = w_main.shape[1]
    nt = t // tm
    tiles_per_b = seq // tm
    const = lambda i: (0, 0)
    batch = lambda i: (i // tiles_per_b, 0, 0)
    return pl.pallas_call(
        functools.partial(_odd_proj_kernel, ncol=n // 512, cch=cch),
        out_shape=(jax.ShapeDtypeStruct((t, n), BF16), jax.ShapeDtypeStruct((t, LANES), F32),
                   jax.ShapeDtypeStruct((t, d), F32)),
        grid=(nt,),
        in_specs=[pl.BlockSpec((tm, d), lambda i: (i, 0)),
                  pl.BlockSpec((tm, _packed_width(d)), lambda i: (i, 0)),
                  pl.BlockSpec((tm, _packed_width(d)), lambda i: (nt + i, 0)),
                  pl.BlockSpec((1, TOP_K, tm), lambda i: (i, 0, 0)),
                  pl.BlockSpec((1, 6, d), batch),
                  pl.BlockSpec((1, 6, d), batch),
                  pl.BlockSpec((1, d), const),
                  pl.BlockSpec((d, n), const, pipeline_mode=pl.Buffered(1)),
                  pl.BlockSpec((d, LANES), const),
                  pl.BlockSpec((1, LANES), const)],
        out_specs=(pl.BlockSpec((tm, n), lambda i: (i, 0)), pl.BlockSpec((tm, LANES), lambda i: (i, 0)),
                   pl.BlockSpec((tm, d), lambda i: (i, 0))),
        compiler_params=_cparams(("arbitrary",)),
        name="odd_proj",
    )(x1, y12, y12, gates, mod_prev, mod_l, gain, w_main, w_dt, dt_bias)


def _ssd_kernel(z_ref, xbc_ref, dt_ref, cw_ref, cb_ref, alog_ref, dsk_ref, ng_ref, o_ref, hist_ref, st_ref):
    @pl.when(pl.program_id(1) == 0)
    def _():
        hist_ref[...] = jnp.zeros_like(hist_ref)
        st_ref[...] = jnp.zeros_like(st_ref)

    L = xbc_ref.shape[0]
    P, N, G = SSD_HEADDIM, SSD_STATE, SSD_GROUPS
    dinner = z_ref.shape[1]
    gw = dinner // G
    hpg = gw // P
    slab = gw + 2 * N
    hrows = hist_ref.shape[0]

    dt = dt_ref[...]
    da = dt * (-jnp.exp(alog_ref[...]))
    trow = _iota((L, L), 0)
    tcol = _iota((L, L), 1)
    causal = trow >= tcol
    cum = _dot_exact_rhs(causal.astype(BF16), da)
    cum_t = jnp.transpose(cum)
    dt_t = jnp.transpose(dt)
    ecum = jnp.exp(cum)
    wend = dt * jnp.exp(cum[L - 1:L, :] - cum)
    lane = _iota((L, LANES), 1)
    lo_half = lane < P

    def pair_cols(a, ha):
        return jnp.where(lo_half, jnp.broadcast_to(a[:, ha:ha + 1], (L, LANES)),
                         jnp.broadcast_to(a[:, ha + 1:ha + 2], (L, LANES)))

    srow = _iota((L, hrows + L), 0)
    scol = _iota((L, hrows + L), 1)
    shift = jnp.concatenate([(scol == srow + (hrows - s)).astype(BF16) for s in range(1, SSD_CONV)], axis=0)

    def group_item(g):
        c0 = g * slab
        xin = xbc_ref[:, c0:c0 + slab]
        xcat = jnp.concatenate([hist_ref[:, c0:c0 + slab], xin], axis=0)
        hist_ref[:, c0:c0 + slab] = xin[L - hrows:L]
        sh = _dot(shift, xcat)
        conv = xin.astype(F32) * cw_ref[SSD_CONV - 1:SSD_CONV, c0:c0 + slab] + cb_ref[:, c0:c0 + slab]
        for s in range(1, SSD_CONV):
            conv = conv + sh[(s - 1) * L:s * L] * cw_ref[SSD_CONV - 1 - s:SSD_CONV - s, c0:c0 + slab]
        act = _silu(conv)
        xs = act[:, :gw]
        bm_g = act[:, gw:gw + N].astype(BF16)
        cm_g = act[:, gw + N:].astype(BF16)
        yield
        cb = _dot_nt(cm_g, bm_g)
        lhs, xw, ec = [], [], []
        for pr in range(hpg // 2):
            ha = g * hpg + 2 * pr
            ms = []
            for hh in (ha, ha + 1):
                seg = jnp.broadcast_to(cum[:, hh:hh + 1], (L, L)) - cum_t[hh:hh + 1, :]
                ms.append((cb * jnp.where(causal, jnp.exp(seg), 0.0) * dt_t[hh:hh + 1, :]).astype(BF16))
            lhs.append(jnp.concatenate(ms, axis=1))
            xw.append((xs[:, pr * 2 * P:(pr + 1) * 2 * P] * pair_cols(wend, ha)).astype(BF16))
            ec.append(pair_cols(ecum, ha))
        ecum_g = jnp.concatenate(ec, axis=1)
        yield
        xs_b = xs.astype(BF16)
        pieces = []
        for pr in range(hpg // 2):
            blk = xs_b[:, pr * 2 * P:(pr + 1) * 2 * P]
            zero = jnp.zeros_like(blk)
            rhs = jnp.concatenate([jnp.where(lo_half, blk, zero), jnp.where(lo_half, zero, blk)], axis=0)
            pieces.append(_dot(lhs[pr], rhs))
        st = st_ref[g]
        cross = _dot(cm_g, st.astype(BF16)) * ecum_g
        st_ref[g] = st * ecum_g[L - 1:L, :] + _dot_tn(bm_g, jnp.concatenate(xw, axis=1))
        y = jnp.concatenate(pieces, axis=1) + cross + dsk_ref[:, g * gw:(g + 1) * gw] * xs
        y = y * z_ref[:, g * gw:(g + 1) * gw].astype(F32)
        o_ref[:, g * gw:(g + 1) * gw] = (_rms(y) * ng_ref[:, g * gw:(g + 1) * gw]).astype(o_ref.dtype)

    _interleave([group_item(g) for g in range(G)], SSD_STARTS)


SSD_STARTS = 1
SSD_HIST_ROWS = 16


def _ssd(y, dt, conv_w, conv_b, a_log_row, d_row, norm_g, bsz, seq):
    t = y.shape[0]
    L = SSD_CHUNK
    ns = seq // L
    dinner = norm_g.shape[1]
    cch = conv_w.shape[1]
    gw = dinner // SSD_GROUPS
    const = lambda b, i: (0, 0)
    assert cch % dinner == 0
    return pl.pallas_call(
        _ssd_kernel,
        out_shape=jax.ShapeDtypeStruct((t, dinner), BF16),
        grid=(bsz, ns),
        in_specs=[pl.BlockSpec((L, dinner), lambda b, i: (b * ns + i, cch // dinner)),
                  pl.BlockSpec((L, cch), lambda b, i: (b * ns + i, 0)),
                  pl.BlockSpec((L, LANES), lambda b, i: (b * ns + i, 0)),
                  pl.BlockSpec((SSD_CONV, cch), const),
                  pl.BlockSpec((1, cch), const),
                  pl.BlockSpec((1, LANES), const),
                  pl.BlockSpec((1, dinner), const),
                  pl.BlockSpec((1, dinner), const)],
        out_specs=pl.BlockSpec((L, dinner), lambda b, i: (b * ns + i, 0)),
        scratch_shapes=[pltpu.VMEM((SSD_HIST_ROWS, cch), BF16),
                        pltpu.VMEM((SSD_GROUPS, SSD_STATE, gw), F32)],
        compiler_params=_cparams(("arbitrary", "arbitrary")),
        name="ssd_mixer",
    )(y, y, dt, conv_w, conv_b, a_log_row, d_row, norm_g)


def _outproj_kernel(o_ref, x_ref, mod_ref, gain_ref, w_ref, x1_ref, h2_ref):
    mod = mod_ref[0]
    x1 = x_ref[...] + mod[2:3, :] * _dot(o_ref[...], w_ref[...])
    x1_ref[...] = x1
    h2_ref[...] = _pack_pair(_norm_mod(x1, gain_ref[...], mod, 3, 4))


def _outproj(o, x2d, mod_l, gain2, w_out, tm, seq):
    t, d = x2d.shape
    kin = o.shape[1]
    tiles_per_b = seq // tm
    const = lambda i: (0, 0)
    return pl.pallas_call(
        _outproj_kernel,
        out_shape=(jax.ShapeDtypeStruct((t, d), F32), jax.ShapeDtypeStruct((t, _packed_width(d)), PACKED)),
        grid=(t // tm,),
        in_specs=[pl.BlockSpec((tm, kin), lambda i: (i, 0)),
                  pl.BlockSpec((tm, d), lambda i: (i, 0)),
                  pl.BlockSpec((1, 6, d), lambda i: (i // tiles_per_b, 0, 0)),
                  pl.BlockSpec((1, d), const),
                  pl.BlockSpec((kin, d), const)],
        out_specs=(pl.BlockSpec((tm, d), lambda i: (i, 0)), pl.BlockSpec((tm, _packed_width(d)), lambda i: (i, 0))),
        compiler_params=_cparams(("arbitrary",)),
        name="out_proj",
    )(o, x2d, mod_l, gain2, w_out)


def _route_kernel(h_ref, rw_ref, rb_ref, dest_ref, gate_ref, blk_ref, cnt_ref, pst_ref, sel_ref, *, nbpad):
    phase = pl.program_id(0)
    step = pl.program_id(1)
    tm = h_ref.shape[0]
    ne = rw_ref.shape[0]
    epg = ne // N_GROUPS
    ei = _iota((ne, tm), 0).astype(F32)

    @pl.when(jnp.logical_and(phase == 0, step == 0))
    def _():
        cnt_ref[...] = jnp.zeros_like(cnt_ref)

    @pl.when(phase == 0)
    def _():
        logits = _dot_nt(rw_ref[...], _unpack_pair(h_ref[...]).astype(BF16))
        s = jax.nn.sigmoid(logits)
        sb = s + jnp.concatenate([rb_ref[...]] * (tm // LANES), axis=1)
        G = N_GROUPS
        v = [sb[j * G:(j + 1) * G, :] for j in range(epg)]
        m01, n01 = jnp.maximum(v[0], v[1]), jnp.minimum(v[0], v[1])
        m23, n23 = jnp.maximum(v[2], v[3]), jnp.minimum(v[2], v[3])
        top1 = jnp.maximum(m01, m23)
        top2 = jnp.maximum(jnp.minimum(m01, m23), jnp.maximum(n01, n23))
        gscore = top1 + top2
        gi = _iota((G, tm), 0)
        gmax = jnp.max(gscore, axis=0, keepdims=True)
        gidx = jnp.min(jnp.where(gscore == gmax, gi, G), axis=0, keepdims=True)
        sel = gi == gidx
        vb = [jnp.sum(jnp.where(sel, v[j], 0.0), axis=0, keepdims=True) for j in range(epg)]
        vs = [jnp.sum(jnp.where(sel, s[j * G:(j + 1) * G, :], 0.0), axis=0, keepdims=True) for j in range(epg)]
        zero = jnp.zeros_like(vb[0])
        e1 = zero
        e2 = zero
        w1 = zero
        w2 = zero
        for j in range(epg):
            rank = zero
            for i in range(epg):
                if i == j:
                    continue
                ahead = (vb[i] >= vb[j]) if i < j else (vb[i] > vb[j])
                rank = rank + jnp.where(ahead, 1.0, 0.0)
            first = rank == 0.0
            second = rank == 1.0
            e1 = e1 + jnp.where(first, float(j), 0.0)
            e2 = e2 + jnp.where(second, float(j), 0.0)
            w1 = w1 + jnp.where(first, vs[j], 0.0)
            w2 = w2 + jnp.where(second, vs[j], 0.0)
        gf = gidx.astype(F32) * float(epg)
        e1 = e1 + gf
        e2 = e2 + gf
        wsum = w1 + w2
        sel_ref[step] = jnp.concatenate([e1, e2, w1 / wsum, w2 / wsum], axis=0)
        onehot = jnp.where((ei == e1) | (ei == e2), 1.0, 0.0)
        cnt_ref[...] = cnt_ref[...] + jnp.sum(onehot, axis=1, keepdims=True)

    @pl.when(jnp.logical_and(phase == 1, step == 0))
    def _():
        nblk = jnp.floor((cnt_ref[...] + (MOE_BLOCK - 1.0)) * (1.0 / MOE_BLOCK))
        hi = jnp.floor(nblk * (1.0 / 16.0))
        lo = nblk - hi * 16.0
        er = _iota((ne, ne), 0)
        ec = _iota((ne, ne), 1)
        lower = (ec < er).astype(BF16)
        pst = _dot(lower, hi.astype(BF16)) * 16.0 + _dot(lower, lo.astype(BF16))
        pst_ref[...] = pst
        pend = jnp.concatenate([pst + nblk] * (nbpad // LANES), axis=1)
        bidx = _iota((ne, nbpad), 1).astype(F32)
        be = jnp.sum(jnp.where(pend <= bidx, 1.0, 0.0), axis=0, keepdims=True)
        total = jnp.max(pend, axis=0, keepdims=True)
        blk_ref[0:1, :] = jnp.minimum(be, ne - 1.0).astype(I32)
        blk_ref[1:2, :] = total.astype(I32)
        cnt_ref[...] = jnp.zeros_like(cnt_ref)

    @pl.when(phase == 1)
    def _():
        choice = sel_ref[step]
        oh1 = ei == choice[0:1, :]
        oh2 = ei == choice[1:2, :]
        onehot = jnp.where(oh1 | oh2, 1.0, 0.0)
        upper = (_iota((tm, tm), 0) < _iota((tm, tm), 1)).astype(BF16)
        prefix = _dot(onehot.astype(BF16), upper)
        cnt = cnt_ref[...]
        base = jnp.concatenate([cnt] * (tm // LANES), axis=1) + prefix
        pst = jnp.concatenate([pst_ref[...]] * (tm // LANES), axis=1) * float(MOE_BLOCK)
        slot = base + pst
        d1 = jnp.sum(jnp.where(oh1, slot, 0.0), axis=0, keepdims=True)
        d2 = jnp.sum(jnp.where(oh2, slot, 0.0), axis=0, keepdims=True)
        cnt_ref[...] = cnt + jnp.sum(onehot, axis=1, keepdims=True)
        dest_ref[0, 0:1, :] = d1.astype(I32)
        dest_ref[0, 1:2, :] = d2.astype(I32)
        gate_ref[0] = choice[2:4, :]


def _route(h2, rw_t, rb_b, tm, nbpad):
    t, dp = h2.shape
    ne, d = rw_t.shape
    nt = t // tm
    return pl.pallas_call(
        functools.partial(_route_kernel, nbpad=nbpad),
        out_shape=(jax.ShapeDtypeStruct((nt, 2, tm), I32), jax.ShapeDtypeStruct((nt, 2, tm), F32),
                   jax.ShapeDtypeStruct((2, nbpad), I32)),
        grid=(2, nt),
        in_specs=[pl.BlockSpec((tm, dp), lambda p, i: (i * (1 - p) + (nt - 1) * p, 0)),
                  pl.BlockSpec((ne, d), lambda p, i: (0, 0)),
                  pl.BlockSpec((ne, LANES), lambda p, i: (0, 0))],
        out_specs=(pl.BlockSpec((1, 2, tm), lambda p, i: (i * p, 0, 0)),
                   pl.BlockSpec((1, 2, tm), lambda p, i: (i * p, 0, 0)),
                   pl.BlockSpec((2, nbpad), lambda p, i: (0, 0))),
        scratch_shapes=[pltpu.VMEM((ne, LANES), F32), pltpu.VMEM((ne, LANES), F32),
                        pltpu.VMEM((nt, 4, tm), F32)],
        compiler_params=_cparams(("arbitrary", "arbitrary")),
        name="router",
    )(h2, rw_t, rb_b)


SC_CORES, SC_SUBCORES = 2, 16
SC_ROWS = 64


def _sc_mesh():
    return plsc.VectorSubcoreMesh(core_axis_name="c", subcore_axis_name="s",
                                  num_cores=SC_CORES, num_subcores=SC_SUBCORES)


def _sc_worker_base(rows_per_worker):
    return (lax.axis_index("s") * SC_CORES + lax.axis_index("c")) * rows_per_worker


def _sc_scatter_rows(src, idx, n_out):
    t, d = src.shape
    assert idx.shape[0] == TOP_K * t
    per_w = t // (SC_CORES * SC_SUBCORES)
    assert per_w % (2 * SC_ROWS) == 0
    rows = pltpu.VMEM((SC_ROWS, d), src.dtype)
    ids = pltpu.VMEM((SC_ROWS,), I32)

    @functools.partial(
        pl.kernel, mesh=_sc_mesh(), out_type=jax.ShapeDtypeStruct((n_out, d), src.dtype),
        scratch_types=[rows, rows, ids, ids, ids, ids] + [pltpu.SemaphoreType.DMA] * 4,
        compiler_params=pltpu.CompilerParams(use_tc_tiling_on_sc=True), name="moe_dispatch_sc")
    def k(src_hbm, idx_hbm, out_hbm, rows_a, rows_b, ia0, ia1, ib0, ib1, s0, s1, s2, s3):
        base = _sc_worker_base(per_w)

        @pl.loop(0, per_w // (2 * SC_ROWS))
        def _(j):
            off_a = pl.multiple_of(base + 2 * j * SC_ROWS, SC_ROWS)
            off_b = pl.multiple_of(off_a + SC_ROWS, SC_ROWS)
            load_a = pltpu.async_copy(src_hbm.at[pl.ds(off_a, SC_ROWS)], rows_a, s0)
            load_b = pltpu.async_copy(src_hbm.at[pl.ds(off_b, SC_ROWS)], rows_b, s1)
            pltpu.sync_copy(idx_hbm.at[pl.ds(off_a, SC_ROWS)], ia0)
            pltpu.sync_copy(idx_hbm.at[pl.ds(t + off_a, SC_ROWS)], ia1)
            pltpu.sync_copy(idx_hbm.at[pl.ds(off_b, SC_ROWS)], ib0)
            pltpu.sync_copy(idx_hbm.at[pl.ds(t + off_b, SC_ROWS)], ib1)
            load_a.wait()
            put_a0 = pltpu.async_copy(rows_a, out_hbm.at[ia0], s0)
            put_a1 = pltpu.async_copy(rows_a, out_hbm.at[ia1], s2)
            load_b.wait()
            put_b0 = pltpu.async_copy(rows_b, out_hbm.at[ib0], s1)
            put_b1 = pltpu.async_copy(rows_b, out_hbm.at[ib1], s3)
            put_a0.wait()
            put_a1.wait()
            put_b0.wait()
            put_b1.wait()

    return k(src, idx)


def _sc_gather_rows(table, idx):
    _, d = table.shape
    b = idx.shape[0]
    per_w = b // (SC_CORES * SC_SUBCORES)
    assert per_w % (2 * SC_ROWS) == 0
    rows = pltpu.VMEM((SC_ROWS, d), table.dtype)
    ids = pltpu.VMEM((SC_ROWS,), I32)

    @functools.partial(
        pl.kernel, mesh=_sc_mesh(), out_type=jax.ShapeDtypeStruct((b, d), table.dtype),
        scratch_types=[rows, rows, ids, ids] + [pltpu.SemaphoreType.DMA] * 2,
        compiler_params=pltpu.CompilerParams(use_tc_tiling_on_sc=True), name="moe_gather_sc")
    def k(table_hbm, idx_hbm, out_hbm, rows_a, rows_b, ia, ib, s0, s1):
        base = _sc_worker_base(per_w)

        @pl.loop(0, per_w // (2 * SC_ROWS))
        def _(j):
            off_a = pl.multiple_of(base + 2 * j * SC_ROWS, SC_ROWS)
            off_b = pl.multiple_of(off_a + SC_ROWS, SC_ROWS)
            pltpu.sync_copy(idx_hbm.at[pl.ds(off_a, SC_ROWS)], ia)
            pltpu.sync_copy(idx_hbm.at[pl.ds(off_b, SC_ROWS)], ib)
            get_a = pltpu.async_copy(table_hbm.at[ia], rows_a, s0)
            get_b = pltpu.async_copy(table_hbm.at[ib], rows_b, s1)
            get_a.wait()
            put_a = pltpu.async_copy(rows_a, out_hbm.at[pl.ds(off_a, SC_ROWS)], s0)
            get_b.wait()
            put_b = pltpu.async_copy(rows_b, out_hbm.at[pl.ds(off_b, SC_ROWS)], s1)
            put_a.wait()
            put_b.wait()

    return k(table, idx)


def _expert_kernel(blk_ref, nused_ref, first_ref, slot_ref, nxt_ref, xs_ref, wg_hbm, wu_hbm, wd_hbm, y_ref,
                   wg_f, wu_f, wd_f, wgb, wub, wdb, sem, *, layer):
    b = pl.program_id(0)

    def fetch(e, s):
        return [pltpu.make_async_copy(w_hbm.at[layer, e], stage.at[s], sem.at[s, k])
                for k, (w_hbm, stage) in enumerate(((wg_hbm, wg_f), (wu_hbm, wu_f), (wd_hbm, wd_f)))]

    @pl.when(b == 0)
    def _():
        for cp in fetch(blk_ref[0], 0):
            cp.start()

    @pl.when(first_ref[b] == 1)
    def _():
        s = slot_ref[b]
        for cp in fetch(blk_ref[b], s):
            cp.wait()

        @pl.when(nxt_ref[b] >= 0)
        def _():
            for cp in fetch(nxt_ref[b], 1 - s):
                cp.start()

        wgb[...] = wg_f[s].astype(BF16)
        wub[...] = wu_f[s].astype(BF16)
        wdb[...] = wd_f[s].astype(BF16)

    @pl.when(b < nused_ref[0])
    def _():
        xb = _unpack_pair(xs_ref[...]).astype(BF16)
        g = _dot(xb, wgb[...])
        u = _dot(xb, wub[...])
        a = (_silu(g) * u).astype(BF16)
        y_ref[...] = _pack_pair(_dot(a, wdb[...]))

    @pl.when(b >= nused_ref[0])
    def _():
        y_ref[...] = jnp.zeros_like(y_ref)


def _experts(xs, blk_e, nused, w_gate, w_up, w_down, layer):
    np_rows, dp = xs.shape
    _, ne, d, ff = w_gate.shape
    nb = np_rows // MOE_BLOCK
    pos = jnp.arange(nb, dtype=I32)
    first = jnp.concatenate([jnp.ones((1,), I32), (blk_e[1:] != blk_e[:-1]).astype(I32)])
    slot = (jnp.cumsum(first) - 1) % 2
    later = lax.cummin(jnp.where(first == 1, pos, nb)[::-1])[::-1]
    nxt_pos = jnp.concatenate([later[1:], jnp.full((1,), nb, I32)])
    nxt_e = jnp.where(nxt_pos < nb, blk_e[jnp.minimum(nxt_pos, nb - 1)], -1).astype(I32)
    grid_spec = pltpu.PrefetchScalarGridSpec(
        num_scalar_prefetch=5,
        grid=(nb,),
        in_specs=[pl.BlockSpec((MOE_BLOCK, dp), lambda b, *_: (b, 0)),
                  pl.BlockSpec(memory_space=pl.ANY),
                  pl.BlockSpec(memory_space=pl.ANY),
                  pl.BlockSpec(memory_space=pl.ANY)],
        out_specs=pl.BlockSpec((MOE_BLOCK, dp), lambda b, *_: (b, 0)),
        scratch_shapes=[pltpu.VMEM((2, d, ff), F32), pltpu.VMEM((2, d, ff), F32), pltpu.VMEM((2, ff, d), F32),
                        pltpu.VMEM((d, ff), BF16), pltpu.VMEM((d, ff), BF16), pltpu.VMEM((ff, d), BF16),
                        pltpu.SemaphoreType.DMA((2, 3))],
    )
    return pl.pallas_call(
        functools.partial(_expert_kernel, layer=layer),
        out_shape=jax.ShapeDtypeStruct((np_rows, dp), PACKED),
        grid_spec=grid_spec,
        compiler_params=_cparams(("arbitrary",)),
        name="moe_experts",
    )(blk_e, nused, first, slot.astype(I32), nxt_e, xs, w_gate, w_up, w_down)


def _moe_residual(x1, y1_ref, y2_ref, gates, out_gate):
    tc, d = x1.shape
    acc = jnp.zeros((tc, d), F32)
    for k, y_ref in enumerate((y1_ref, y2_ref)):
        gcol = jnp.transpose(jnp.broadcast_to(gates[k:k + 1, :], (LANES, tc)))
        acc = acc + _unpack_pair(y_ref[...]) * jnp.concatenate([gcol] * (d // LANES), axis=1)
    return x1 + out_gate * acc


def _combine_kernel(x1_ref, y1_ref, y2_ref, gate_ref, mod_ref, fg_ref, o_ref, *, final):
    x2 = _moe_residual(x1_ref[...], y1_ref, y2_ref, gate_ref[0], mod_ref[0][5:6, :])
    if final:
        x2 = _rms(x2) * fg_ref[...]
    o_ref[...] = x2


def _combine(x1, y12, gates, mod_l, final_gain, tc, seq, final):
    t, d = x1.shape
    nt = t // tc
    tiles_per_b = seq // tc
    return pl.pallas_call(
        functools.partial(_combine_kernel, final=final),
        out_shape=jax.ShapeDtypeStruct((t, d), F32),
        grid=(nt,),
        in_specs=[pl.BlockSpec((tc, d), lambda i: (i, 0)),
                  pl.BlockSpec((tc, _packed_width(d)), lambda i: (i, 0)),
                  pl.BlockSpec((tc, _packed_width(d)), lambda i: (nt + i, 0)),
                  pl.BlockSpec((1, TOP_K, tc), lambda i: (i, 0, 0)),
                  pl.BlockSpec((1, 6, d), lambda i: (i // tiles_per_b, 0, 0)),
                  pl.BlockSpec((1, d), lambda i: (0, 0))],
        out_specs=pl.BlockSpec((tc, d), lambda i: (i, 0)),
        compiler_params=_cparams(("arbitrary",)),
        name="moe_combine",
    )(x1, y12, y12, gates, mod_l, final_gain)


ROUTE_TILE = 512


def _moe(x1, h2, rw_t, rb_b, w_gate, w_up, w_down, layer):
    t, d = x1.shape
    ne = w_gate.shape[1]
    nb = (t * TOP_K + ne * (MOE_BLOCK - 1) + MOE_BLOCK - 1) // MOE_BLOCK
    nbpad = -(-nb // LANES) * LANES
    tm = ROUTE_TILE
    dest, gates, blk = _route(h2, rw_t, rb_b, tm, nbpad)
    slot = jnp.transpose(dest, (1, 0, 2)).reshape(TOP_K * t)
    xs = _sc_scatter_rows(h2, slot, nb * MOE_BLOCK)
    ys = _experts(xs, blk[0, :nb], blk[1, :1], w_gate, w_up, w_down, layer)
    return _sc_gather_rows(ys, slot), gates


def kernel(x, c, positions, mod_w, mod_b, norm_mix, norm_ffn, ev_w_in, ev_ret_norm, ev_gla_wa2, ev_gla_ba,
           ev_gla_norm, ev_w_out, od_w_in, od_conv_w, od_conv_b, od_dt_bias, od_a_log, od_d, od_norm, od_w_out,
           router_w, router_b, exp_w_gate, exp_w_up, exp_w_down, final_norm):
    bsz, seq, d = x.shape
    depth = mod_w.shape[0]
    t = bsz * seq
    ne = router_w.shape[1]
    epg = ne // N_GROUPS
    mod = _modulation(c, mod_w, mod_b)
    x2d = x.reshape(t, d)

    perm = np.array([g * epg + j for j in range(epg) for g in range(N_GROUPS)])
    rw_t = router_w.T[perm].astype(BF16)
    rb_b = jnp.broadcast_to(router_b.astype(F32)[perm][:, None], (ne, LANES))
    final_gain = final_norm.reshape(1, d)

    half = RET_DK // 2
    inv = ROPE_BASE ** (-jnp.arange(half, dtype=F32) / half)
    inv2 = jnp.concatenate([inv, inv]).reshape(1, LANES)
    pos_col = positions.reshape(t, 1)

    pending = None
    for layer in range(depth):
        i = layer // 2
        mod_l = mod[layer]
        gain1 = norm_mix[layer].reshape(1, d)
        gain2 = norm_ffn[layer].reshape(1, d)
        if layer % 2 == 0:
            if pending is not None:
                x2d = _combine(*pending, final_gain, ROUTE_TILE, seq, final=False)
            w_in = ev_w_in[i]
            nmain = w_in.shape[1] - GLA_RANK
            w_main = w_in[:, :nmain].astype(BF16)
            w_ga = jnp.pad(w_in[:, nmain:], ((0, 0), (0, LANES - GLA_RANK))).astype(BF16)
            wa2 = jnp.pad(ev_gla_wa2[i], ((0, LANES - GLA_RANK), (0, 0))).astype(BF16)
            y, glog = _even_proj(x2d, mod_l, gain1, pos_col, inv2, w_main, w_ga, wa2,
                                 ev_gla_ba[i].reshape(1, -1), PROJ_TILE, seq)
            o = _ret_gla(y, glog, ev_ret_norm[i].reshape(1, -1), ev_gla_norm[i].reshape(1, -1), bsz, seq, 512)
            w_out = ev_w_out[i].astype(BF16)
        else:
            assert pending is not None
            w_in = od_w_in[i]
            nheads = od_a_log.shape[1]
            nmain = w_in.shape[1] - nheads
            dinner = od_norm.shape[1]
            gw, ns_ = dinner // SSD_GROUPS, SSD_STATE
            bc0 = dinner + SSD_GROUPS * ns_

            def slabs(a, off=0):
                return [a[..., off + lo:off + lo + n] for g in range(SSD_GROUPS)
                        for lo, n in ((g * gw, gw), (dinner + g * ns_, ns_), (bc0 + g * ns_, ns_))]

            conv_w = jnp.concatenate(slabs(od_conv_w[i]), axis=-1)
            conv_b = jnp.concatenate(slabs(od_conv_b[i]), axis=-1).reshape(1, -1)
            w_main = jnp.concatenate(slabs(w_in, dinner) + [w_in[:, :dinner]], axis=1).astype(BF16)
            w_dt = jnp.pad(w_in[:, nmain:], ((0, 0), (0, LANES - nheads))).astype(BF16)
            dtb = jnp.pad(od_dt_bias[i], (0, LANES - nheads)).reshape(1, LANES)
            y, dt, x2d = _odd_proj(*pending, mod_l, gain1, w_main, w_dt, dtb, od_conv_w.shape[2], ROUTE_TILE, seq)
            alog = jnp.pad(od_a_log[i], (0, LANES - nheads)).reshape(1, LANES)
            d_row = jnp.repeat(od_d[i], SSD_HEADDIM).reshape(1, dinner)
            o = _ssd(y, dt, conv_w, conv_b, alog, d_row, od_norm[i].reshape(1, dinner), bsz, seq)
            w_out = od_w_out[i].astype(BF16)
        x1, h2 = _outproj(o, x2d, mod_l, gain2, w_out, 512, seq)
        y12, gates = _moe(x1, h2, rw_t, rb_b, exp_w_gate, exp_w_up, exp_w_down, layer)
        pending = (x1, y12, gates, mod_l)
    out = _combine(*pending, final_gain, ROUTE_TILE, seq, final=True)
    return out.reshape(bsz, seq, d)
```

```python
import functools

import jax
import jax.numpy as jnp
import numpy as np
from jax import lax
from jax.experimental import pallas as pl
from jax.experimental.pallas import tpu as pltpu
from jax.experimental.pallas import tpu_sc as plsc

F32 = jnp.float32
BF16 = jnp.bfloat16
I32 = jnp.int32

RET_HEADS, RET_DK, RET_DV, RET_CHUNK = 4, 128, 256, 128
ROPE_BASE = 10000.0
GLA_HEADS, GLA_DK, GLA_DV, GLA_RANK, GLA_TAU, GLA_CHUNK = 4, 128, 256, 16, 16.0, 64
SSD_HEADDIM, SSD_GROUPS, SSD_STATE, SSD_CONV, SSD_CHUNK = 64, 8, 128, 4, 128
N_GROUPS, TOP_K, GROUP_SCORE_K, MOE_BLOCK = 8, 2, 2, 256
NORM_EPS = 1e-6

PROJ_TILE = 512
LANES = 128
VMEM_LIMIT = 56 * 1024 * 1024


def _cparams(sem, vmem=VMEM_LIMIT):
    return pltpu.CompilerParams(dimension_semantics=sem, vmem_limit_bytes=vmem)


def _dot(a, b):
    return jnp.dot(a, b, preferred_element_type=F32)


def _dot_nt(a, b):
    return lax.dot_general(a, b, (((1,), (1,)), ((), ())), preferred_element_type=F32)


def _dot_tn(a, b):
    return lax.dot_general(a, b, (((0,), (0,)), ((), ())), preferred_element_type=F32)


def _split3(x):
    a = x.astype(BF16)
    r = x - a.astype(F32)
    b = r.astype(BF16)
    c = (r - b.astype(F32)).astype(BF16)
    return a, b, c


def _dot_exact_rhs(m01, x):
    a, b, c = _split3(x)
    return _dot(m01, a) + _dot(m01, b) + _dot(m01, c)


def _dot_exact_lhs(x, m01):
    a, b, c = _split3(x)
    return _dot(a, m01) + _dot(b, m01) + _dot(c, m01)


def _dot_tn_exact(x, m01):
    a, b, c = _split3(x)
    return _dot_tn(a, m01) + _dot_tn(b, m01) + _dot_tn(c, m01)


def _silu(x):
    return x * jax.nn.sigmoid(x)


def _rms(x, eps=NORM_EPS):
    return x * lax.rsqrt(jnp.mean(x * x, axis=-1, keepdims=True) + eps)


def _iota(shape, dim):
    return lax.broadcasted_iota(I32, shape, dim)


def _interleave(items, starts_per_round):
    waiting, running = list(items), []
    while waiting or running:
        running += [waiting.pop(0) for _ in range(min(starts_per_round, len(waiting)))]
        for item in reversed(list(running)):
            if next(item, StopIteration) is StopIteration:
                running.remove(item)


PACKED = jnp.uint32
_HI16 = 0xFFFF0000


def _packed_width(n):
    return n // 2


def _pack_pair(x):
    n = x.shape[1] // 2
    lo = lax.bitcast_convert_type(x[:, :n].astype(BF16).astype(F32), jnp.uint32)
    hi = lax.bitcast_convert_type(x[:, n:].astype(BF16).astype(F32), jnp.uint32)
    return (hi & jnp.uint32(_HI16)) | (lo >> 16)


def _unpack_pair(w):
    lo = lax.bitcast_convert_type(w << 16, F32)
    hi = lax.bitcast_convert_type(w & jnp.uint32(_HI16), F32)
    return jnp.concatenate([lo, hi], axis=1)


def _mod_kernel(c_ref, w_ref, b_ref, o_ref):
    sc = _silu(c_ref[...])
    o_ref[0] = _dot(sc.astype(BF16), w_ref[0].astype(BF16)) + b_ref[0]


def _modulation(c, mod_w, mod_b):
    depth, d, d6 = mod_w.shape
    bsz = c.shape[0]
    nb = d6 // d
    out = pl.pallas_call(
        _mod_kernel,
        out_shape=jax.ShapeDtypeStruct((depth, bsz, d6), F32),
        grid=(depth, nb),
        in_specs=[pl.BlockSpec((bsz, d), lambda l, j: (0, 0)),
                  pl.BlockSpec((1, d, d), lambda l, j: (l, 0, j)),
                  pl.BlockSpec((1, 1, d), lambda l, j: (l, 0, j))],
        out_specs=pl.BlockSpec((1, bsz, d), lambda l, j: (l, 0, j)),
        compiler_params=_cparams(("arbitrary", "arbitrary")),
        name="adaln_mod",
    )(c, mod_w, mod_b.reshape(depth, 1, d6))
    return out.reshape(depth, bsz, nb, d)


def _norm_mod(x, gain, mod, shift_row, scale_row):
    y = _rms(x) * gain
    return y * (1.0 + mod[scale_row:scale_row + 1, :]) + mod[shift_row:shift_row + 1, :]


def _even_proj_kernel(x_ref, mod_ref, gain_ref, pos_ref, inv_ref, w_ref, wga_ref, wa2_ref, ba_ref,
                      y_ref, glog_ref, *, ncol):
    h = _norm_mod(x_ref[...], gain_ref[...], mod_ref[0], 0, 1)
    hb = h.astype(BF16)
    tm = hb.shape[0]
    hm = tm // 2
    half = LANES // 2
    lo = _iota((hm, LANES), 1) < half
    ang = jnp.where(lo, pos_ref[0:hm, :], pos_ref[hm:tm, :]).astype(F32) * inv_ref[...]
    c2, s2 = jnp.cos(ang), jnp.sin(ang)
    c2r, s2r = pltpu.roll(c2, half, 1), pltpu.roll(s2, half, 1)
    cos = jnp.concatenate([jnp.where(lo, c2, c2r), jnp.where(lo, c2r, c2)], axis=0)
    sin_a, sin_b = jnp.where(lo, s2, s2r), jnp.where(lo, s2r, s2)
    sin = jnp.concatenate([jnp.where(lo, -sin_a, sin_a), jnp.where(lo, -sin_b, sin_b)], axis=0)
    qk_scale = RET_DK ** -0.5
    rqk = RET_HEADS * RET_DK
    rv = RET_HEADS * RET_DV
    gq0 = 2 * rqk + 2 * rv
    width = w_ref.shape[1] // ncol
    for j in range(ncol):
        c0 = j * width
        acc = _dot(hb, w_ref[:, c0:c0 + width])
        for s in range(width // LANES):
            col = c0 + s * LANES
            blk = acc[:, s * LANES:(s + 1) * LANES]
            if col < 2 * rqk:
                blk = blk * cos + pltpu.roll(blk, LANES // 2, 1) * sin
                if col >= rqk:
                    blk = blk * qk_scale
            elif gq0 <= col < gq0 + GLA_HEADS * GLA_DK:
                blk = blk * (GLA_DK ** -0.5)
            y_ref[:, col:col + LANES] = blk.astype(y_ref.dtype)
    ga = _dot(hb, wga_ref[...])
    z = _dot(ga.astype(BF16), wa2_ref[...]) + ba_ref[...]
    logsig = jnp.minimum(z, 0.0) - jnp.log1p(jnp.exp(-jnp.abs(z)))
    glog_ref[...] = logsig * (1.0 / GLA_TAU)


def _even_proj(x2d, mod_l, gain, pos_col, inv2, w_main, w_ga, wa2, ba, tm, seq):
    t, d = x2d.shape
    n = w_main.shape[1]
    tiles_per_b = seq // tm
    gk = wa2.shape[1]
    const = lambda i: (0, 0)
    return pl.pallas_call(
        functools.partial(_even_proj_kernel, ncol=n // 512),
        out_shape=(jax.ShapeDtypeStruct((t, n), BF16), jax.ShapeDtypeStruct((t, gk), F32)),
        grid=(t // tm,),
        in_specs=[pl.BlockSpec((tm, d), lambda i: (i, 0)),
                  pl.BlockSpec((1, 6, d), lambda i: (i // tiles_per_b, 0, 0)),
                  pl.BlockSpec((1, d), const),
                  pl.BlockSpec((tm, 1), lambda i: (i, 0)),
                  pl.BlockSpec((1, LANES), const),
                  pl.BlockSpec((d, n), const, pipeline_mode=pl.Buffered(1)),
                  pl.BlockSpec((d, LANES), const),
                  pl.BlockSpec((LANES, gk), const),
                  pl.BlockSpec((1, gk), const)],
        out_specs=(pl.BlockSpec((tm, n), lambda i: (i, 0)), pl.BlockSpec((tm, gk), lambda i: (i, 0))),
        compiler_params=_cparams(("arbitrary",)),
        name="even_proj",
    )(x2d, mod_l, gain, pos_col, inv2, w_main, w_ga, wa2, ba)


RET_GLA_UNROLL = 4
RET_GLA_STARTS = 4


def _ret_gla_kernel(rq_ref, rk_ref, rv_ref, rg_ref, gq_ref, gk_ref, gv_ref, gg_ref, gl_ref,
                    rn_ref, gn_ref, dec_ref, zeta_ref, xi_ref, o_ref, rst_ref, gst_ref, *, chunk_decay):
    @pl.when(pl.program_id(1) == 0)
    def _():
        rst_ref[...] = jnp.zeros_like(rst_ref)
        gst_ref[...] = jnp.zeros_like(gst_ref)

    ls = rq_ref.shape[0]
    lr, lg = RET_CHUNK, GLA_CHUNK
    trow = _iota((lg, lg), 0)
    tcol = _iota((lg, lg), 1)
    causal = trow >= tcol
    tri = causal.astype(BF16)
    base = RET_HEADS * RET_DV

    def ret_item(r0, h):
        q = rq_ref[pl.ds(r0, lr), h * RET_DK:(h + 1) * RET_DK]
        k = rk_ref[pl.ds(r0, lr), h * RET_DK:(h + 1) * RET_DK]
        v = rv_ref[pl.ds(r0, lr), h * RET_DV:(h + 1) * RET_DV]
        s = _dot_nt(q, k) * dec_ref[h]
        qx = (q.astype(F32) * xi_ref[h]).astype(BF16)
        kz = (k.astype(F32) * zeta_ref[h]).astype(BF16)
        yield
        inner = _dot(s.astype(BF16), v)
        kv = _dot_tn(kz, v)
        yield
        gate = _silu(rg_ref[pl.ds(r0, lr), h * RET_DV:(h + 1) * RET_DV].astype(F32))
        yield
        st = rst_ref[h]
        cross = _dot(qx, st.astype(BF16))
        rst_ref[h] = st * chunk_decay[h] + kv
        o = _rms(inner + cross) * rn_ref[:, h * RET_DV:(h + 1) * RET_DV] * gate
        o_ref[pl.ds(r0, lr), h * RET_DV:(h + 1) * RET_DV] = o.astype(o_ref.dtype)

    def gla_item(r0, h):
        gl = gl_ref[pl.ds(r0, lg), h * GLA_DK:(h + 1) * GLA_DK]
        b = _dot_exact_rhs(tri, gl)
        yield
        q = gq_ref[pl.ds(r0, lg), h * GLA_DK:(h + 1) * GLA_DK].astype(F32)
        k = gk_ref[pl.ds(r0, lg), h * GLA_DK:(h + 1) * GLA_DK].astype(F32)
        bref = b[lg // 2:lg // 2 + 1, :]
        blast = b[lg - 1:lg, :]
        att = _dot_nt((q * jnp.exp(b - bref)).astype(BF16), (k * jnp.exp(bref - b)).astype(BF16))
        qb = (q * jnp.exp(b)).astype(BF16)
        kd = (k * jnp.exp(blast - b)).astype(BF16)
        dec = jnp.exp(blast)
        yield
        v = gv_ref[pl.ds(r0, lg), h * GLA_DV:(h + 1) * GLA_DV]
        inner = _dot(jnp.where(causal, att, 0.0).astype(BF16), v)
        kvt = _dot_tn(v, kd)
        gate = _silu(gg_ref[pl.ds(r0, lg), h * GLA_DV:(h + 1) * GLA_DV].astype(F32))
        yield
        st = gst_ref[h]
        cross = _dot_nt(qb, st.astype(BF16))
        gst_ref[h] = st * dec + kvt
        o = _rms(inner + cross) * gn_ref[:, h * GLA_DV:(h + 1) * GLA_DV] * gate
        o_ref[pl.ds(r0, lg), base + h * GLA_DV:base + (h + 1) * GLA_DV] = o.astype(o_ref.dtype)

    def step(c, carry):
        waiting = []
        for u in range(RET_GLA_UNROLL):
            r0 = pl.multiple_of((c * RET_GLA_UNROLL + u) * lr, lr)
            waiting += [ret_item(r0, h) for h in range(RET_HEADS)]
            for j in range(lr // lg):
                rj = pl.multiple_of(r0 + j * lg, lg)
                waiting += [gla_item(rj, h) for h in range(GLA_HEADS)]
        assert RET_HEADS % RET_GLA_STARTS == 0 and GLA_HEADS % RET_GLA_STARTS == 0
        _interleave(waiting, RET_GLA_STARTS)
        return carry

    lax.fori_loop(0, ls // (lr * RET_GLA_UNROLL), step, 0)


def _ret_gla(y, glog, ret_norm, gla_norm, bsz, seq, ls):
    t = y.shape[0]
    ns = seq // ls
    qw, vw = RET_HEADS * RET_DK, RET_HEADS * RET_DV
    L = RET_CHUNK
    log_gamma = jnp.log1p(-jnp.exp2(-5.0 - jnp.arange(RET_HEADS, dtype=F32)))
    idx = jnp.arange(L, dtype=F32)
    diff = idx[:, None] - idx[None, :]
    decay = jnp.where(diff >= 0, jnp.exp(log_gamma[:, None, None] * jnp.maximum(diff, 0.0)), 0.0)
    zeta = jnp.broadcast_to(jnp.exp(log_gamma[:, None] * (L - 1 - idx)[None, :])[:, :, None], (RET_HEADS, L, RET_DK))
    xi = jnp.broadcast_to(jnp.exp(log_gamma[:, None] * (idx + 1.0)[None, :])[:, :, None], (RET_HEADS, L, RET_DK))
    chunk_decay = tuple(float(np.exp(np.float32(np.log1p(-np.exp2(np.float32(-5.0 - i)))) * np.float32(L)))
                        for i in range(RET_HEADS))
    rowmap = lambda j: (lambda b, i: (b * ns + i, j))
    const = lambda b, i: (0, 0)
    const3 = lambda b, i: (0, 0, 0)
    return pl.pallas_call(
        functools.partial(_ret_gla_kernel, chunk_decay=chunk_decay),
        out_shape=jax.ShapeDtypeStruct((t, 2 * vw), BF16),
        grid=(bsz, ns),
        in_specs=[pl.BlockSpec((ls, qw), rowmap(0)),
                  pl.BlockSpec((ls, qw), rowmap(1)),
                  pl.BlockSpec((ls, vw), rowmap(1)),
                  pl.BlockSpec((ls, vw), rowmap(2)),
                  pl.BlockSpec((ls, qw), rowmap(6)),
                  pl.BlockSpec((ls, qw), rowmap(7)),
                  pl.BlockSpec((ls, vw), rowmap(4)),
                  pl.BlockSpec((ls, vw), rowmap(5)),
                  pl.BlockSpec((ls, qw), rowmap(0)),
                  pl.BlockSpec((1, vw), const),
                  pl.BlockSpec((1, vw), const),
                  pl.BlockSpec((RET_HEADS, L, L), const3),
                  pl.BlockSpec((RET_HEADS, L, RET_DK), const3),
                  pl.BlockSpec((RET_HEADS, L, RET_DK), const3)],
        out_specs=pl.BlockSpec((ls, 2 * vw), lambda b, i: (b * ns + i, 0)),
        scratch_shapes=[pltpu.VMEM((RET_HEADS, RET_DK, RET_DV), F32),
                        pltpu.VMEM((GLA_HEADS, GLA_DV, GLA_DK), F32)],
        compiler_params=_cparams(("arbitrary", "arbitrary")),
        name="ret_gla_mixer",
    )(y, y, y, y, y, y, y, y, glog, ret_norm, gla_norm, decay, zeta, xi)


def _odd_proj_kernel(x1_ref, y1_ref, y2_ref, gate_ref, modp_ref, mod_ref, gain_ref, w_ref, wdt_ref, dtb_ref,
                     y_ref, dt_ref, x_ref, *, ncol, cch):
    x = _moe_residual(x1_ref[...], y1_ref, y2_ref, gate_ref[0], modp_ref[0][5:6, :])
    x_ref[...] = x
    h = _norm_mod(x, gain_ref[...], mod_ref[0], 0, 1)
    hb = h.astype(BF16)
    width = w_ref.shape[1] // ncol
    for j in range(ncol):
        c0 = j * width
        acc = _dot(hb, w_ref[:, c0:c0 + width])
        if c0 >= cch:
            acc = _silu(acc)
        y_ref[:, c0:c0 + width] = acc.astype(y_ref.dtype)
    z = _dot(hb, wdt_ref[...]) + dtb_ref[...]
    dt_ref[...] = jnp.maximum(z, 0.0) + jnp.log1p(jnp.exp(-jnp.abs(z)))


def _odd_proj(x1, y12, gates, mod_prev, mod_l, gain, w_main, w_dt, dt_bias, cch, tm, seq):
    t, d = x1.shape
    n = w_main.shape[1]
    nt = t // tm
    tiles_per_b = seq // tm
    const = lambda i: (0, 0)
    batch = lambda i: (i // tiles_per_b, 0, 0)
    return pl.pallas_call(
        functools.partial(_odd_proj_kernel, ncol=n // 512, cch=cch),
        out_shape=(jax.ShapeDtypeStruct((t, n), BF16), jax.ShapeDtypeStruct((t, LANES), F32),
                   jax.ShapeDtypeStruct((t, d), F32)),
        grid=(nt,),
        in_specs=[pl.BlockSpec((tm, d), lambda i: (i, 0)),
                  pl.BlockSpec((tm, _packed_width(d)), lambda i: (i, 0)),
                  pl.BlockSpec((tm, _packed_width(d)), lambda i: (nt + i, 0)),
                  pl.BlockSpec((1, TOP_K, tm), lambda i: (i, 0, 0)),
                  pl.BlockSpec((1, 6, d), batch),
                  pl.BlockSpec((1, 6, d), batch),
                  pl.BlockSpec((1, d), const),
                  pl.BlockSpec((d, n), const, pipeline_mode=pl.Buffered(1)),
                  pl.BlockSpec((d, LANES), const),
                  pl.BlockSpec((1, LANES), const)],
        out_specs=(pl.BlockSpec((tm, n), lambda i: (i, 0)), pl.BlockSpec((tm, LANES), lambda i: (i, 0)),
                   pl.BlockSpec((tm, d), lambda i: (i, 0))),
        compiler_params=_cparams(("arbitrary",)),
        name="odd_proj",
    )(x1, y12, y12, gates, mod_prev, mod_l, gain, w_main, w_dt, dt_bias)


def _ssd_kernel(z_ref, xbc_ref, dt_ref, cw_ref, cb_ref, alog_ref, dsk_ref, ng_ref, o_ref, hist_ref, st_ref):
    @pl.when(pl.program_id(1) == 0)
    def _():
        hist_ref[...] = jnp.zeros_like(hist_ref)
        st_ref[...] = jnp.zeros_like(st_ref)

    L = xbc_ref.shape[0]
    P, N, G = SSD_HEADDIM, SSD_STATE, SSD_GROUPS
    dinner = z_ref.shape[1]
    gw = dinner // G
    hpg = gw // P
    slab = gw + 2 * N
    hrows = hist_ref.shape[0]

    dt = dt_ref[...]
    da = dt * (-jnp.exp(alog_ref[...]))
    trow = _iota((L, L), 0)
    tcol = _iota((L, L), 1)
    causal = trow >= tcol
    cum = _dot_exact_rhs(causal.astype(BF16), da)
    cum_t = jnp.transpose(cum)
    dt_t = jnp.transpose(dt)
    ecum = jnp.exp(cum)
    wend = dt * jnp.exp(cum[L - 1:L, :] - cum)
    lane = _iota((L, LANES), 1)
    lo_half = lane < P

    def pair_cols(a, ha):
        return jnp.where(lo_half, jnp.broadcast_to(a[:, ha:ha + 1], (L, LANES)),
                         jnp.broadcast_to(a[:, ha + 1:ha + 2], (L, LANES)))

    srow = _iota((L, hrows + L), 0)
    scol = _iota((L, hrows + L), 1)
    shift = jnp.concatenate([(scol == srow + (hrows - s)).astype(BF16) for s in range(1, SSD_CONV)], axis=0)

    def group_item(g):
        c0 = g * slab
        xin = xbc_ref[:, c0:c0 + slab]
        xcat = jnp.concatenate([hist_ref[:, c0:c0 + slab], xin], axis=0)
        hist_ref[:, c0:c0 + slab] = xin[L - hrows:L]
        sh = _dot(shift, xcat)
        conv = xin.astype(F32) * cw_ref[SSD_CONV - 1:SSD_CONV, c0:c0 + slab] + cb_ref[:, c0:c0 + slab]
        for s in range(1, SSD_CONV):
            conv = conv + sh[(s - 1) * L:s * L] * cw_ref[SSD_CONV - 1 - s:SSD_CONV - s, c0:c0 + slab]
        act = _silu(conv)
        xs = act[:, :gw]
        bm_g = act[:, gw:gw + N].astype(BF16)
        cm_g = act[:, gw + N:].astype(BF16)
        yield
        cb = _dot_nt(cm_g, bm_g)
        lhs, xw, ec = [], [], []
        for pr in range(hpg // 2):
            ha = g * hpg + 2 * pr
            ms = []
            for hh in (ha, ha + 1):
                seg = jnp.broadcast_to(cum[:, hh:hh + 1], (L, L)) - cum_t[hh:hh + 1, :]
                ms.append((cb * jnp.where(causal, jnp.exp(seg), 0.0) * dt_t[hh:hh + 1, :]).astype(BF16))
            lhs.append(jnp.concatenate(ms, axis=1))
            xw.append((xs[:, pr * 2 * P:(pr + 1) * 2 * P] * pair_cols(wend, ha)).astype(BF16))
            ec.append(pair_cols(ecum, ha))
        ecum_g = jnp.concatenate(ec, axis=1)
        yield
        xs_b = xs.astype(BF16)
        pieces = []
        for pr in range(hpg // 2):
            blk = xs_b[:, pr * 2 * P:(pr + 1) * 2 * P]
            zero = jnp.zeros_like(blk)
            rhs = jnp.concatenate([jnp.where(lo_half, blk, zero), jnp.where(lo_half, zero, blk)], axis=0)
            pieces.append(_dot(lhs[pr], rhs))
        st = st_ref[g]
        cross = _dot(cm_g, st.astype(BF16)) * ecum_g
        st_ref[g] = st * ecum_g[L - 1:L, :] + _dot_tn(bm_g, jnp.concatenate(xw, axis=1))
        y = jnp.concatenate(pieces, axis=1) + cross + dsk_ref[:, g * gw:(g + 1) * gw] * xs
        y = y * z_ref[:, g * gw:(g + 1) * gw].astype(F32)
        o_ref[:, g * gw:(g + 1) * gw] = (_rms(y) * ng_ref[:, g * gw:(g + 1) * gw]).astype(o_ref.dtype)

    _interleave([group_item(g) for g in range(G)], SSD_STARTS)


SSD_STARTS = 1
SSD_HIST_ROWS = 16


def _ssd(y, dt, conv_w, conv_b, a_log_row, d_row, norm_g, bsz, seq):
    t = y.shape[0]
    L = SSD_CHUNK
    ns = seq // L
    dinner = norm_g.shape[1]
    cch = conv_w.shape[1]
    gw = dinner // SSD_GROUPS
    const = lambda b, i: (0, 0)
    assert cch % dinner == 0
    return pl.pallas_call(
        _ssd_kernel,
        out_shape=jax.ShapeDtypeStruct((t, dinner), BF16),
        grid=(bsz, ns),
        in_specs=[pl.BlockSpec((L, dinner), lambda b, i: (b * ns + i, cch // dinner)),
                  pl.BlockSpec((L, cch), lambda b, i: (b * ns + i, 0)),
                  pl.BlockSpec((L, LANES), lambda b, i: (b * ns + i, 0)),
                  pl.BlockSpec((SSD_CONV, cch), const),
                  pl.BlockSpec((1, cch), const),
                  pl.BlockSpec((1, LANES), const),
                  pl.BlockSpec((1, dinner), const),
                  pl.BlockSpec((1, dinner), const)],
        out_specs=pl.BlockSpec((L, dinner), lambda b, i: (b * ns + i, 0)),
        scratch_shapes=[pltpu.VMEM((SSD_HIST_ROWS, cch), BF16),
                        pltpu.VMEM((SSD_GROUPS, SSD_STATE, gw), F32)],
        compiler_params=_cparams(("arbitrary", "arbitrary")),
        name="ssd_mixer",
    )(y, y, dt, conv_w, conv_b, a_log_row, d_row, norm_g)


OUT_SLAB = 256


def _route_choices(hb, rw, rb):
    tm = hb.shape[0]
    ne = rw.shape[0]
    epg = ne // N_GROUPS
    logits = _dot_nt(rw, hb)
    s = jax.nn.sigmoid(logits)
    sb = s + jnp.concatenate([rb] * (tm // LANES), axis=1)
    G = N_GROUPS
    v = [sb[j * G:(j + 1) * G, :] for j in range(epg)]
    m01, n01 = jnp.maximum(v[0], v[1]), jnp.minimum(v[0], v[1])
    m23, n23 = jnp.maximum(v[2], v[3]), jnp.minimum(v[2], v[3])
    top1 = jnp.maximum(m01, m23)
    top2 = jnp.maximum(jnp.minimum(m01, m23), jnp.maximum(n01, n23))
    gscore = top1 + top2
    gi = _iota((G, tm), 0)
    gmax = jnp.max(gscore, axis=0, keepdims=True)
    gidx = jnp.min(jnp.where(gscore == gmax, gi, G), axis=0, keepdims=True)
    sel = gi == gidx
    vb = [jnp.sum(jnp.where(sel, v[j], 0.0), axis=0, keepdims=True) for j in range(epg)]
    vs = [jnp.sum(jnp.where(sel, s[j * G:(j + 1) * G, :], 0.0), axis=0, keepdims=True) for j in range(epg)]
    zero = jnp.zeros_like(vb[0])
    e1 = zero
    e2 = zero
    w1 = zero
    w2 = zero
    for j in range(epg):
        rank = zero
        for i in range(epg):
            if i == j:
                continue
            ahead = (vb[i] >= vb[j]) if i < j else (vb[i] > vb[j])
            rank = rank + jnp.where(ahead, 1.0, 0.0)
        first = rank == 0.0
        second = rank == 1.0
        e1 = e1 + jnp.where(first, float(j), 0.0)
        e2 = e2 + jnp.where(second, float(j), 0.0)
        w1 = w1 + jnp.where(first, vs[j], 0.0)
        w2 = w2 + jnp.where(second, vs[j], 0.0)
    gf = gidx.astype(F32) * float(epg)
    wsum = w1 + w2
    return jnp.concatenate([e1 + gf, e2 + gf, w1 / wsum, w2 / wsum], axis=0)


def _outproj_kernel(o_ref, x_ref, mod_ref, gain_ref, w_ref, rw_ref, rb_ref, x1_ref, h2_ref, sel_ref, cnt_ref):
    mod = mod_ref[0]
    ne = rw_ref.shape[0]

    @pl.when(pl.program_id(0) == 0)
    def _():
        cnt_ref[...] = jnp.zeros_like(cnt_ref)

    def row_slab(r0):
        rows = pl.ds(r0, OUT_SLAB)
        mix = _dot(o_ref[rows, :], w_ref[...])
        yield
        x1 = x_ref[rows, :] + mod[2:3, :] * mix
        x1_ref[rows, :] = x1
        h2 = _norm_mod(x1, gain_ref[...], mod, 3, 4)
        h2_ref[rows, :] = _pack_pair(h2)
        choice = _route_choices(h2.astype(BF16), rw_ref[...], rb_ref[...])
        sel_ref[0, :, rows] = choice
        ei = _iota((ne, OUT_SLAB), 0).astype(F32)
        onehot = jnp.where((ei == choice[0:1, :]) | (ei == choice[1:2, :]), 1.0, 0.0)
        cnt_ref[...] = cnt_ref[...] + jnp.sum(onehot, axis=1, keepdims=True)

    _interleave([row_slab(r0) for r0 in range(0, o_ref.shape[0], OUT_SLAB)], 1)


def _outproj(o, x2d, mod_l, gain2, w_out, rw_t, rb_b, tm, seq):
    t, d = x2d.shape
    kin = o.shape[1]
    ne = rw_t.shape[0]
    nt = t // tm
    tiles_per_b = seq // tm
    const = lambda i: (0, 0)
    return pl.pallas_call(
        _outproj_kernel,
        out_shape=(jax.ShapeDtypeStruct((t, d), F32), jax.ShapeDtypeStruct((t, _packed_width(d)), PACKED),
                   jax.ShapeDtypeStruct((nt, 4, tm), F32), jax.ShapeDtypeStruct((ne, LANES), F32)),
        grid=(nt,),
        in_specs=[pl.BlockSpec((tm, kin), lambda i: (i, 0)),
                  pl.BlockSpec((tm, d), lambda i: (i, 0)),
                  pl.BlockSpec((1, 6, d), lambda i: (i // tiles_per_b, 0, 0)),
                  pl.BlockSpec((1, d), const),
                  pl.BlockSpec((kin, d), const),
                  pl.BlockSpec((ne, d), const),
                  pl.BlockSpec((ne, LANES), const)],
        out_specs=(pl.BlockSpec((tm, d), lambda i: (i, 0)), pl.BlockSpec((tm, _packed_width(d)), lambda i: (i, 0)),
                   pl.BlockSpec((1, 4, tm), lambda i: (i, 0, 0)), pl.BlockSpec((ne, LANES), const)),
        compiler_params=_cparams(("arbitrary",)),
        name="out_proj",
    )(o, x2d, mod_l, gain2, w_out, rw_t, rb_b)


def _route_kernel(sel_ref, tot_ref, dest_ref, gate_ref, blk_ref, cnt_ref, pst_ref, *, nbpad):
    step = pl.program_id(0)
    tm = sel_ref.shape[2]
    ne = tot_ref.shape[0]
    ei = _iota((ne, tm), 0).astype(F32)

    @pl.when(step == 0)
    def _():
        nblk = jnp.floor((tot_ref[...] + (MOE_BLOCK - 1.0)) * (1.0 / MOE_BLOCK))
        hi = jnp.floor(nblk * (1.0 / 16.0))
        lo = nblk - hi * 16.0
        er = _iota((ne, ne), 0)
        ec = _iota((ne, ne), 1)
        lower = (ec < er).astype(BF16)
        pst = _dot(lower, hi.astype(BF16)) * 16.0 + _dot(lower, lo.astype(BF16))
        pst_ref[...] = pst
        pend = jnp.concatenate([pst + nblk] * (nbpad // LANES), axis=1)
        bidx = _iota((ne, nbpad), 1).astype(F32)
        be = jnp.sum(jnp.where(pend <= bidx, 1.0, 0.0), axis=0, keepdims=True)
        total = jnp.max(pend, axis=0, keepdims=True)
        blk_ref[0:1, :] = jnp.minimum(be, ne - 1.0).astype(I32)
        blk_ref[1:2, :] = total.astype(I32)
        cnt_ref[...] = jnp.zeros_like(cnt_ref)

    choice = sel_ref[0]
    oh1 = ei == choice[0:1, :]
    oh2 = ei == choice[1:2, :]
    onehot = jnp.where(oh1 | oh2, 1.0, 0.0)
    upper = (_iota((tm, tm), 0) < _iota((tm, tm), 1)).astype(BF16)
    prefix = _dot(onehot.astype(BF16), upper)
    cnt = cnt_ref[...]
    base = jnp.concatenate([cnt] * (tm // LANES), axis=1) + prefix
    pst = jnp.concatenate([pst_ref[...]] * (tm // LANES), axis=1) * float(MOE_BLOCK)
    slot = base + pst
    d1 = jnp.sum(jnp.where(oh1, slot, 0.0), axis=0, keepdims=True)
    d2 = jnp.sum(jnp.where(oh2, slot, 0.0), axis=0, keepdims=True)
    cnt_ref[...] = cnt + jnp.sum(onehot, axis=1, keepdims=True)
    dest_ref[0, 0:1, :] = d1.astype(I32)
    dest_ref[0, 1:2, :] = d2.astype(I32)
    gate_ref[0] = choice[2:4, :]


def _route(sel, totals, nbpad):
    nt, _, tm = sel.shape
    ne = totals.shape[0]
    return pl.pallas_call(
        functools.partial(_route_kernel, nbpad=nbpad),
        out_shape=(jax.ShapeDtypeStruct((nt, 2, tm), I32), jax.ShapeDtypeStruct((nt, 2, tm), F32),
                   jax.ShapeDtypeStruct((2, nbpad), I32)),
        grid=(nt,),
        in_specs=[pl.BlockSpec((1, 4, tm), lambda i: (i, 0, 0)),
                  pl.BlockSpec((ne, LANES), lambda i: (0, 0))],
        out_specs=(pl.BlockSpec((1, 2, tm), lambda i: (i, 0, 0)),
                   pl.BlockSpec((1, 2, tm), lambda i: (i, 0, 0)),
                   pl.BlockSpec((2, nbpad), lambda i: (0, 0))),
        scratch_shapes=[pltpu.VMEM((ne, LANES), F32), pltpu.VMEM((ne, LANES), F32)],
        compiler_params=_cparams(("arbitrary",)),
        name="router_slots",
    )(sel, totals)


SC_CORES, SC_SUBCORES = 2, 16
SC_ROWS = 64


def _sc_mesh():
    return plsc.VectorSubcoreMesh(core_axis_name="c", subcore_axis_name="s",
                                  num_cores=SC_CORES, num_subcores=SC_SUBCORES)


def _sc_worker_base(rows_per_worker):
    return (lax.axis_index("s") * SC_CORES + lax.axis_index("c")) * rows_per_worker


def _sc_scatter_rows(src, idx, n_out):
    t, d = src.shape
    assert idx.shape[0] == TOP_K * t
    per_w = t // (SC_CORES * SC_SUBCORES)
    assert per_w % (2 * SC_ROWS) == 0
    rows = pltpu.VMEM((SC_ROWS, d), src.dtype)
    ids = pltpu.VMEM((SC_ROWS,), I32)

    @functools.partial(
        pl.kernel, mesh=_sc_mesh(), out_type=jax.ShapeDtypeStruct((n_out, d), src.dtype),
        scratch_types=[rows, rows, ids, ids, ids, ids] + [pltpu.SemaphoreType.DMA] * 4,
        compiler_params=pltpu.CompilerParams(use_tc_tiling_on_sc=True), name="moe_dispatch_sc")
    def k(src_hbm, idx_hbm, out_hbm, rows_a, rows_b, ia0, ia1, ib0, ib1, s0, s1, s2, s3):
        base = _sc_worker_base(per_w)

        @pl.loop(0, per_w // (2 * SC_ROWS))
        def _(j):
            off_a = pl.multiple_of(base + 2 * j * SC_ROWS, SC_ROWS)
            off_b = pl.multiple_of(off_a + SC_ROWS, SC_ROWS)
            load_a = pltpu.async_copy(src_hbm.at[pl.ds(off_a, SC_ROWS)], rows_a, s0)
            load_b = pltpu.async_copy(src_hbm.at[pl.ds(off_b, SC_ROWS)], rows_b, s1)
            pltpu.sync_copy(idx_hbm.at[pl.ds(off_a, SC_ROWS)], ia0)
            pltpu.sync_copy(idx_hbm.at[pl.ds(t + off_a, SC_ROWS)], ia1)
            pltpu.sync_copy(idx_hbm.at[pl.ds(off_b, SC_ROWS)], ib0)
            pltpu.sync_copy(idx_hbm.at[pl.ds(t + off_b, SC_ROWS)], ib1)
            load_a.wait()
            put_a0 = pltpu.async_copy(rows_a, out_hbm.at[ia0], s0)
            put_a1 = pltpu.async_copy(rows_a, out_hbm.at[ia1], s2)
            load_b.wait()
            put_b0 = pltpu.async_copy(rows_b, out_hbm.at[ib0], s1)
            put_b1 = pltpu.async_copy(rows_b, out_hbm.at[ib1], s3)
            put_a0.wait()
            put_a1.wait()
            put_b0.wait()
            put_b1.wait()

    return k(src, idx)


def _sc_gather_rows(table, idx):
    _, d = table.shape
    b = idx.shape[0]
    per_w = b // (SC_CORES * SC_SUBCORES)
    assert per_w % (2 * SC_ROWS) == 0
    rows = pltpu.VMEM((SC_ROWS, d), table.dtype)
    ids = pltpu.VMEM((SC_ROWS,), I32)

    @functools.partial(
        pl.kernel, mesh=_sc_mesh(), out_type=jax.ShapeDtypeStruct((b, d), table.dtype),
        scratch_types=[rows, rows, ids, ids] + [pltpu.SemaphoreType.DMA] * 2,
        compiler_params=pltpu.CompilerParams(use_tc_tiling_on_sc=True), name="moe_gather_sc")
    def k(table_hbm, idx_hbm, out_hbm, rows_a, rows_b, ia, ib, s0, s1):
        base = _sc_worker_base(per_w)

        @pl.loop(0, per_w // (2 * SC_ROWS))
        def _(j):
            off_a = pl.multiple_of(base + 2 * j * SC_ROWS, SC_ROWS)
            off_b = pl.multiple_of(off_a + SC_ROWS, SC_ROWS)
            pltpu.sync_copy(idx_hbm.at[pl.ds(off_a, SC_ROWS)], ia)
            pltpu.sync_copy(idx_hbm.at[pl.ds(off_b, SC_ROWS)], ib)
            get_a = pltpu.async_copy(table_hbm.at[ia], rows_a, s0)
            get_b = pltpu.async_copy(table_hbm.at[ib], rows_b, s1)
            get_a.wait()
            put_a = pltpu.async_copy(rows_a, out_hbm.at[pl.ds(off_a, SC_ROWS)], s0)
            get_b.wait()
            put_b = pltpu.async_copy(rows_b, out_hbm.at[pl.ds(off_b, SC_ROWS)], s1)
            put_a.wait()
            put_b.wait()

    return k(table, idx)


def _expert_kernel(blk_ref, nused_ref, first_ref, slot_ref, nxt_ref, xs_ref, wg_hbm, wu_hbm, wd_hbm, y_ref,
                   wg_f, wu_f, wd_f, wgb, wub, wdb, sem, *, layer):
    b = pl.program_id(0)

    def fetch(e, s):
        return [pltpu.make_async_copy(w_hbm.at[layer, e], stage.at[s], sem.at[s, k])
                for k, (w_hbm, stage) in enumerate(((wg_hbm, wg_f), (wu_hbm, wu_f), (wd_hbm, wd_f)))]

    @pl.when(b == 0)
    def _():
        for cp in fetch(blk_ref[0], 0):
            cp.start()

    @pl.when(first_ref[b] == 1)
    def _():
        s = slot_ref[b]
        for cp in fetch(blk_ref[b], s):
            cp.wait()

        @pl.when(nxt_ref[b] >= 0)
        def _():
            for cp in fetch(nxt_ref[b], 1 - s):
                cp.start()

        wgb[...] = wg_f[s].astype(BF16)
        wub[...] = wu_f[s].astype(BF16)
        wdb[...] = wd_f[s].astype(BF16)

    @pl.when(b < nused_ref[0])
    def _():
        xb = _unpack_pair(xs_ref[...]).astype(BF16)
        g = _dot(xb, wgb[...])
        u = _dot(xb, wub[...])
        a = (_silu(g) * u).astype(BF16)
        y_ref[...] = _pack_pair(_dot(a, wdb[...]))

    @pl.when(b >= nused_ref[0])
    def _():
        y_ref[...] = jnp.zeros_like(y_ref)


def _experts(xs, blk_e, nused, w_gate, w_up, w_down, layer):
    np_rows, dp = xs.shape
    _, ne, d, ff = w_gate.shape
    nb = np_rows // MOE_BLOCK
    pos = jnp.arange(nb, dtype=I32)
    first = jnp.concatenate([jnp.ones((1,), I32), (blk_e[1:] != blk_e[:-1]).astype(I32)])
    slot = (jnp.cumsum(first) - 1) % 2
    later = lax.cummin(jnp.where(first == 1, pos, nb)[::-1])[::-1]
    nxt_pos = jnp.concatenate([later[1:], jnp.full((1,), nb, I32)])
    nxt_e = jnp.where(nxt_pos < nb, blk_e[jnp.minimum(nxt_pos, nb - 1)], -1).astype(I32)
    grid_spec = pltpu.PrefetchScalarGridSpec(
        num_scalar_prefetch=5,
        grid=(nb,),
        in_specs=[pl.BlockSpec((MOE_BLOCK, dp), lambda b, *_: (b, 0)),
                  pl.BlockSpec(memory_space=pl.ANY),
                  pl.BlockSpec(memory_space=pl.ANY),
                  pl.BlockSpec(memory_space=pl.ANY)],
        out_specs=pl.BlockSpec((MOE_BLOCK, dp), lambda b, *_: (b, 0)),
        scratch_shapes=[pltpu.VMEM((2, d, ff), F32), pltpu.VMEM((2, d, ff), F32), pltpu.VMEM((2, ff, d), F32),
                        pltpu.VMEM((d, ff), BF16), pltpu.VMEM((d, ff), BF16), pltpu.VMEM((ff, d), BF16),
                        pltpu.SemaphoreType.DMA((2, 3))],
    )
    return pl.pallas_call(
        functools.partial(_expert_kernel, layer=layer),
        out_shape=jax.ShapeDtypeStruct((np_rows, dp), PACKED),
        grid_spec=grid_spec,
        compiler_params=_cparams(("arbitrary",)),
        name="moe_experts",
    )(blk_e, nused, first, slot.astype(I32), nxt_e, xs, w_gate, w_up, w_down)


def _moe_residual(x1, y1_ref, y2_ref, gates, out_gate):
    tc, d = x1.shape
    acc = jnp.zeros((tc, d), F32)
    for k, y_ref in enumerate((y1_ref, y2_ref)):
        gcol = jnp.transpose(jnp.broadcast_to(gates[k:k + 1, :], (LANES, tc)))
        acc = acc + _unpack_pair(y_ref[...]) * jnp.concatenate([gcol] * (d // LANES), axis=1)
    return x1 + out_gate * acc


def _combine_kernel(x1_ref, y1_ref, y2_ref, gate_ref, mod_ref, fg_ref, o_ref, *, final):
    x2 = _moe_residual(x1_ref[...], y1_ref, y2_ref, gate_ref[0], mod_ref[0][5:6, :])
    if final:
        x2 = _rms(x2) * fg_ref[...]
    o_ref[...] = x2


def _combine(x1, y12, gates, mod_l, final_gain, tc, seq, final):
    t, d = x1.shape
    nt = t // tc
    tiles_per_b = seq // tc
    return pl.pallas_call(
        functools.partial(_combine_kernel, final=final),
        out_shape=jax.ShapeDtypeStruct((t, d), F32),
        grid=(nt,),
        in_specs=[pl.BlockSpec((tc, d), lambda i: (i, 0)),
                  pl.BlockSpec((tc, _packed_width(d)), lambda i: (i, 0)),
                  pl.BlockSpec((tc, _packed_width(d)), lambda i: (nt + i, 0)),
                  pl.BlockSpec((1, TOP_K, tc), lambda i: (i, 0, 0)),
                  pl.BlockSpec((1, 6, d), lambda i: (i // tiles_per_b, 0, 0)),
                  pl.BlockSpec((1, d), lambda i: (0, 0))],
        out_specs=pl.BlockSpec((tc, d), lambda i: (i, 0)),
        compiler_params=_cparams(("arbitrary",)),
        name="moe_combine",
    )(x1, y12, y12, gates, mod_l, final_gain)


ROUTE_TILE = 512


def _moe(h2, sel, totals, w_gate, w_up, w_down, layer):
    t = h2.shape[0]
    ne = w_gate.shape[1]
    nb = (t * TOP_K + ne * (MOE_BLOCK - 1) + MOE_BLOCK - 1) // MOE_BLOCK
    nbpad = -(-nb // LANES) * LANES
    dest, gates, blk = _route(sel, totals, nbpad)
    slot = jnp.transpose(dest, (1, 0, 2)).reshape(TOP_K * t)
    xs = _sc_scatter_rows(h2, slot, nb * MOE_BLOCK)
    ys = _experts(xs, blk[0, :nb], blk[1, :1], w_gate, w_up, w_down, layer)
    return _sc_gather_rows(ys, slot), gates


def kernel(x, c, positions, mod_w, mod_b, norm_mix, norm_ffn, ev_w_in, ev_ret_norm, ev_gla_wa2, ev_gla_ba,
           ev_gla_norm, ev_w_out, od_w_in, od_conv_w, od_conv_b, od_dt_bias, od_a_log, od_d, od_norm, od_w_out,
           router_w, router_b, exp_w_gate, exp_w_up, exp_w_down, final_norm):
    bsz, seq, d = x.shape
    depth = mod_w.shape[0]
    t = bsz * seq
    ne = router_w.shape[1]
    epg = ne // N_GROUPS
    mod = _modulation(c, mod_w, mod_b)
    x2d = x.reshape(t, d)

    perm = np.array([g * epg + j for j in range(epg) for g in range(N_GROUPS)])
    rw_t = router_w.T[perm].astype(BF16)
    rb_b = jnp.broadcast_to(router_b.astype(F32)[perm][:, None], (ne, LANES))
    final_gain = final_norm.reshape(1, d)

    half = RET_DK // 2
    inv = ROPE_BASE ** (-jnp.arange(half, dtype=F32) / half)
    inv2 = jnp.concatenate([inv, inv]).reshape(1, LANES)
    pos_col = positions.reshape(t, 1)

    pending = None
    for layer in range(depth):
        i = layer // 2
        mod_l = mod[layer]
        gain1 = norm_mix[layer].reshape(1, d)
        gain2 = norm_ffn[layer].reshape(1, d)
        if layer % 2 == 0:
            if pending is not None:
                x2d = _combine(*pending, final_gain, ROUTE_TILE, seq, final=False)
            w_in = ev_w_in[i]
            nmain = w_in.shape[1] - GLA_RANK
            w_main = w_in[:, :nmain].astype(BF16)
            w_ga = jnp.pad(w_in[:, nmain:], ((0, 0), (0, LANES - GLA_RANK))).astype(BF16)
            wa2 = jnp.pad(ev_gla_wa2[i], ((0, LANES - GLA_RANK), (0, 0))).astype(BF16)
            y, glog = _even_proj(x2d, mod_l, gain1, pos_col, inv2, w_main, w_ga, wa2,
                                 ev_gla_ba[i].reshape(1, -1), PROJ_TILE, seq)
            o = _ret_gla(y, glog, ev_ret_norm[i].reshape(1, -1), ev_gla_norm[i].reshape(1, -1), bsz, seq, 512)
            w_out = ev_w_out[i].astype(BF16)
        else:
            assert pending is not None
            w_in = od_w_in[i]
            nheads = od_a_log.shape[1]
            nmain = w_in.shape[1] - nheads
            dinner = od_norm.shape[1]
            gw, ns_ = dinner // SSD_GROUPS, SSD_STATE
            bc0 = dinner + SSD_GROUPS * ns_

            def slabs(a, off=0):
                return [a[..., off + lo:off + lo + n] for g in range(SSD_GROUPS)
                        for lo, n in ((g * gw, gw), (dinner + g * ns_, ns_), (bc0 + g * ns_, ns_))]

            conv_w = jnp.concatenate(slabs(od_conv_w[i]), axis=-1)
            conv_b = jnp.concatenate(slabs(od_conv_b[i]), axis=-1).reshape(1, -1)
            w_main = jnp.concatenate(slabs(w_in, dinner) + [w_in[:, :dinner]], axis=1).astype(BF16)
            w_dt = jnp.pad(w_in[:, nmain:], ((0, 0), (0, LANES - nheads))).astype(BF16)
            dtb = jnp.pad(od_dt_bias[i], (0, LANES - nheads)).reshape(1, LANES)
            y, dt, x2d = _odd_proj(*pending, mod_l, gain1, w_main, w_dt, dtb, od_conv_w.shape[2], ROUTE_TILE, seq)
            alog = jnp.pad(od_a_log[i], (0, LANES - nheads)).reshape(1, LANES)
            d_row = jnp.repeat(od_d[i], SSD_HEADDIM).reshape(1, dinner)
            o = _ssd(y, dt, conv_w, conv_b, alog, d_row, od_norm[i].reshape(1, dinner), bsz, seq)
            w_out = od_w_out[i].astype(BF16)
        x1, h2, sel, totals = _outproj(o, x2d, mod_l, gain2, w_out, rw_t, rb_b, ROUTE_TILE, seq)
        y12, gates = _moe(h2, sel, totals, exp_w_gate, exp_w_up, exp_w_down, layer)
        pending = (x1, y12, gates, mod_l)
    out = _combine(*pending, final_gain, ROUTE_TILE, seq, final=True)
    return out.reshape(bsz, seq, d)
```

```python
import functools

import jax
import jax.numpy as jnp
import numpy as np
from jax import lax
from jax.experimental import pallas as pl
from jax.experimental.pallas import tpu as pltpu
from jax.experimental.pallas import tpu_sc as plsc

F32 = jnp.float32
BF16 = jnp.bfloat16
I32 = jnp.int32

RET_HEADS, RET_DK, RET_DV, RET_CHUNK = 4, 128, 256, 128
ROPE_BASE = 10000.0
GLA_HEADS, GLA_DK, GLA_DV, GLA_RANK, GLA_TAU, GLA_CHUNK = 4, 128, 256, 16, 16.0, 64
SSD_HEADDIM, SSD_GROUPS, SSD_STATE, SSD_CONV, SSD_CHUNK = 64, 8, 128, 4, 128
N_GROUPS, TOP_K, GROUP_SCORE_K = 8, 2, 2
MOE_BLOCK = 512
NORM_EPS = 1e-6

PROJ_TILE = 512
LANES = 128
VMEM_LIMIT = 56 * 1024 * 1024


def _cparams(sem, vmem=VMEM_LIMIT):
    return pltpu.CompilerParams(dimension_semantics=sem, vmem_limit_bytes=vmem)


def _dot(a, b):
    return jnp.dot(a, b, preferred_element_type=F32)


def _dot_nt(a, b):
    return lax.dot_general(a, b, (((1,), (1,)), ((), ())), preferred_element_type=F32)


def _dot_tn(a, b):
    return lax.dot_general(a, b, (((0,), (0,)), ((), ())), preferred_element_type=F32)


def _split3(x):
    a = x.astype(BF16)
    r = x - a.astype(F32)
    b = r.astype(BF16)
    c = (r - b.astype(F32)).astype(BF16)
    return a, b, c


def _dot_exact_rhs(m01, x):
    a, b, c = _split3(x)
    return _dot(m01, a) + _dot(m01, b) + _dot(m01, c)


def _dot_exact_lhs(x, m01):
    a, b, c = _split3(x)
    return _dot(a, m01) + _dot(b, m01) + _dot(c, m01)


def _dot_tn_exact(x, m01):
    a, b, c = _split3(x)
    return _dot_tn(a, m01) + _dot_tn(b, m01) + _dot_tn(c, m01)


def _silu(x):
    return x * jax.nn.sigmoid(x)


def _rms(x, eps=NORM_EPS):
    return x * lax.rsqrt(jnp.mean(x * x, axis=-1, keepdims=True) + eps)


def _iota(shape, dim):
    return lax.broadcasted_iota(I32, shape, dim)


def _interleave(items, starts_per_round):
    waiting, running = list(items), []
    while waiting or running:
        running += [waiting.pop(0) for _ in range(min(starts_per_round, len(waiting)))]
        for item in reversed(list(running)):
            if next(item, StopIteration) is StopIteration:
                running.remove(item)


PACKED = jnp.uint32
_HI16 = 0xFFFF0000


def _packed_width(n):
    return n // 2


def _pack_pair(x):
    n = x.shape[1] // 2
    lo = lax.bitcast_convert_type(x[:, :n].astype(BF16).astype(F32), jnp.uint32)
    hi = lax.bitcast_convert_type(x[:, n:].astype(BF16).astype(F32), jnp.uint32)
    return (hi & jnp.uint32(_HI16)) | (lo >> 16)


def _unpack_pair(w):
    lo = lax.bitcast_convert_type(w << 16, F32)
    hi = lax.bitcast_convert_type(w & jnp.uint32(_HI16), F32)
    return jnp.concatenate([lo, hi], axis=1)


def _mod_kernel(c_ref, w_ref, b_ref, o_ref):
    sc = _silu(c_ref[...])
    o_ref[0] = _dot(sc.astype(BF16), w_ref[0].astype(BF16)) + b_ref[0]


def _modulation(c, mod_w, mod_b):
    depth, d, d6 = mod_w.shape
    bsz = c.shape[0]
    nb = d6 // d
    out = pl.pallas_call(
        _mod_kernel,
        out_shape=jax.ShapeDtypeStruct((depth, bsz, d6), F32),
        grid=(depth, nb),
        in_specs=[pl.BlockSpec((bsz, d), lambda l, j: (0, 0)),
                  pl.BlockSpec((1, d, d), lambda l, j: (l, 0, j)),
                  pl.BlockSpec((1, 1, d), lambda l, j: (l, 0, j))],
        out_specs=pl.BlockSpec((1, bsz, d), lambda l, j: (l, 0, j)),
        compiler_params=_cparams(("arbitrary", "arbitrary")),
        name="adaln_mod",
    )(c, mod_w, mod_b.reshape(depth, 1, d6))
    return out.reshape(depth, bsz, nb, d)


def _norm_mod(x, gain, mod, shift_row, scale_row):
    y = _rms(x) * gain
    return y * (1.0 + mod[scale_row:scale_row + 1, :]) + mod[shift_row:shift_row + 1, :]


def _even_proj_kernel(x_ref, mod_ref, gain_ref, pos_ref, inv_ref, w_ref, wga_ref, wa2_ref, ba_ref,
                      y_ref, glog_ref, *, ncol):
    h = _norm_mod(x_ref[...], gain_ref[...], mod_ref[0], 0, 1)
    hb = h.astype(BF16)
    tm = hb.shape[0]
    hm = tm // 2
    half = LANES // 2
    lo = _iota((hm, LANES), 1) < half
    ang = jnp.where(lo, pos_ref[0:hm, :], pos_ref[hm:tm, :]).astype(F32) * inv_ref[...]
    c2, s2 = jnp.cos(ang), jnp.sin(ang)
    c2r, s2r = pltpu.roll(c2, half, 1), pltpu.roll(s2, half, 1)
    cos = jnp.concatenate([jnp.where(lo, c2, c2r), jnp.where(lo, c2r, c2)], axis=0)
    sin_a, sin_b = jnp.where(lo, s2, s2r), jnp.where(lo, s2r, s2)
    sin = jnp.concatenate([jnp.where(lo, -sin_a, sin_a), jnp.where(lo, -sin_b, sin_b)], axis=0)
    qk_scale = RET_DK ** -0.5
    rqk = RET_HEADS * RET_DK
    rv = RET_HEADS * RET_DV
    gq0 = 2 * rqk + 2 * rv
    width = w_ref.shape[1] // ncol
    for j in range(ncol):
        c0 = j * width
        acc = _dot(hb, w_ref[:, c0:c0 + width])
        for s in range(width // LANES):
            col = c0 + s * LANES
            blk = acc[:, s * LANES:(s + 1) * LANES]
            if col < 2 * rqk:
                blk = blk * cos + pltpu.roll(blk, LANES // 2, 1) * sin
                if col >= rqk:
                    blk = blk * qk_scale
            elif gq0 <= col < gq0 + GLA_HEADS * GLA_DK:
                blk = blk * (GLA_DK ** -0.5)
            y_ref[:, col:col + LANES] = blk.astype(y_ref.dtype)
    ga = _dot(hb, wga_ref[...])
    z = _dot(ga.astype(BF16), wa2_ref[...]) + ba_ref[...]
    logsig = jnp.minimum(z, 0.0) - jnp.log1p(jnp.exp(-jnp.abs(z)))
    glog_ref[...] = logsig * (1.0 / GLA_TAU)


def _even_proj(x2d, mod_l, gain, pos_col, inv2, w_main, w_ga, wa2, ba, tm, seq):
    t, d = x2d.shape
    n = w_main.shape[1]
    tiles_per_b = seq // tm
    gk = wa2.shape[1]
    const = lambda i: (0, 0)
    return pl.pallas_call(
        functools.partial(_even_proj_kernel, ncol=n // 512),
        out_shape=(jax.ShapeDtypeStruct((t, n), BF16), jax.ShapeDtypeStruct((t, gk), F32)),
        grid=(t // tm,),
        in_specs=[pl.BlockSpec((tm, d), lambda i: (i, 0)),
                  pl.BlockSpec((1, 6, d), lambda i: (i // tiles_per_b, 0, 0)),
                  pl.BlockSpec((1, d), const),
                  pl.BlockSpec((tm, 1), lambda i: (i, 0)),
                  pl.BlockSpec((1, LANES), const),
                  pl.BlockSpec((d, n), const, pipeline_mode=pl.Buffered(1)),
                  pl.BlockSpec((d, LANES), const),
                  pl.BlockSpec((LANES, gk), const),
                  pl.BlockSpec((1, gk), const)],
        out_specs=(pl.BlockSpec((tm, n), lambda i: (i, 0)), pl.BlockSpec((tm, gk), lambda i: (i, 0))),
        compiler_params=_cparams(("arbitrary",)),
        name="even_proj",
    )(x2d, mod_l, gain, pos_col, inv2, w_main, w_ga, wa2, ba)


RET_GLA_UNROLL = 4
RET_GLA_STARTS = 4


def _ret_gla_kernel(rq_ref, rk_ref, rv_ref, rg_ref, gq_ref, gk_ref, gv_ref, gg_ref, gl_ref,
                    rn_ref, gn_ref, dec_ref, zeta_ref, xi_ref, o_ref, rst_ref, gst_ref, *, chunk_decay):
    @pl.when(pl.program_id(1) == 0)
    def _():
        rst_ref[...] = jnp.zeros_like(rst_ref)
        gst_ref[...] = jnp.zeros_like(gst_ref)

    ls = rq_ref.shape[0]
    lr, lg = RET_CHUNK, GLA_CHUNK
    trow = _iota((lg, lg), 0)
    tcol = _iota((lg, lg), 1)
    causal = trow >= tcol
    tri = causal.astype(BF16)
    base = RET_HEADS * RET_DV

    def ret_item(r0, h):
        q = rq_ref[pl.ds(r0, lr), h * RET_DK:(h + 1) * RET_DK]
        k = rk_ref[pl.ds(r0, lr), h * RET_DK:(h + 1) * RET_DK]
        v = rv_ref[pl.ds(r0, lr), h * RET_DV:(h + 1) * RET_DV]
        s = _dot_nt(q, k) * dec_ref[h]
        qx = (q.astype(F32) * xi_ref[h]).astype(BF16)
        kz = (k.astype(F32) * zeta_ref[h]).astype(BF16)
        yield
        inner = _dot(s.astype(BF16), v)
        kv = _dot_tn(kz, v)
        yield
        gate = _silu(rg_ref[pl.ds(r0, lr), h * RET_DV:(h + 1) * RET_DV].astype(F32))
        yield
        st = rst_ref[h]
        cross = _dot(qx, st.astype(BF16))
        rst_ref[h] = st * chunk_decay[h] + kv
        o = _rms(inner + cross) * rn_ref[:, h * RET_DV:(h + 1) * RET_DV] * gate
        o_ref[pl.ds(r0, lr), h * RET_DV:(h + 1) * RET_DV] = o.astype(o_ref.dtype)

    def gla_item(r0, h):
        gl = gl_ref[pl.ds(r0, lg), h * GLA_DK:(h + 1) * GLA_DK]
        b = _dot_exact_rhs(tri, gl)
        yield
        q = gq_ref[pl.ds(r0, lg), h * GLA_DK:(h + 1) * GLA_DK].astype(F32)
        k = gk_ref[pl.ds(r0, lg), h * GLA_DK:(h + 1) * GLA_DK].astype(F32)
        bref = b[lg // 2:lg // 2 + 1, :]
        blast = b[lg - 1:lg, :]
        att = _dot_nt((q * jnp.exp(b - bref)).astype(BF16), (k * jnp.exp(bref - b)).astype(BF16))
        qb = (q * jnp.exp(b)).astype(BF16)
        kd = (k * jnp.exp(blast - b)).astype(BF16)
        dec = jnp.exp(blast)
        yield
        v = gv_ref[pl.ds(r0, lg), h * GLA_DV:(h + 1) * GLA_DV]
        inner = _dot(jnp.where(causal, att, 0.0).astype(BF16), v)
        kvt = _dot_tn(v, kd)
        gate = _silu(gg_ref[pl.ds(r0, lg), h * GLA_DV:(h + 1) * GLA_DV].astype(F32))
        yield
        st = gst_ref[h]
        cross = _dot_nt(qb, st.astype(BF16))
        gst_ref[h] = st * dec + kvt
        o = _rms(inner + cross) * gn_ref[:, h * GLA_DV:(h + 1) * GLA_DV] * gate
        o_ref[pl.ds(r0, lg), base + h * GLA_DV:base + (h + 1) * GLA_DV] = o.astype(o_ref.dtype)

    def step(c, carry):
        waiting = []
        for u in range(RET_GLA_UNROLL):
            r0 = pl.multiple_of((c * RET_GLA_UNROLL + u) * lr, lr)
            waiting += [ret_item(r0, h) for h in range(RET_HEADS)]
            for j in range(lr // lg):
                rj = pl.multiple_of(r0 + j * lg, lg)
                waiting += [gla_item(rj, h) for h in range(GLA_HEADS)]
        assert RET_HEADS % RET_GLA_STARTS == 0 and GLA_HEADS % RET_GLA_STARTS == 0
        _interleave(waiting, RET_GLA_STARTS)
        return carry

    lax.fori_loop(0, ls // (lr * RET_GLA_UNROLL), step, 0)


def _ret_gla(y, glog, ret_norm, gla_norm, bsz, seq, ls):
    t = y.shape[0]
    ns = seq // ls
    qw, vw = RET_HEADS * RET_DK, RET_HEADS * RET_DV
    L = RET_CHUNK
    log_gamma = jnp.log1p(-jnp.exp2(-5.0 - jnp.arange(RET_HEADS, dtype=F32)))
    idx = jnp.arange(L, dtype=F32)
    diff = idx[:, None] - idx[None, :]
    decay = jnp.where(diff >= 0, jnp.exp(log_gamma[:, None, None] * jnp.maximum(diff, 0.0)), 0.0)
    zeta = jnp.broadcast_to(jnp.exp(log_gamma[:, None] * (L - 1 - idx)[None, :])[:, :, None], (RET_HEADS, L, RET_DK))
    xi = jnp.broadcast_to(jnp.exp(log_gamma[:, None] * (idx + 1.0)[None, :])[:, :, None], (RET_HEADS, L, RET_DK))
    chunk_decay = tuple(float(np.exp(np.float32(np.log1p(-np.exp2(np.float32(-5.0 - i)))) * np.float32(L)))
                        for i in range(RET_HEADS))
    rowmap = lambda j: (lambda b, i: (b * ns + i, j))
    const = lambda b, i: (0, 0)
    const3 = lambda b, i: (0, 0, 0)
    return pl.pallas_call(
        functools.partial(_ret_gla_kernel, chunk_decay=chunk_decay),
        out_shape=jax.ShapeDtypeStruct((t, 2 * vw), BF16),
        grid=(bsz, ns),
        in_specs=[pl.BlockSpec((ls, qw), rowmap(0)),
                  pl.BlockSpec((ls, qw), rowmap(1)),
                  pl.BlockSpec((ls, vw), rowmap(1)),
                  pl.BlockSpec((ls, vw), rowmap(2)),
                  pl.BlockSpec((ls, qw), rowmap(6)),
                  pl.BlockSpec((ls, qw), rowmap(7)),
                  pl.BlockSpec((ls, vw), rowmap(4)),
                  pl.BlockSpec((ls, vw), rowmap(5)),
                  pl.BlockSpec((ls, qw), rowmap(0)),
                  pl.BlockSpec((1, vw), const),
                  pl.BlockSpec((1, vw), const),
                  pl.BlockSpec((RET_HEADS, L, L), const3),
                  pl.BlockSpec((RET_HEADS, L, RET_DK), const3),
                  pl.BlockSpec((RET_HEADS, L, RET_DK), const3)],
        out_specs=pl.BlockSpec((ls, 2 * vw), lambda b, i: (b * ns + i, 0)),
        scratch_shapes=[pltpu.VMEM((RET_HEADS, RET_DK, RET_DV), F32),
                        pltpu.VMEM((GLA_HEADS, GLA_DV, GLA_DK), F32)],
        compiler_params=_cparams(("arbitrary", "arbitrary")),
        name="ret_gla_mixer",
    )(y, y, y, y, y, y, y, y, glog, ret_norm, gla_norm, decay, zeta, xi)


def _odd_proj_kernel(x1_ref, y1_ref, y2_ref, gate_ref, modp_ref, mod_ref, gain_ref, w_ref, wdt_ref, dtb_ref,
                     y_ref, dt_ref, x_ref, *, ncol, cch):
    x = _moe_residual(x1_ref[...], y1_ref, y2_ref, gate_ref[0], modp_ref[0][5:6, :])
    x_ref[...] = x
    h = _norm_mod(x, gain_ref[...], mod_ref[0], 0, 1)
    hb = h.astype(BF16)
    width = w_ref.shape[1] // ncol
    for j in range(ncol):
        c0 = j * width
        acc = _dot(hb, w_ref[:, c0:c0 + width])
        if c0 >= cch:
            acc = _silu(acc)
        y_ref[:, c0:c0 + width] = acc.astype(y_ref.dtype)
    z = _dot(hb, wdt_ref[...]) + dtb_ref[...]
    dt_ref[...] = jnp.maximum(z, 0.0) + jnp.log1p(jnp.exp(-jnp.abs(z)))


def _odd_proj(x1, y12, gates, mod_prev, mod_l, gain, w_main, w_dt, dt_bias, cch, tm, seq):
    t, d = x1.shape
    n = w_main.shape[1]
    nt = t // tm
    tiles_per_b = seq // tm
    const = lambda i: (0, 0)
    batch = lambda i: (i // tiles_per_b, 0, 0)
    return pl.pallas_call(
        functools.partial(_odd_proj_kernel, ncol=n // 512, cch=cch),
        out_shape=(jax.ShapeDtypeStruct((t, n), BF16), jax.ShapeDtypeStruct((t, LANES), F32),
                   jax.ShapeDtypeStruct((t, d), F32)),
        grid=(nt,),
        in_specs=[pl.BlockSpec((tm, d), lambda i: (i, 0)),
                  pl.BlockSpec((tm, _packed_width(d)), lambda i: (i, 0)),
                  pl.BlockSpec((tm, _packed_width(d)), lambda i: (nt + i, 0)),
                  pl.BlockSpec((1, TOP_K, tm), lambda i: (i, 0, 0)),
                  pl.BlockSpec((1, 6, d), batch),
                  pl.BlockSpec((1, 6, d), batch),
                  pl.BlockSpec((1, d), const),
                  pl.BlockSpec((d, n), const, pipeline_mode=pl.Buffered(1)),
                  pl.BlockSpec((d, LANES), const),
                  pl.BlockSpec((1, LANES), const)],
        out_specs=(pl.BlockSpec((tm, n), lambda i: (i, 0)), pl.BlockSpec((tm, LANES), lambda i: (i, 0)),
                   pl.BlockSpec((tm, d), lambda i: (i, 0))),
        compiler_params=_cparams(("arbitrary",)),
        name="odd_proj",
    )(x1, y12, y12, gates, mod_prev, mod_l, gain, w_main, w_dt, dt_bias)


def _ssd_kernel(z_ref, xbc_ref, dt_ref, cw_ref, cb_ref, alog_ref, dsk_ref, ng_ref, o_ref, hist_ref, st_ref):
    @pl.when(pl.program_id(1) == 0)
    def _():
        hist_ref[...] = jnp.zeros_like(hist_ref)
        st_ref[...] = jnp.zeros_like(st_ref)

    L = xbc_ref.shape[0]
    P, N, G = SSD_HEADDIM, SSD_STATE, SSD_GROUPS
    dinner = z_ref.shape[1]
    gw = dinner // G
    hpg = gw // P
    slab = gw + 2 * N
    hrows = hist_ref.shape[0]

    dt = dt_ref[...]
    da = dt * (-jnp.exp(alog_ref[...]))
    trow = _iota((L, L), 0)
    tcol = _iota((L, L), 1)
    causal = trow >= tcol
    cum = _dot_exact_rhs(causal.astype(BF16), da)
    cum_t = jnp.transpose(cum)
    dt_t = jnp.transpose(dt)
    ecum = jnp.exp(cum)
    wend = dt * jnp.exp(cum[L - 1:L, :] - cum)
    lane = _iota((L, LANES), 1)
    lo_half = lane < P

    def pair_cols(a, ha):
        return jnp.where(lo_half, jnp.broadcast_to(a[:, ha:ha + 1], (L, LANES)),
                         jnp.broadcast_to(a[:, ha + 1:ha + 2], (L, LANES)))

    srow = _iota((L, hrows + L), 0)
    scol = _iota((L, hrows + L), 1)
    shift = jnp.concatenate([(scol == srow + (hrows - s)).astype(BF16) for s in range(1, SSD_CONV)], axis=0)

    def group_item(g):
        c0 = g * slab
        xin = xbc_ref[:, c0:c0 + slab]
        xcat = jnp.concatenate([hist_ref[:, c0:c0 + slab], xin], axis=0)
        hist_ref[:, c0:c0 + slab] = xin[L - hrows:L]
        sh = _dot(shift, xcat)
        conv = xin.astype(F32) * cw_ref[SSD_CONV - 1:SSD_CONV, c0:c0 + slab] + cb_ref[:, c0:c0 + slab]
        for s in range(1, SSD_CONV):
            conv = conv + sh[(s - 1) * L:s * L] * cw_ref[SSD_CONV - 1 - s:SSD_CONV - s, c0:c0 + slab]
        act = _silu(conv)
        xs = act[:, :gw]
        bm_g = act[:, gw:gw + N].astype(BF16)
        cm_g = act[:, gw + N:].astype(BF16)
        yield
        cb = _dot_nt(cm_g, bm_g)
        lhs, xw, ec = [], [], []
        for pr in range(hpg // 2):
            ha = g * hpg + 2 * pr
            ms = []
            for hh in (ha, ha + 1):
                seg = jnp.broadcast_to(cum[:, hh:hh + 1], (L, L)) - cum_t[hh:hh + 1, :]
                ms.append((cb * jnp.where(causal, jnp.exp(seg), 0.0) * dt_t[hh:hh + 1, :]).astype(BF16))
            lhs.append(jnp.concatenate(ms, axis=1))
            xw.append((xs[:, pr * 2 * P:(pr + 1) * 2 * P] * pair_cols(wend, ha)).astype(BF16))
            ec.append(pair_cols(ecum, ha))
        ecum_g = jnp.concatenate(ec, axis=1)
        yield
        xs_b = xs.astype(BF16)
        pieces = []
        for pr in range(hpg // 2):
            blk = xs_b[:, pr * 2 * P:(pr + 1) * 2 * P]
            zero = jnp.zeros_like(blk)
            rhs = jnp.concatenate([jnp.where(lo_half, blk, zero), jnp.where(lo_half, zero, blk)], axis=0)
            pieces.append(_dot(lhs[pr], rhs))
        st = st_ref[g]
        cross = _dot(cm_g, st.astype(BF16)) * ecum_g
        st_ref[g] = st * ecum_g[L - 1:L, :] + _dot_tn(bm_g, jnp.concatenate(xw, axis=1))
        y = jnp.concatenate(pieces, axis=1) + cross + dsk_ref[:, g * gw:(g + 1) * gw] * xs
        y = y * z_ref[:, g * gw:(g + 1) * gw].astype(F32)
        o_ref[:, g * gw:(g + 1) * gw] = (_rms(y) * ng_ref[:, g * gw:(g + 1) * gw]).astype(o_ref.dtype)

    _interleave([group_item(g) for g in range(G)], SSD_STARTS)


SSD_STARTS = 1
SSD_HIST_ROWS = 16


def _ssd(y, dt, conv_w, conv_b, a_log_row, d_row, norm_g, bsz, seq):
    t = y.shape[0]
    L = SSD_CHUNK
    ns = seq // L
    dinner = norm_g.shape[1]
    cch = conv_w.shape[1]
    gw = dinner // SSD_GROUPS
    const = lambda b, i: (0, 0)
    assert cch % dinner == 0
    return pl.pallas_call(
        _ssd_kernel,
        out_shape=jax.ShapeDtypeStruct((t, dinner), BF16),
        grid=(bsz, ns),
        in_specs=[pl.BlockSpec((L, dinner), lambda b, i: (b * ns + i, cch // dinner)),
                  pl.BlockSpec((L, cch), lambda b, i: (b * ns + i, 0)),
                  pl.BlockSpec((L, LANES), lambda b, i: (b * ns + i, 0)),
                  pl.BlockSpec((SSD_CONV, cch), const),
                  pl.BlockSpec((1, cch), const),
                  pl.BlockSpec((1, LANES), const),
                  pl.BlockSpec((1, dinner), const),
                  pl.BlockSpec((1, dinner), const)],
        out_specs=pl.BlockSpec((L, dinner), lambda b, i: (b * ns + i, 0)),
        scratch_shapes=[pltpu.VMEM((SSD_HIST_ROWS, cch), BF16),
                        pltpu.VMEM((SSD_GROUPS, SSD_STATE, gw), F32)],
        compiler_params=_cparams(("arbitrary", "arbitrary")),
        name="ssd_mixer",
    )(y, y, dt, conv_w, conv_b, a_log_row, d_row, norm_g)


OUT_SLAB = 256


def _route_choices(hb, rw, rb):
    tm = hb.shape[0]
    ne = rw.shape[0]
    epg = ne // N_GROUPS
    logits = _dot_nt(rw, hb)
    s = jax.nn.sigmoid(logits)
    sb = s + jnp.concatenate([rb] * (tm // LANES), axis=1)
    G = N_GROUPS
    v = [sb[j * G:(j + 1) * G, :] for j in range(epg)]
    m01, n01 = jnp.maximum(v[0], v[1]), jnp.minimum(v[0], v[1])
    m23, n23 = jnp.maximum(v[2], v[3]), jnp.minimum(v[2], v[3])
    top1 = jnp.maximum(m01, m23)
    top2 = jnp.maximum(jnp.minimum(m01, m23), jnp.maximum(n01, n23))
    gscore = top1 + top2
    gi = _iota((G, tm), 0)
    gmax = jnp.max(gscore, axis=0, keepdims=True)
    gidx = jnp.min(jnp.where(gscore == gmax, gi, G), axis=0, keepdims=True)
    sel = gi == gidx
    vb = [jnp.sum(jnp.where(sel, v[j], 0.0), axis=0, keepdims=True) for j in range(epg)]
    vs = [jnp.sum(jnp.where(sel, s[j * G:(j + 1) * G, :], 0.0), axis=0, keepdims=True) for j in range(epg)]
    zero = jnp.zeros_like(vb[0])
    e1 = zero
    e2 = zero
    w1 = zero
    w2 = zero
    for j in range(epg):
        rank = zero
        for i in range(epg):
            if i == j:
                continue
            ahead = (vb[i] >= vb[j]) if i < j else (vb[i] > vb[j])
            rank = rank + jnp.where(ahead, 1.0, 0.0)
        first = rank == 0.0
        second = rank == 1.0
        e1 = e1 + jnp.where(first, float(j), 0.0)
        e2 = e2 + jnp.where(second, float(j), 0.0)
        w1 = w1 + jnp.where(first, vs[j], 0.0)
        w2 = w2 + jnp.where(second, vs[j], 0.0)
    gf = gidx.astype(F32) * float(epg)
    wsum = w1 + w2
    return jnp.concatenate([e1 + gf, e2 + gf, w1 / wsum, w2 / wsum], axis=0)


def _outproj_kernel(o_ref, x_ref, mod_ref, gain_ref, w_ref, rw_ref, rb_ref, x1_ref, h2_ref, sel_ref, cnt_ref):
    mod = mod_ref[0]
    ne = rw_ref.shape[0]

    @pl.when(pl.program_id(0) == 0)
    def _():
        cnt_ref[...] = jnp.zeros_like(cnt_ref)

    def row_slab(r0):
        rows = pl.ds(r0, OUT_SLAB)
        mix = _dot(o_ref[rows, :], w_ref[...])
        yield
        x1 = x_ref[rows, :] + mod[2:3, :] * mix
        x1_ref[rows, :] = x1
        h2 = _norm_mod(x1, gain_ref[...], mod, 3, 4)
        h2_ref[rows, :] = _pack_pair(h2)
        choice = _route_choices(h2.astype(BF16), rw_ref[...], rb_ref[...])
        sel_ref[0, :, rows] = choice
        ei = _iota((ne, OUT_SLAB), 0).astype(F32)
        onehot = jnp.where((ei == choice[0:1, :]) | (ei == choice[1:2, :]), 1.0, 0.0)
        cnt_ref[...] = cnt_ref[...] + jnp.sum(onehot, axis=1, keepdims=True)

    _interleave([row_slab(r0) for r0 in range(0, o_ref.shape[0], OUT_SLAB)], 1)


def _outproj(o, x2d, mod_l, gain2, w_out, rw_t, rb_b, tm, seq):
    t, d = x2d.shape
    kin = o.shape[1]
    ne = rw_t.shape[0]
    nt = t // tm
    tiles_per_b = seq // tm
    const = lambda i: (0, 0)
    return pl.pallas_call(
        _outproj_kernel,
        out_shape=(jax.ShapeDtypeStruct((t, d), F32), jax.ShapeDtypeStruct((t, _packed_width(d)), PACKED),
                   jax.ShapeDtypeStruct((nt, 4, tm), F32), jax.ShapeDtypeStruct((ne, LANES), F32)),
        grid=(nt,),
        in_specs=[pl.BlockSpec((tm, kin), lambda i: (i, 0)),
                  pl.BlockSpec((tm, d), lambda i: (i, 0)),
                  pl.BlockSpec((1, 6, d), lambda i: (i // tiles_per_b, 0, 0)),
                  pl.BlockSpec((1, d), const),
                  pl.BlockSpec((kin, d), const),
                  pl.BlockSpec((ne, d), const),
                  pl.BlockSpec((ne, LANES), const)],
        out_specs=(pl.BlockSpec((tm, d), lambda i: (i, 0)), pl.BlockSpec((tm, _packed_width(d)), lambda i: (i, 0)),
                   pl.BlockSpec((1, 4, tm), lambda i: (i, 0, 0)), pl.BlockSpec((ne, LANES), const)),
        compiler_params=_cparams(("arbitrary",)),
        name="out_proj",
    )(o, x2d, mod_l, gain2, w_out, rw_t, rb_b)


def _route_kernel(sel_ref, tot_ref, dest_ref, gate_ref, blk_ref, cnt_ref, pst_ref, *, nbpad):
    step = pl.program_id(0)
    tm = sel_ref.shape[2]
    ne = tot_ref.shape[0]
    ei = _iota((ne, tm), 0).astype(F32)

    @pl.when(step == 0)
    def _():
        nblk = jnp.floor((tot_ref[...] + (MOE_BLOCK - 1.0)) * (1.0 / MOE_BLOCK))
        hi = jnp.floor(nblk * (1.0 / 16.0))
        lo = nblk - hi * 16.0
        er = _iota((ne, ne), 0)
        ec = _iota((ne, ne), 1)
        lower = (ec < er).astype(BF16)
        pst = _dot(lower, hi.astype(BF16)) * 16.0 + _dot(lower, lo.astype(BF16))
        pst_ref[...] = pst
        pend = jnp.concatenate([pst + nblk] * (nbpad // LANES), axis=1)
        bidx = _iota((ne, nbpad), 1).astype(F32)
        be = jnp.sum(jnp.where(pend <= bidx, 1.0, 0.0), axis=0, keepdims=True)
        total = jnp.max(pend, axis=0, keepdims=True)
        blk_ref[0:1, :] = jnp.minimum(be, ne - 1.0).astype(I32)
        blk_ref[1:2, :] = total.astype(I32)
        cnt_ref[...] = jnp.zeros_like(cnt_ref)

    choice = sel_ref[0]
    oh1 = ei == choice[0:1, :]
    oh2 = ei == choice[1:2, :]
    onehot = jnp.where(oh1 | oh2, 1.0, 0.0)
    upper = (_iota((tm, tm), 0) < _iota((tm, tm), 1)).astype(BF16)
    prefix = _dot(onehot.astype(BF16), upper)
    cnt = cnt_ref[...]
    base = jnp.concatenate([cnt] * (tm // LANES), axis=1) + prefix
    pst = jnp.concatenate([pst_ref[...]] * (tm // LANES), axis=1) * float(MOE_BLOCK)
    slot = base + pst
    d1 = jnp.sum(jnp.where(oh1, slot, 0.0), axis=0, keepdims=True)
    d2 = jnp.sum(jnp.where(oh2, slot, 0.0), axis=0, keepdims=True)
    cnt_ref[...] = cnt + jnp.sum(onehot, axis=1, keepdims=True)
    dest_ref[0, 0:1, :] = d1.astype(I32)
    dest_ref[0, 1:2, :] = d2.astype(I32)
    gate_ref[0] = choice[2:4, :]


def _route(sel, totals, nbpad):
    nt, _, tm = sel.shape
    ne = totals.shape[0]
    return pl.pallas_call(
        functools.partial(_route_kernel, nbpad=nbpad),
        out_shape=(jax.ShapeDtypeStruct((nt, 2, tm), I32), jax.ShapeDtypeStruct((nt, 2, tm), F32),
                   jax.ShapeDtypeStruct((2, nbpad), I32)),
        grid=(nt,),
        in_specs=[pl.BlockSpec((1, 4, tm), lambda i: (i, 0, 0)),
                  pl.BlockSpec((ne, LANES), lambda i: (0, 0))],
        out_specs=(pl.BlockSpec((1, 2, tm), lambda i: (i, 0, 0)),
                   pl.BlockSpec((1, 2, tm), lambda i: (i, 0, 0)),
                   pl.BlockSpec((2, nbpad), lambda i: (0, 0))),
        scratch_shapes=[pltpu.VMEM((ne, LANES), F32), pltpu.VMEM((ne, LANES), F32)],
        compiler_params=_cparams(("arbitrary",)),
        name="router_slots",
    )(sel, totals)


SC_CORES, SC_SUBCORES = 2, 16
SC_ROWS = 64


def _sc_mesh():
    return plsc.VectorSubcoreMesh(core_axis_name="c", subcore_axis_name="s",
                                  num_cores=SC_CORES, num_subcores=SC_SUBCORES)


def _sc_worker_base(rows_per_worker):
    return (lax.axis_index("s") * SC_CORES + lax.axis_index("c")) * rows_per_worker


def _sc_scatter_rows(src, idx, n_out):
    t, d = src.shape
    assert idx.shape[0] == TOP_K * t
    per_w = t // (SC_CORES * SC_SUBCORES)
    assert per_w % (2 * SC_ROWS) == 0
    rows = pltpu.VMEM((SC_ROWS, d), src.dtype)
    ids = pltpu.VMEM((SC_ROWS,), I32)

    @functools.partial(
        pl.kernel, mesh=_sc_mesh(), out_type=jax.ShapeDtypeStruct((n_out, d), src.dtype),
        scratch_types=[rows, rows, ids, ids, ids, ids] + [pltpu.SemaphoreType.DMA] * 4,
        compiler_params=pltpu.CompilerParams(use_tc_tiling_on_sc=True), name="moe_dispatch_sc")
    def k(src_hbm, idx_hbm, out_hbm, rows_a, rows_b, ia0, ia1, ib0, ib1, s0, s1, s2, s3):
        base = _sc_worker_base(per_w)

        @pl.loop(0, per_w // (2 * SC_ROWS))
        def _(j):
            off_a = pl.multiple_of(base + 2 * j * SC_ROWS, SC_ROWS)
            off_b = pl.multiple_of(off_a + SC_ROWS, SC_ROWS)
            load_a = pltpu.async_copy(src_hbm.at[pl.ds(off_a, SC_ROWS)], rows_a, s0)
            load_b = pltpu.async_copy(src_hbm.at[pl.ds(off_b, SC_ROWS)], rows_b, s1)
            pltpu.sync_copy(idx_hbm.at[pl.ds(off_a, SC_ROWS)], ia0)
            pltpu.sync_copy(idx_hbm.at[pl.ds(t + off_a, SC_ROWS)], ia1)
            pltpu.sync_copy(idx_hbm.at[pl.ds(off_b, SC_ROWS)], ib0)
            pltpu.sync_copy(idx_hbm.at[pl.ds(t + off_b, SC_ROWS)], ib1)
            load_a.wait()
            put_a0 = pltpu.async_copy(rows_a, out_hbm.at[ia0], s0)
            put_a1 = pltpu.async_copy(rows_a, out_hbm.at[ia1], s2)
            load_b.wait()
            put_b0 = pltpu.async_copy(rows_b, out_hbm.at[ib0], s1)
            put_b1 = pltpu.async_copy(rows_b, out_hbm.at[ib1], s3)
            put_a0.wait()
            put_a1.wait()
            put_b0.wait()
            put_b1.wait()

    return k(src, idx)


def _sc_gather_rows(table, idx):
    _, d = table.shape
    b = idx.shape[0]
    per_w = b // (SC_CORES * SC_SUBCORES)
    assert per_w % (2 * SC_ROWS) == 0
    rows = pltpu.VMEM((SC_ROWS, d), table.dtype)
    ids = pltpu.VMEM((SC_ROWS,), I32)

    @functools.partial(
        pl.kernel, mesh=_sc_mesh(), out_type=jax.ShapeDtypeStruct((b, d), table.dtype),
        scratch_types=[rows, rows, ids, ids] + [pltpu.SemaphoreType.DMA] * 2,
        compiler_params=pltpu.CompilerParams(use_tc_tiling_on_sc=True), name="moe_gather_sc")
    def k(table_hbm, idx_hbm, out_hbm, rows_a, rows_b, ia, ib, s0, s1):
        base = _sc_worker_base(per_w)

        @pl.loop(0, per_w // (2 * SC_ROWS))
        def _(j):
            off_a = pl.multiple_of(base + 2 * j * SC_ROWS, SC_ROWS)
            off_b = pl.multiple_of(off_a + SC_ROWS, SC_ROWS)
            pltpu.sync_copy(idx_hbm.at[pl.ds(off_a, SC_ROWS)], ia)
            pltpu.sync_copy(idx_hbm.at[pl.ds(off_b, SC_ROWS)], ib)
            get_a = pltpu.async_copy(table_hbm.at[ia], rows_a, s0)
            get_b = pltpu.async_copy(table_hbm.at[ib], rows_b, s1)
            get_a.wait()
            put_a = pltpu.async_copy(rows_a, out_hbm.at[pl.ds(off_a, SC_ROWS)], s0)
            get_b.wait()
            put_b = pltpu.async_copy(rows_b, out_hbm.at[pl.ds(off_b, SC_ROWS)], s1)
            put_a.wait()
            put_b.wait()

    return k(table, idx)


def _expert_kernel(blk_ref, nused_ref, first_ref, slot_ref, nxt_ref, xs_ref, wg_hbm, wu_hbm, wd_hbm, y_ref,
                   wg_f, wu_f, wd_f, wgb, wub, wdb, sem, *, layer):
    b = pl.program_id(0)

    def fetch(e, s):
        return [pltpu.make_async_copy(w_hbm.at[layer, e], stage.at[s], sem.at[s, k])
                for k, (w_hbm, stage) in enumerate(((wg_hbm, wg_f), (wu_hbm, wu_f), (wd_hbm, wd_f)))]

    @pl.when(b == 0)
    def _():
        for cp in fetch(blk_ref[0], 0):
            cp.start()

    @pl.when(first_ref[b] == 1)
    def _():
        s = slot_ref[b]
        for cp in fetch(blk_ref[b], s):
            cp.wait()

        @pl.when(nxt_ref[b] >= 0)
        def _():
            for cp in fetch(nxt_ref[b], 1 - s):
                cp.start()

        wgb[...] = wg_f[s].astype(BF16)
        wub[...] = wu_f[s].astype(BF16)
        wdb[...] = wd_f[s].astype(BF16)

    @pl.when(b < nused_ref[0])
    def _():
        xb = _unpack_pair(xs_ref[...]).astype(BF16)
        g = _dot(xb, wgb[...])
        u = _dot(xb, wub[...])
        a = (_silu(g) * u).astype(BF16)
        y_ref[...] = _pack_pair(_dot(a, wdb[...]))

    @pl.when(b >= nused_ref[0])
    def _():
        y_ref[...] = jnp.zeros_like(y_ref)


def _experts(xs, blk_e, nused, w_gate, w_up, w_down, layer):
    np_rows, dp = xs.shape
    _, ne, d, ff = w_gate.shape
    nb = np_rows // MOE_BLOCK
    pos = jnp.arange(nb, dtype=I32)
    first = jnp.concatenate([jnp.ones((1,), I32), (blk_e[1:] != blk_e[:-1]).astype(I32)])
    slot = (jnp.cumsum(first) - 1) % 2
    later = lax.cummin(jnp.where(first == 1, pos, nb)[::-1])[::-1]
    nxt_pos = jnp.concatenate([later[1:], jnp.full((1,), nb, I32)])
    nxt_e = jnp.where(nxt_pos < nb, blk_e[jnp.minimum(nxt_pos, nb - 1)], -1).astype(I32)
    grid_spec = pltpu.PrefetchScalarGridSpec(
        num_scalar_prefetch=5,
        grid=(nb,),
        in_specs=[pl.BlockSpec((MOE_BLOCK, dp), lambda b, *_: (b, 0)),
                  pl.BlockSpec(memory_space=pl.ANY),
                  pl.BlockSpec(memory_space=pl.ANY),
                  pl.BlockSpec(memory_space=pl.ANY)],
        out_specs=pl.BlockSpec((MOE_BLOCK, dp), lambda b, *_: (b, 0)),
        scratch_shapes=[pltpu.VMEM((2, d, ff), F32), pltpu.VMEM((2, d, ff), F32), pltpu.VMEM((2, ff, d), F32),
                        pltpu.VMEM((d, ff), BF16), pltpu.VMEM((d, ff), BF16), pltpu.VMEM((ff, d), BF16),
                        pltpu.SemaphoreType.DMA((2, 3))],
    )
    return pl.pallas_call(
        functools.partial(_expert_kernel, layer=layer),
        out_shape=jax.ShapeDtypeStruct((np_rows, dp), PACKED),
        grid_spec=grid_spec,
        compiler_params=_cparams(("arbitrary",)),
        name="moe_experts",
    )(blk_e, nused, first, slot.astype(I32), nxt_e, xs, w_gate, w_up, w_down)


def _moe_residual(x1, y1_ref, y2_ref, gates, out_gate):
    tc, d = x1.shape
    acc = jnp.zeros((tc, d), F32)
    for k, y_ref in enumerate((y1_ref, y2_ref)):
        gcol = jnp.transpose(jnp.broadcast_to(gates[k:k + 1, :], (LANES, tc)))
        acc = acc + _unpack_pair(y_ref[...]) * jnp.concatenate([gcol] * (d // LANES), axis=1)
    return x1 + out_gate * acc


def _combine_kernel(x1_ref, y1_ref, y2_ref, gate_ref, mod_ref, fg_ref, o_ref, *, final):
    x2 = _moe_residual(x1_ref[...], y1_ref, y2_ref, gate_ref[0], mod_ref[0][5:6, :])
    if final:
        x2 = _rms(x2) * fg_ref[...]
    o_ref[...] = x2


def _combine(x1, y12, gates, mod_l, final_gain, tc, seq, final):
    t, d = x1.shape
    nt = t // tc
    tiles_per_b = seq // tc
    return pl.pallas_call(
        functools.partial(_combine_kernel, final=final),
        out_shape=jax.ShapeDtypeStruct((t, d), F32),
        grid=(nt,),
        in_specs=[pl.BlockSpec((tc, d), lambda i: (i, 0)),
                  pl.BlockSpec((tc, _packed_width(d)), lambda i: (i, 0)),
                  pl.BlockSpec((tc, _packed_width(d)), lambda i: (nt + i, 0)),
                  pl.BlockSpec((1, TOP_K, tc), lambda i: (i, 0, 0)),
                  pl.BlockSpec((1, 6, d), lambda i: (i // tiles_per_b, 0, 0)),
                  pl.BlockSpec((1, d), lambda i: (0, 0))],
        out_specs=pl.BlockSpec((tc, d), lambda i: (i, 0)),
        compiler_params=_cparams(("arbitrary",)),
        name="moe_combine",
    )(x1, y12, y12, gates, mod_l, final_gain)


ROUTE_TILE = 512


def _moe(h2, sel, totals, w_gate, w_up, w_down, layer):
    t = h2.shape[0]
    ne = w_gate.shape[1]
    nb = (t * TOP_K + ne * (MOE_BLOCK - 1) + MOE_BLOCK - 1) // MOE_BLOCK
    nbpad = -(-nb // LANES) * LANES
    dest, gates, blk = _route(sel, totals, nbpad)
    slot = jnp.transpose(dest, (1, 0, 2)).reshape(TOP_K * t)
    xs = _sc_scatter_rows(h2, slot, nb * MOE_BLOCK)
    ys = _experts(xs, blk[0, :nb], blk[1, :1], w_gate, w_up, w_down, layer)
    return _sc_gather_rows(ys, slot), gates


def kernel(x, c, positions, mod_w, mod_b, norm_mix, norm_ffn, ev_w_in, ev_ret_norm, ev_gla_wa2, ev_gla_ba,
           ev_gla_norm, ev_w_out, od_w_in, od_conv_w, od_conv_b, od_dt_bias, od_a_log, od_d, od_norm, od_w_out,
           router_w, router_b, exp_w_gate, exp_w_up, exp_w_down, final_norm):
    bsz, seq, d = x.shape
    depth = mod_w.shape[0]
    t = bsz * seq
    ne = router_w.shape[1]
    epg = ne // N_GROUPS
    mod = _modulation(c, mod_w, mod_b)
    x2d = x.reshape(t, d)

    perm = np.array([g * epg + j for j in range(epg) for g in range(N_GROUPS)])
    rw_t = router_w.T[perm].astype(BF16)
    rb_b = jnp.broadcast_to(router_b.astype(F32)[perm][:, None], (ne, LANES))
    final_gain = final_norm.reshape(1, d)

    half = RET_DK // 2
    inv = ROPE_BASE ** (-jnp.arange(half, dtype=F32) / half)
    inv2 = jnp.concatenate([inv, inv]).reshape(1, LANES)
    pos_col = positions.reshape(t, 1)

    pending = None
    for layer in range(depth):
        i = layer // 2
        mod_l = mod[layer]
        gain1 = norm_mix[layer].reshape(1, d)
        gain2 = norm_ffn[layer].reshape(1, d)
        if layer % 2 == 0:
            if pending is not None:
                x2d = _combine(*pending, final_gain, ROUTE_TILE, seq, final=False)
            w_in = ev_w_in[i]
            nmain = w_in.shape[1] - GLA_RANK
            w_main = w_in[:, :nmain].astype(BF16)
            w_ga = jnp.pad(w_in[:, nmain:], ((0, 0), (0, LANES - GLA_RANK))).astype(BF16)
            wa2 = jnp.pad(ev_gla_wa2[i], ((0, LANES - GLA_RANK), (0, 0))).astype(BF16)
            y, glog = _even_proj(x2d, mod_l, gain1, pos_col, inv2, w_main, w_ga, wa2,
                                 ev_gla_ba[i].reshape(1, -1), PROJ_TILE, seq)
            o = _ret_gla(y, glog, ev_ret_norm[i].reshape(1, -1), ev_gla_norm[i].reshape(1, -1), bsz, seq, 512)
            w_out = ev_w_out[i].astype(BF16)
        else:
            assert pending is not None
            w_in = od_w_in[i]
            nheads = od_a_log.shape[1]
            nmain = w_in.shape[1] - nheads
            dinner = od_norm.shape[1]
            gw, ns_ = dinner // SSD_GROUPS, SSD_STATE
            bc0 = dinner + SSD_GROUPS * ns_

            def slabs(a, off=0):
                return [a[..., off + lo:off + lo + n] for g in range(SSD_GROUPS)
                        for lo, n in ((g * gw, gw), (dinner + g * ns_, ns_), (bc0 + g * ns_, ns_))]

            conv_w = jnp.concatenate(slabs(od_conv_w[i]), axis=-1)
            conv_b = jnp.concatenate(slabs(od_conv_b[i]), axis=-1).reshape(1, -1)
            w_main = jnp.concatenate(slabs(w_in, dinner) + [w_in[:, :dinner]], axis=1).astype(BF16)
            w_dt = jnp.pad(w_in[:, nmain:], ((0, 0), (0, LANES - nheads))).astype(BF16)
            dtb = jnp.pad(od_dt_bias[i], (0, LANES - nheads)).reshape(1, LANES)
            y, dt, x2d = _odd_proj(*pending, mod_l, gain1, w_main, w_dt, dtb, od_conv_w.shape[2], ROUTE_TILE, seq)
            alog = jnp.pad(od_a_log[i], (0, LANES - nheads)).reshape(1, LANES)
            d_row = jnp.repeat(od_d[i], SSD_HEADDIM).reshape(1, dinner)
            o = _ssd(y, dt, conv_w, conv_b, alog, d_row, od_norm[i].reshape(1, dinner), bsz, seq)
            w_out = od_w_out[i].astype(BF16)
        x1, h2, sel, totals = _outproj(o, x2d, mod_l, gain2, w_out, rw_t, rb_b, ROUTE_TILE, seq)
        y12, gates = _moe(h2, sel, totals, exp_w_gate, exp_w_up, exp_w_down, layer)
        pending = (x1, y12, gates, mod_l)
    out = _combine(*pending, final_gain, ROUTE_TILE, seq, final=True)
    return out.reshape(bsz, seq, d)
```

```python
import functools

import jax
import jax.numpy as jnp
import numpy as np
from jax import lax
from jax.experimental import pallas as pl
from jax.experimental.pallas import tpu as pltpu
from jax.experimental.pallas import tpu_sc as plsc

F32 = jnp.float32
BF16 = jnp.bfloat16
I32 = jnp.int32

RET_HEADS, RET_DK, RET_DV, RET_CHUNK = 4, 128, 256, 128
ROPE_BASE = 10000.0
GLA_HEADS, GLA_DK, GLA_DV, GLA_RANK, GLA_TAU, GLA_CHUNK = 4, 128, 256, 16, 16.0, 64
SSD_HEADDIM, SSD_GROUPS, SSD_STATE, SSD_CONV, SSD_CHUNK = 64, 8, 128, 4, 128
N_GROUPS, TOP_K, GROUP_SCORE_K = 8, 2, 2
MOE_BLOCK = 512
NORM_EPS = 1e-6

PROJ_TILE = 512
LANES = 128
VMEM_LIMIT = 56 * 1024 * 1024


def _cparams(sem, vmem=VMEM_LIMIT):
    return pltpu.CompilerParams(dimension_semantics=sem, vmem_limit_bytes=vmem)


def _dot(a, b):
    return jnp.dot(a, b, preferred_element_type=F32)


def _dot_nt(a, b):
    return lax.dot_general(a, b, (((1,), (1,)), ((), ())), preferred_element_type=F32)


def _dot_tn(a, b):
    return lax.dot_general(a, b, (((0,), (0,)), ((), ())), preferred_element_type=F32)


def _split3(x):
    a = x.astype(BF16)
    r = x - a.astype(F32)
    b = r.astype(BF16)
    c = (r - b.astype(F32)).astype(BF16)
    return a, b, c


def _dot_exact_rhs(m01, x):
    a, b, c = _split3(x)
    return _dot(m01, a) + _dot(m01, b) + _dot(m01, c)


def _dot_exact_lhs(x, m01):
    a, b, c = _split3(x)
    return _dot(a, m01) + _dot(b, m01) + _dot(c, m01)


def _dot_tn_exact(x, m01):
    a, b, c = _split3(x)
    return _dot_tn(a, m01) + _dot_tn(b, m01) + _dot_tn(c, m01)


def _silu(x):
    return x * jax.nn.sigmoid(x)


def _rms(x, eps=NORM_EPS):
    return x * lax.rsqrt(jnp.mean(x * x, axis=-1, keepdims=True) + eps)


def _iota(shape, dim):
    return lax.broadcasted_iota(I32, shape, dim)


def _interleave(items, starts_per_round):
    waiting, running = list(items), []
    while waiting or running:
        running += [waiting.pop(0) for _ in range(min(starts_per_round, len(waiting)))]
        for item in reversed(list(running)):
            if next(item, StopIteration) is StopIteration:
                running.remove(item)


PACKED = jnp.uint32
_HI16 = 0xFFFF0000


def _packed_width(n):
    return n // 2


def _pack_pair(x):
    n = x.shape[1] // 2
    lo = lax.bitcast_convert_type(x[:, :n].astype(BF16).astype(F32), jnp.uint32)
    hi = lax.bitcast_convert_type(x[:, n:].astype(BF16).astype(F32), jnp.uint32)
    return (hi & jnp.uint32(_HI16)) | (lo >> 16)


def _unpack_pair(w):
    lo = lax.bitcast_convert_type(w << 16, F32)
    hi = lax.bitcast_convert_type(w & jnp.uint32(_HI16), F32)
    return jnp.concatenate([lo, hi], axis=1)


def _mod_kernel(c_ref, w_ref, b_ref, o_ref):
    sc = _silu(c_ref[...])
    o_ref[0] = _dot(sc.astype(BF16), w_ref[0].astype(BF16)) + b_ref[0]


def _modulation(c, mod_w, mod_b):
    depth, d, d6 = mod_w.shape
    bsz = c.shape[0]
    nb = d6 // d
    out = pl.pallas_call(
        _mod_kernel,
        out_shape=jax.ShapeDtypeStruct((depth, bsz, d6), F32),
        grid=(depth, nb),
        in_specs=[pl.BlockSpec((bsz, d), lambda l, j: (0, 0)),
                  pl.BlockSpec((1, d, d), lambda l, j: (l, 0, j)),
                  pl.BlockSpec((1, 1, d), lambda l, j: (l, 0, j))],
        out_specs=pl.BlockSpec((1, bsz, d), lambda l, j: (l, 0, j)),
        compiler_params=_cparams(("arbitrary", "arbitrary")),
        name="adaln_mod",
    )(c, mod_w, mod_b.reshape(depth, 1, d6))
    return out.reshape(depth, bsz, nb, d)


def _norm_mod(x, gain, mod, shift_row, scale_row):
    y = _rms(x) * gain
    return y * (1.0 + mod[scale_row:scale_row + 1, :]) + mod[shift_row:shift_row + 1, :]


def _even_proj_kernel(x_ref, mod_ref, gain_ref, pos_ref, inv_ref, w_ref, wga_ref, wa2_ref, ba_ref,
                      y_ref, glog_ref, *, ncol):
    h = _norm_mod(x_ref[...], gain_ref[...], mod_ref[0], 0, 1)
    hb = h.astype(BF16)
    tm = hb.shape[0]
    hm = tm // 2
    half = LANES // 2
    lo = _iota((hm, LANES), 1) < half
    ang = jnp.where(lo, pos_ref[0:hm, :], pos_ref[hm:tm, :]).astype(F32) * inv_ref[...]
    c2, s2 = jnp.cos(ang), jnp.sin(ang)
    c2r, s2r = pltpu.roll(c2, half, 1), pltpu.roll(s2, half, 1)
    cos = jnp.concatenate([jnp.where(lo, c2, c2r), jnp.where(lo, c2r, c2)], axis=0)
    sin_a, sin_b = jnp.where(lo, s2, s2r), jnp.where(lo, s2r, s2)
    sin = jnp.concatenate([jnp.where(lo, -sin_a, sin_a), jnp.where(lo, -sin_b, sin_b)], axis=0)
    qk_scale = RET_DK ** -0.5
    rqk = RET_HEADS * RET_DK
    rv = RET_HEADS * RET_DV
    gq0 = 2 * rqk + 2 * rv
    width = w_ref.shape[1] // ncol
    for j in range(ncol):
        c0 = j * width
        acc = _dot(hb, w_ref[:, c0:c0 + width])
        for s in range(width // LANES):
            col = c0 + s * LANES
            blk = acc[:, s * LANES:(s + 1) * LANES]
            if col < 2 * rqk:
                blk = blk * cos + pltpu.roll(blk, LANES // 2, 1) * sin
                if col >= rqk:
                    blk = blk * qk_scale
            elif gq0 <= col < gq0 + GLA_HEADS * GLA_DK:
                blk = blk * (GLA_DK ** -0.5)
            y_ref[:, col:col + LANES] = blk.astype(y_ref.dtype)
    ga = _dot(hb, wga_ref[...])
    z = _dot(ga.astype(BF16), wa2_ref[...]) + ba_ref[...]
    logsig = jnp.minimum(z, 0.0) - jnp.log1p(jnp.exp(-jnp.abs(z)))
    glog_ref[...] = logsig * (1.0 / GLA_TAU)


def _even_proj(x2d, mod_l, gain, pos_col, inv2, w_main, w_ga, wa2, ba, tm, seq):
    t, d = x2d.shape
    n = w_main.shape[1]
    tiles_per_b = seq // tm
    gk = wa2.shape[1]
    const = lambda i: (0, 0)
    return pl.pallas_call(
        functools.partial(_even_proj_kernel, ncol=n // 512),
        out_shape=(jax.ShapeDtypeStruct((t, n), BF16), jax.ShapeDtypeStruct((t, gk), F32)),
        grid=(t // tm,),
        in_specs=[pl.BlockSpec((tm, d), lambda i: (i, 0)),
                  pl.BlockSpec((1, 6, d), lambda i: (i // tiles_per_b, 0, 0)),
                  pl.BlockSpec((1, d), const),
                  pl.BlockSpec((tm, 1), lambda i: (i, 0)),
                  pl.BlockSpec((1, LANES), const),
                  pl.BlockSpec((d, n), const, pipeline_mode=pl.Buffered(1)),
                  pl.BlockSpec((d, LANES), const),
                  pl.BlockSpec((LANES, gk), const),
                  pl.BlockSpec((1, gk), const)],
        out_specs=(pl.BlockSpec((tm, n), lambda i: (i, 0)), pl.BlockSpec((tm, gk), lambda i: (i, 0))),
        compiler_params=_cparams(("arbitrary",)),
        name="even_proj",
    )(x2d, mod_l, gain, pos_col, inv2, w_main, w_ga, wa2, ba)


RET_GLA_UNROLL = 4
RET_GLA_STARTS = 4


def _ret_gla_kernel(rq_ref, rk_ref, rv_ref, rg_ref, gq_ref, gk_ref, gv_ref, gg_ref, gl_ref,
                    rn_ref, gn_ref, dec_ref, zeta_ref, xi_ref, o_ref, rst_ref, gst_ref, *, chunk_decay):
    @pl.when(pl.program_id(1) == 0)
    def _():
        rst_ref[...] = jnp.zeros_like(rst_ref)
        gst_ref[...] = jnp.zeros_like(gst_ref)

    ls = rq_ref.shape[0]
    lr, lg = RET_CHUNK, GLA_CHUNK
    trow = _iota((lg, lg), 0)
    tcol = _iota((lg, lg), 1)
    causal = trow >= tcol
    tri = causal.astype(BF16)
    base = RET_HEADS * RET_DV

    def ret_item(r0, h):
        q = rq_ref[pl.ds(r0, lr), h * RET_DK:(h + 1) * RET_DK]
        k = rk_ref[pl.ds(r0, lr), h * RET_DK:(h + 1) * RET_DK]
        v = rv_ref[pl.ds(r0, lr), h * RET_DV:(h + 1) * RET_DV]
        s = _dot_nt(q, k) * dec_ref[h]
        qx = (q.astype(F32) * xi_ref[h]).astype(BF16)
        kz = (k.astype(F32) * zeta_ref[h]).astype(BF16)
        yield
        inner = _dot(s.astype(BF16), v)
        kv = _dot_tn(kz, v)
        yield
        gate = _silu(rg_ref[pl.ds(r0, lr), h * RET_DV:(h + 1) * RET_DV].astype(F32))
        yield
        st = rst_ref[h]
        cross = _dot(qx, st.astype(BF16))
        rst_ref[h] = st * chunk_decay[h] + kv
        o = _rms(inner + cross) * rn_ref[:, h * RET_DV:(h + 1) * RET_DV] * gate
        o_ref[pl.ds(r0, lr), h * RET_DV:(h + 1) * RET_DV] = o.astype(o_ref.dtype)

    def gla_item(r0, h):
        gl = gl_ref[pl.ds(r0, lg), h * GLA_DK:(h + 1) * GLA_DK]
        b = _dot_exact_rhs(tri, gl)
        yield
        q = gq_ref[pl.ds(r0, lg), h * GLA_DK:(h + 1) * GLA_DK].astype(F32)
        k = gk_ref[pl.ds(r0, lg), h * GLA_DK:(h + 1) * GLA_DK].astype(F32)
        bref = b[lg // 2:lg // 2 + 1, :]
        blast = b[lg - 1:lg, :]
        att = _dot_nt((q * jnp.exp(b - bref)).astype(BF16), (k * jnp.exp(bref - b)).astype(BF16))
        qb = (q * jnp.exp(b)).astype(BF16)
        kd = (k * jnp.exp(blast - b)).astype(BF16)
        dec = jnp.exp(blast)
        yield
        v = gv_ref[pl.ds(r0, lg), h * GLA_DV:(h + 1) * GLA_DV]
        inner = _dot(jnp.where(causal, att, 0.0).astype(BF16), v)
        kvt = _dot_tn(v, kd)
        gate = _silu(gg_ref[pl.ds(r0, lg), h * GLA_DV:(h + 1) * GLA_DV].astype(F32))
        yield
        st = gst_ref[h]
        cross = _dot_nt(qb, st.astype(BF16))
        gst_ref[h] = st * dec + kvt
        o = _rms(inner + cross) * gn_ref[:, h * GLA_DV:(h + 1) * GLA_DV] * gate
        o_ref[pl.ds(r0, lg), base + h * GLA_DV:base + (h + 1) * GLA_DV] = o.astype(o_ref.dtype)

    def step(c, carry):
        waiting = []
        for u in range(RET_GLA_UNROLL):
            r0 = pl.multiple_of((c * RET_GLA_UNROLL + u) * lr, lr)
            waiting += [ret_item(r0, h) for h in range(RET_HEADS)]
            for j in range(lr // lg):
                rj = pl.multiple_of(r0 + j * lg, lg)
                waiting += [gla_item(rj, h) for h in range(GLA_HEADS)]
        assert RET_HEADS % RET_GLA_STARTS == 0 and GLA_HEADS % RET_GLA_STARTS == 0
        _interleave(waiting, RET_GLA_STARTS)
        return carry

    lax.fori_loop(0, ls // (lr * RET_GLA_UNROLL), step, 0)


def _ret_gla(y, glog, ret_norm, gla_norm, bsz, seq, ls):
    t = y.shape[0]
    ns = seq // ls
    qw, vw = RET_HEADS * RET_DK, RET_HEADS * RET_DV
    L = RET_CHUNK
    log_gamma = jnp.log1p(-jnp.exp2(-5.0 - jnp.arange(RET_HEADS, dtype=F32)))
    idx = jnp.arange(L, dtype=F32)
    diff = idx[:, None] - idx[None, :]
    decay = jnp.where(diff >= 0, jnp.exp(log_gamma[:, None, None] * jnp.maximum(diff, 0.0)), 0.0)
    zeta = jnp.broadcast_to(jnp.exp(log_gamma[:, None] * (L - 1 - idx)[None, :])[:, :, None], (RET_HEADS, L, RET_DK))
    xi = jnp.broadcast_to(jnp.exp(log_gamma[:, None] * (idx + 1.0)[None, :])[:, :, None], (RET_HEADS, L, RET_DK))
    chunk_decay = tuple(float(np.exp(np.float32(np.log1p(-np.exp2(np.float32(-5.0 - i)))) * np.float32(L)))
                        for i in range(RET_HEADS))
    rowmap = lambda j: (lambda b, i: (b * ns + i, j))
    const = lambda b, i: (0, 0)
    const3 = lambda b, i: (0, 0, 0)
    return pl.pallas_call(
        functools.partial(_ret_gla_kernel, chunk_decay=chunk_decay),
        out_shape=jax.ShapeDtypeStruct((t, 2 * vw), BF16),
        grid=(bsz, ns),
        in_specs=[pl.BlockSpec((ls, qw), rowmap(0)),
                  pl.BlockSpec((ls, qw), rowmap(1)),
                  pl.BlockSpec((ls, vw), rowmap(1)),
                  pl.BlockSpec((ls, vw), rowmap(2)),
                  pl.BlockSpec((ls, qw), rowmap(6)),
                  pl.BlockSpec((ls, qw), rowmap(7)),
                  pl.BlockSpec((ls, vw), rowmap(4)),
                  pl.BlockSpec((ls, vw), rowmap(5)),
                  pl.BlockSpec((ls, qw), rowmap(0)),
                  pl.BlockSpec((1, vw), const),
                  pl.BlockSpec((1, vw), const),
                  pl.BlockSpec((RET_HEADS, L, L), const3),
                  pl.BlockSpec((RET_HEADS, L, RET_DK), const3),
                  pl.BlockSpec((RET_HEADS, L, RET_DK), const3)],
        out_specs=pl.BlockSpec((ls, 2 * vw), lambda b, i: (b * ns + i, 0)),
        scratch_shapes=[pltpu.VMEM((RET_HEADS, RET_DK, RET_DV), F32),
                        pltpu.VMEM((GLA_HEADS, GLA_DV, GLA_DK), F32)],
        compiler_params=_cparams(("arbitrary", "arbitrary")),
        name="ret_gla_mixer",
    )(y, y, y, y, y, y, y, y, glog, ret_norm, gla_norm, decay, zeta, xi)


def _odd_proj_kernel(x1_ref, y1_ref, y2_ref, gate_ref, modp_ref, mod_ref, gain_ref, w_ref, wdt_ref, dtb_ref,
                     y_ref, dt_ref, x_ref, *, ncol, cch):
    x = _moe_residual(x1_ref[...], y1_ref, y2_ref, gate_ref[0], modp_ref[0][5:6, :])
    x_ref[...] = x
    h = _norm_mod(x, gain_ref[...], mod_ref[0], 0, 1)
    hb = h.astype(BF16)
    width = w_ref.shape[1] // ncol
    for j in range(ncol):
        c0 = j * width
        acc = _dot(hb, w_ref[:, c0:c0 + width])
        if c0 >= cch:
            acc = _silu(acc)
        y_ref[:, c0:c0 + width] = acc.astype(y_ref.dtype)
    z = _dot(hb, wdt_ref[...]) + dtb_ref[...]
    dt_ref[...] = jnp.maximum(z, 0.0) + jnp.log1p(jnp.exp(-jnp.abs(z)))


def _odd_proj(x1, y12, gates, mod_prev, mod_l, gain, w_main, w_dt, dt_bias, cch, tm, seq):
    t, d = x1.shape
    n = w_main.shape[1]
    nt = t // tm
    tiles_per_b = seq // tm
    const = lambda i: (0, 0)
    batch = lambda i: (i // tiles_per_b, 0, 0)
    return pl.pallas_call(
        functools.partial(_odd_proj_kernel, ncol=n // 512, cch=cch),
        out_shape=(jax.ShapeDtypeStruct((t, n), BF16), jax.ShapeDtypeStruct((t, LANES), F32),
                   jax.ShapeDtypeStruct((t, d), F32)),
        grid=(nt,),
        in_specs=[pl.BlockSpec((tm, d), lambda i: (i, 0)),
                  pl.BlockSpec((tm, _packed_width(d)), lambda i: (i, 0)),
                  pl.BlockSpec((tm, _packed_width(d)), lambda i: (nt + i, 0)),
                  pl.BlockSpec((1, TOP_K, tm), lambda i: (i, 0, 0)),
                  pl.BlockSpec((1, 6, d), batch),
                  pl.BlockSpec((1, 6, d), batch),
                  pl.BlockSpec((1, d), const),
                  pl.BlockSpec((d, n), const, pipeline_mode=pl.Buffered(1)),
                  pl.BlockSpec((d, LANES), const),
                  pl.BlockSpec((1, LANES), const)],
        out_specs=(pl.BlockSpec((tm, n), lambda i: (i, 0)), pl.BlockSpec((tm, LANES), lambda i: (i, 0)),
                   pl.BlockSpec((tm, d), lambda i: (i, 0))),
        compiler_params=_cparams(("arbitrary",)),
        name="odd_proj",
    )(x1, y12, y12, gates, mod_prev, mod_l, gain, w_main, w_dt, dt_bias)


def _ssd_kernel(z_ref, xbc_ref, dt_ref, cw_ref, cb_ref, alog_ref, dsk_ref, ng_ref, o_ref, hist_ref, st_ref):
    @pl.when(pl.program_id(1) == 0)
    def _():
        hist_ref[...] = jnp.zeros_like(hist_ref)
        st_ref[...] = jnp.zeros_like(st_ref)

    L = SSD_CHUNK
    nchunks = xbc_ref.shape[0] // L
    P, N, G = SSD_HEADDIM, SSD_STATE, SSD_GROUPS
    dinner = z_ref.shape[1]
    gw = dinner // G
    hpg = gw // P
    slab = gw + 2 * N
    hrows = hist_ref.shape[0]

    trow = _iota((L, L), 0)
    tcol = _iota((L, L), 1)
    causal = trow >= tcol

    def discretise(c):
        dt = dt_ref[c * L:(c + 1) * L, :]
        da = dt * (-jnp.exp(alog_ref[...]))
        cum = _dot_exact_rhs(causal.astype(BF16), da)
        wend = dt * jnp.exp(cum[L - 1:L, :] - cum)
        return cum, jnp.transpose(cum), jnp.transpose(dt), jnp.exp(cum), wend

    heads = [discretise(c) for c in range(nchunks)]
    lane = _iota((L, LANES), 1)
    lo_half = lane < P

    def pair_cols(a, ha):
        return jnp.where(lo_half, jnp.broadcast_to(a[:, ha:ha + 1], (L, LANES)),
                         jnp.broadcast_to(a[:, ha + 1:ha + 2], (L, LANES)))

    srow = _iota((L, hrows + L), 0)
    scol = _iota((L, hrows + L), 1)
    shift = jnp.concatenate([(scol == srow + (hrows - s)).astype(BF16) for s in range(1, SSD_CONV)], axis=0)

    def group_item(c, g):
        cum, cum_t, dt_t, ecum, wend = heads[c]
        r0 = c * L
        c0 = g * slab
        xin = xbc_ref[r0:r0 + L, c0:c0 + slab]
        prev = hist_ref[:, c0:c0 + slab] if c == 0 else xbc_ref[r0 - hrows:r0, c0:c0 + slab]
        xcat = jnp.concatenate([prev, xin], axis=0)
        if c == nchunks - 1:
            hist_ref[:, c0:c0 + slab] = xin[L - hrows:L]
        sh = _dot(shift, xcat)
        conv = xin.astype(F32) * cw_ref[SSD_CONV - 1:SSD_CONV, c0:c0 + slab] + cb_ref[:, c0:c0 + slab]
        for s in range(1, SSD_CONV):
            conv = conv + sh[(s - 1) * L:s * L] * cw_ref[SSD_CONV - 1 - s:SSD_CONV - s, c0:c0 + slab]
        act = _silu(conv)
        xs = act[:, :gw]
        bm_g = act[:, gw:gw + N].astype(BF16)
        cm_g = act[:, gw + N:].astype(BF16)
        yield
        cb = _dot_nt(cm_g, bm_g)
        lhs, xw, ec = [], [], []
        for pr in range(hpg // 2):
            ha = g * hpg + 2 * pr
            ms = []
            for hh in (ha, ha + 1):
                seg = jnp.broadcast_to(cum[:, hh:hh + 1], (L, L)) - cum_t[hh:hh + 1, :]
                ms.append((cb * jnp.where(causal, jnp.exp(seg), 0.0) * dt_t[hh:hh + 1, :]).astype(BF16))
            lhs.append(jnp.concatenate(ms, axis=1))
            xw.append((xs[:, pr * 2 * P:(pr + 1) * 2 * P] * pair_cols(wend, ha)).astype(BF16))
            ec.append(pair_cols(ecum, ha))
        ecum_g = jnp.concatenate(ec, axis=1)
        yield
        xs_b = xs.astype(BF16)
        pieces = []
        for pr in range(hpg // 2):
            blk = xs_b[:, pr * 2 * P:(pr + 1) * 2 * P]
            zero = jnp.zeros_like(blk)
            rhs = jnp.concatenate([jnp.where(lo_half, blk, zero), jnp.where(lo_half, zero, blk)], axis=0)
            pieces.append(_dot(lhs[pr], rhs))
        st = st_ref[g]
        cross = _dot(cm_g, st.astype(BF16)) * ecum_g
        st_ref[g] = st * ecum_g[L - 1:L, :] + _dot_tn(bm_g, jnp.concatenate(xw, axis=1))
        y = jnp.concatenate(pieces, axis=1) + cross + dsk_ref[:, g * gw:(g + 1) * gw] * xs
        y = y * z_ref[r0:r0 + L, g * gw:(g + 1) * gw].astype(F32)
        o_ref[r0:r0 + L, g * gw:(g + 1) * gw] = (_rms(y) * ng_ref[:, g * gw:(g + 1) * gw]).astype(o_ref.dtype)

    _interleave([group_item(c, g) for c in range(nchunks) for g in range(G)], SSD_STARTS)


SSD_STARTS = 1
SSD_CHUNKS_PER_STEP = 4
SSD_HIST_ROWS = 16


def _ssd(y, dt, conv_w, conv_b, a_log_row, d_row, norm_g, bsz, seq):
    t = y.shape[0]
    L = SSD_CHUNK * SSD_CHUNKS_PER_STEP
    ns = seq // L
    dinner = norm_g.shape[1]
    cch = conv_w.shape[1]
    gw = dinner // SSD_GROUPS
    const = lambda b, i: (0, 0)
    assert cch % dinner == 0
    return pl.pallas_call(
        _ssd_kernel,
        out_shape=jax.ShapeDtypeStruct((t, dinner), BF16),
        grid=(bsz, ns),
        in_specs=[pl.BlockSpec((L, dinner), lambda b, i: (b * ns + i, cch // dinner)),
                  pl.BlockSpec((L, cch), lambda b, i: (b * ns + i, 0)),
                  pl.BlockSpec((L, LANES), lambda b, i: (b * ns + i, 0)),
                  pl.BlockSpec((SSD_CONV, cch), const),
                  pl.BlockSpec((1, cch), const),
                  pl.BlockSpec((1, LANES), const),
                  pl.BlockSpec((1, dinner), const),
                  pl.BlockSpec((1, dinner), const)],
        out_specs=pl.BlockSpec((L, dinner), lambda b, i: (b * ns + i, 0)),
        scratch_shapes=[pltpu.VMEM((SSD_HIST_ROWS, cch), BF16),
                        pltpu.VMEM((SSD_GROUPS, SSD_STATE, gw), F32)],
        compiler_params=_cparams(("arbitrary", "arbitrary")),
        name="ssd_mixer",
    )(y, y, dt, conv_w, conv_b, a_log_row, d_row, norm_g)


OUT_SLAB = 256


def _route_choices(hb, rw, rb):
    tm = hb.shape[0]
    ne = rw.shape[0]
    epg = ne // N_GROUPS
    logits = _dot_nt(rw, hb)
    s = jax.nn.sigmoid(logits)
    sb = s + jnp.concatenate([rb] * (tm // LANES), axis=1)
    G = N_GROUPS
    v = [sb[j * G:(j + 1) * G, :] for j in range(epg)]
    m01, n01 = jnp.maximum(v[0], v[1]), jnp.minimum(v[0], v[1])
    m23, n23 = jnp.maximum(v[2], v[3]), jnp.minimum(v[2], v[3])
    top1 = jnp.maximum(m01, m23)
    top2 = jnp.maximum(jnp.minimum(m01, m23), jnp.maximum(n01, n23))
    gscore = top1 + top2
    gi = _iota((G, tm), 0)
    gmax = jnp.max(gscore, axis=0, keepdims=True)
    gidx = jnp.min(jnp.where(gscore == gmax, gi, G), axis=0, keepdims=True)
    sel = gi == gidx
    vb = [jnp.sum(jnp.where(sel, v[j], 0.0), axis=0, keepdims=True) for j in range(epg)]
    vs = [jnp.sum(jnp.where(sel, s[j * G:(j + 1) * G, :], 0.0), axis=0, keepdims=True) for j in range(epg)]
    zero = jnp.zeros_like(vb[0])
    e1 = zero
    e2 = zero
    w1 = zero
    w2 = zero
    for j in range(epg):
        rank = zero
        for i in range(epg):
            if i == j:
                continue
            ahead = (vb[i] >= vb[j]) if i < j else (vb[i] > vb[j])
            rank = rank + jnp.where(ahead, 1.0, 0.0)
        first = rank == 0.0
        second = rank == 1.0
        e1 = e1 + jnp.where(first, float(j), 0.0)
        e2 = e2 + jnp.where(second, float(j), 0.0)
        w1 = w1 + jnp.where(first, vs[j], 0.0)
        w2 = w2 + jnp.where(second, vs[j], 0.0)
    gf = gidx.astype(F32) * float(epg)
    wsum = w1 + w2
    return jnp.concatenate([e1 + gf, e2 + gf, w1 / wsum, w2 / wsum], axis=0)


def _outproj_kernel(o_ref, x_ref, mod_ref, gain_ref, w_ref, rw_ref, rb_ref, x1_ref, h2_ref, sel_ref, cnt_ref):
    mod = mod_ref[0]
    ne = rw_ref.shape[0]

    @pl.when(pl.program_id(0) == 0)
    def _():
        cnt_ref[...] = jnp.zeros_like(cnt_ref)

    def row_slab(r0):
        rows = pl.ds(r0, OUT_SLAB)
        mix = _dot(o_ref[rows, :], w_ref[...])
        yield
        x1 = x_ref[rows, :] + mod[2:3, :] * mix
        x1_ref[rows, :] = x1
        h2 = _norm_mod(x1, gain_ref[...], mod, 3, 4)
        h2_ref[rows, :] = _pack_pair(h2)
        choice = _route_choices(h2.astype(BF16), rw_ref[...], rb_ref[...])
        sel_ref[0, :, rows] = choice
        ei = _iota((ne, OUT_SLAB), 0).astype(F32)
        onehot = jnp.where((ei == choice[0:1, :]) | (ei == choice[1:2, :]), 1.0, 0.0)
        cnt_ref[...] = cnt_ref[...] + jnp.sum(onehot, axis=1, keepdims=True)

    _interleave([row_slab(r0) for r0 in range(0, o_ref.shape[0], OUT_SLAB)], 1)


def _outproj(o, x2d, mod_l, gain2, w_out, rw_t, rb_b, tm, seq):
    t, d = x2d.shape
    kin = o.shape[1]
    ne = rw_t.shape[0]
    nt = t // tm
    tiles_per_b = seq // tm
    const = lambda i: (0, 0)
    return pl.pallas_call(
        _outproj_kernel,
        out_shape=(jax.ShapeDtypeStruct((t, d), F32), jax.ShapeDtypeStruct((t, _packed_width(d)), PACKED),
                   jax.ShapeDtypeStruct((nt, 4, tm), F32), jax.ShapeDtypeStruct((ne, LANES), F32)),
        grid=(nt,),
        in_specs=[pl.BlockSpec((tm, kin), lambda i: (i, 0)),
                  pl.BlockSpec((tm, d), lambda i: (i, 0)),
                  pl.BlockSpec((1, 6, d), lambda i: (i // tiles_per_b, 0, 0)),
                  pl.BlockSpec((1, d), const),
                  pl.BlockSpec((kin, d), const),
                  pl.BlockSpec((ne, d), const),
                  pl.BlockSpec((ne, LANES), const)],
        out_specs=(pl.BlockSpec((tm, d), lambda i: (i, 0)), pl.BlockSpec((tm, _packed_width(d)), lambda i: (i, 0)),
                   pl.BlockSpec((1, 4, tm), lambda i: (i, 0, 0)), pl.BlockSpec((ne, LANES), const)),
        compiler_params=_cparams(("arbitrary",)),
        name="out_proj",
    )(o, x2d, mod_l, gain2, w_out, rw_t, rb_b)


def _route_kernel(sel_ref, tot_ref, dest_ref, gate_ref, blk_ref, cnt_ref, pst_ref, *, nbpad):
    step = pl.program_id(0)
    tm = sel_ref.shape[2]
    ne = tot_ref.shape[0]
    ei = _iota((ne, tm), 0).astype(F32)

    @pl.when(step == 0)
    def _():
        nblk = jnp.floor((tot_ref[...] + (MOE_BLOCK - 1.0)) * (1.0 / MOE_BLOCK))
        hi = jnp.floor(nblk * (1.0 / 16.0))
        lo = nblk - hi * 16.0
        er = _iota((ne, ne), 0)
        ec = _iota((ne, ne), 1)
        lower = (ec < er).astype(BF16)
        pst = _dot(lower, hi.astype(BF16)) * 16.0 + _dot(lower, lo.astype(BF16))
        pst_ref[...] = pst
        pend = jnp.concatenate([pst + nblk] * (nbpad // LANES), axis=1)
        bidx = _iota((ne, nbpad), 1).astype(F32)
        be = jnp.sum(jnp.where(pend <= bidx, 1.0, 0.0), axis=0, keepdims=True)
        total = jnp.max(pend, axis=0, keepdims=True)
        blk_ref[0:1, :] = jnp.minimum(be, ne - 1.0).astype(I32)
        blk_ref[1:2, :] = total.astype(I32)
        cnt_ref[...] = jnp.zeros_like(cnt_ref)

    choice = sel_ref[0]
    oh1 = ei == choice[0:1, :]
    oh2 = ei == choice[1:2, :]
    onehot = jnp.where(oh1 | oh2, 1.0, 0.0)
    upper = (_iota((tm, tm), 0) < _iota((tm, tm), 1)).astype(BF16)
    prefix = _dot(onehot.astype(BF16), upper)
    cnt = cnt_ref[...]
    base = jnp.concatenate([cnt] * (tm // LANES), axis=1) + prefix
    pst = jnp.concatenate([pst_ref[...]] * (tm // LANES), axis=1) * float(MOE_BLOCK)
    slot = base + pst
    d1 = jnp.sum(jnp.where(oh1, slot, 0.0), axis=0, keepdims=True)
    d2 = jnp.sum(jnp.where(oh2, slot, 0.0), axis=0, keepdims=True)
    cnt_ref[...] = cnt + jnp.sum(onehot, axis=1, keepdims=True)
    dest_ref[0, 0:1, :] = d1.astype(I32)
    dest_ref[0, 1:2, :] = d2.astype(I32)
    gate_ref[0] = choice[2:4, :]


def _route(sel, totals, nbpad):
    nt, _, tm = sel.shape
    ne = totals.shape[0]
    return pl.pallas_call(
        functools.partial(_route_kernel, nbpad=nbpad),
        out_shape=(jax.ShapeDtypeStruct((nt, 2, tm), I32), jax.ShapeDtypeStruct((nt, 2, tm), F32),
                   jax.ShapeDtypeStruct((2, nbpad), I32)),
        grid=(nt,),
        in_specs=[pl.BlockSpec((1, 4, tm), lambda i: (i, 0, 0)),
                  pl.BlockSpec((ne, LANES), lambda i: (0, 0))],
        out_specs=(pl.BlockSpec((1, 2, tm), lambda i: (i, 0, 0)),
                   pl.BlockSpec((1, 2, tm), lambda i: (i, 0, 0)),
                   pl.BlockSpec((2, nbpad), lambda i: (0, 0))),
        scratch_shapes=[pltpu.VMEM((ne, LANES), F32), pltpu.VMEM((ne, LANES), F32)],
        compiler_params=_cparams(("arbitrary",)),
        name="router_slots",
    )(sel, totals)


SC_CORES, SC_SUBCORES = 2, 16
SC_ROWS = 64


def _sc_mesh():
    return plsc.VectorSubcoreMesh(core_axis_name="c", subcore_axis_name="s",
                                  num_cores=SC_CORES, num_subcores=SC_SUBCORES)


def _sc_worker_base(rows_per_worker):
    return (lax.axis_index("s") * SC_CORES + lax.axis_index("c")) * rows_per_worker


def _sc_scatter_rows(src, idx, n_out):
    t, d = src.shape
    assert idx.shape[0] == TOP_K * t
    per_w = t // (SC_CORES * SC_SUBCORES)
    assert per_w % (2 * SC_ROWS) == 0
    rows = pltpu.VMEM((SC_ROWS, d), src.dtype)
    ids = pltpu.VMEM((SC_ROWS,), I32)

    @functools.partial(
        pl.kernel, mesh=_sc_mesh(), out_type=jax.ShapeDtypeStruct((n_out, d), src.dtype),
        scratch_types=[rows, rows, ids, ids, ids, ids] + [pltpu.SemaphoreType.DMA] * 4,
        compiler_params=pltpu.CompilerParams(use_tc_tiling_on_sc=True), name="moe_dispatch_sc")
    def k(src_hbm, idx_hbm, out_hbm, rows_a, rows_b, ia0, ia1, ib0, ib1, s0, s1, s2, s3):
        base = _sc_worker_base(per_w)

        @pl.loop(0, per_w // (2 * SC_ROWS))
        def _(j):
            off_a = pl.multiple_of(base + 2 * j * SC_ROWS, SC_ROWS)
            off_b = pl.multiple_of(off_a + SC_ROWS, SC_ROWS)
            load_a = pltpu.async_copy(src_hbm.at[pl.ds(off_a, SC_ROWS)], rows_a, s0)
            load_b = pltpu.async_copy(src_hbm.at[pl.ds(off_b, SC_ROWS)], rows_b, s1)
            pltpu.sync_copy(idx_hbm.at[pl.ds(off_a, SC_ROWS)], ia0)
            pltpu.sync_copy(idx_hbm.at[pl.ds(t + off_a, SC_ROWS)], ia1)
            pltpu.sync_copy(idx_hbm.at[pl.ds(off_b, SC_ROWS)], ib0)
            pltpu.sync_copy(idx_hbm.at[pl.ds(t + off_b, SC_ROWS)], ib1)
            load_a.wait()
            put_a0 = pltpu.async_copy(rows_a, out_hbm.at[ia0], s0)
            put_a1 = pltpu.async_copy(rows_a, out_hbm.at[ia1], s2)
            load_b.wait()
            put_b0 = pltpu.async_copy(rows_b, out_hbm.at[ib0], s1)
            put_b1 = pltpu.async_copy(rows_b, out_hbm.at[ib1], s3)
            put_a0.wait()
            put_a1.wait()
            put_b0.wait()
            put_b1.wait()

    return k(src, idx)


def _sc_gather_rows(table, idx):
    _, d = table.shape
    b = idx.shape[0]
    per_w = b // (SC_CORES * SC_SUBCORES)
    assert per_w % (2 * SC_ROWS) == 0
    rows = pltpu.VMEM((SC_ROWS, d), table.dtype)
    ids = pltpu.VMEM((SC_ROWS,), I32)

    @functools.partial(
        pl.kernel, mesh=_sc_mesh(), out_type=jax.ShapeDtypeStruct((b, d), table.dtype),
        scratch_types=[rows, rows, ids, ids] + [pltpu.SemaphoreType.DMA] * 2,
        compiler_params=pltpu.CompilerParams(use_tc_tiling_on_sc=True), name="moe_gather_sc")
    def k(table_hbm, idx_hbm, out_hbm, rows_a, rows_b, ia, ib, s0, s1):
        base = _sc_worker_base(per_w)

        @pl.loop(0, per_w // (2 * SC_ROWS))
        def _(j):
            off_a = pl.multiple_of(base + 2 * j * SC_ROWS, SC_ROWS)
            off_b = pl.multiple_of(off_a + SC_ROWS, SC_ROWS)
            pltpu.sync_copy(idx_hbm.at[pl.ds(off_a, SC_ROWS)], ia)
            pltpu.sync_copy(idx_hbm.at[pl.ds(off_b, SC_ROWS)], ib)
            get_a = pltpu.async_copy(table_hbm.at[ia], rows_a, s0)
            get_b = pltpu.async_copy(table_hbm.at[ib], rows_b, s1)
            get_a.wait()
            put_a = pltpu.async_copy(rows_a, out_hbm.at[pl.ds(off_a, SC_ROWS)], s0)
            get_b.wait()
            put_b = pltpu.async_copy(rows_b, out_hbm.at[pl.ds(off_b, SC_ROWS)], s1)
            put_a.wait()
            put_b.wait()

    return k(table, idx)


def _expert_kernel(blk_ref, nused_ref, first_ref, slot_ref, nxt_ref, xs_ref, wg_hbm, wu_hbm, wd_hbm, y_ref,
                   wg_f, wu_f, wd_f, wgb, wub, wdb, sem, *, layer):
    b = pl.program_id(0)

    def fetch(e, s):
        return [pltpu.make_async_copy(w_hbm.at[layer, e], stage.at[s], sem.at[s, k])
                for k, (w_hbm, stage) in enumerate(((wg_hbm, wg_f), (wu_hbm, wu_f), (wd_hbm, wd_f)))]

    @pl.when(b == 0)
    def _():
        for cp in fetch(blk_ref[0], 0):
            cp.start()

    @pl.when(first_ref[b] == 1)
    def _():
        s = slot_ref[b]
        for cp in fetch(blk_ref[b], s):
            cp.wait()

        @pl.when(nxt_ref[b] >= 0)
        def _():
            for cp in fetch(nxt_ref[b], 1 - s):
                cp.start()

        wgb[...] = wg_f[s].astype(BF16)
        wub[...] = wu_f[s].astype(BF16)
        wdb[...] = wd_f[s].astype(BF16)

    @pl.when(b < nused_ref[0])
    def _():
        xb = _unpack_pair(xs_ref[...]).astype(BF16)
        g = _dot(xb, wgb[...])
        u = _dot(xb, wub[...])
        a = (_silu(g) * u).astype(BF16)
        y_ref[...] = _pack_pair(_dot(a, wdb[...]))

    @pl.when(b >= nused_ref[0])
    def _():
        y_ref[...] = jnp.zeros_like(y_ref)


def _experts(xs, blk_e, nused, w_gate, w_up, w_down, layer):
    np_rows, dp = xs.shape
    _, ne, d, ff = w_gate.shape
    nb = np_rows // MOE_BLOCK
    pos = jnp.arange(nb, dtype=I32)
    first = jnp.concatenate([jnp.ones((1,), I32), (blk_e[1:] != blk_e[:-1]).astype(I32)])
    slot = (jnp.cumsum(first) - 1) % 2
    later = lax.cummin(jnp.where(first == 1, pos, nb)[::-1])[::-1]
    nxt_pos = jnp.concatenate([later[1:], jnp.full((1,), nb, I32)])
    nxt_e = jnp.where(nxt_pos < nb, blk_e[jnp.minimum(nxt_pos, nb - 1)], -1).astype(I32)
    grid_spec = pltpu.PrefetchScalarGridSpec(
        num_scalar_prefetch=5,
        grid=(nb,),
        in_specs=[pl.BlockSpec((MOE_BLOCK, dp), lambda b, *_: (b, 0)),
                  pl.BlockSpec(memory_space=pl.ANY),
                  pl.BlockSpec(memory_space=pl.ANY),
                  pl.BlockSpec(memory_space=pl.ANY)],
        out_specs=pl.BlockSpec((MOE_BLOCK, dp), lambda b, *_: (b, 0)),
        scratch_shapes=[pltpu.VMEM((2, d, ff), F32), pltpu.VMEM((2, d, ff), F32), pltpu.VMEM((2, ff, d), F32),
                        pltpu.VMEM((d, ff), BF16), pltpu.VMEM((d, ff), BF16), pltpu.VMEM((ff, d), BF16),
                        pltpu.SemaphoreType.DMA((2, 3))],
    )
    return pl.pallas_call(
        functools.partial(_expert_kernel, layer=layer),
        out_shape=jax.ShapeDtypeStruct((np_rows, dp), PACKED),
        grid_spec=grid_spec,
        compiler_params=_cparams(("arbitrary",)),
        name="moe_experts",
    )(blk_e, nused, first, slot.astype(I32), nxt_e, xs, w_gate, w_up, w_down)


def _moe_residual(x1, y1_ref, y2_ref, gates, out_gate):
    tc, d = x1.shape
    acc = jnp.zeros((tc, d), F32)
    for k, y_ref in enumerate((y1_ref, y2_ref)):
        gcol = jnp.transpose(jnp.broadcast_to(gates[k:k + 1, :], (LANES, tc)))
        acc = acc + _unpack_pair(y_ref[...]) * jnp.concatenate([gcol] * (d // LANES), axis=1)
    return x1 + out_gate * acc


def _combine_kernel(x1_ref, y1_ref, y2_ref, gate_ref, mod_ref, fg_ref, o_ref, *, final):
    x2 = _moe_residual(x1_ref[...], y1_ref, y2_ref, gate_ref[0], mod_ref[0][5:6, :])
    if final:
        x2 = _rms(x2) * fg_ref[...]
    o_ref[...] = x2


def _combine(x1, y12, gates, mod_l, final_gain, tc, seq, final):
    t, d = x1.shape
    nt = t // tc
    tiles_per_b = seq // tc
    return pl.pallas_call(
        functools.partial(_combine_kernel, final=final),
        out_shape=jax.ShapeDtypeStruct((t, d), F32),
        grid=(nt,),
        in_specs=[pl.BlockSpec((tc, d), lambda i: (i, 0)),
                  pl.BlockSpec((tc, _packed_width(d)), lambda i: (i, 0)),
                  pl.BlockSpec((tc, _packed_width(d)), lambda i: (nt + i, 0)),
                  pl.BlockSpec((1, TOP_K, tc), lambda i: (i, 0, 0)),
                  pl.BlockSpec((1, 6, d), lambda i: (i // tiles_per_b, 0, 0)),
                  pl.BlockSpec((1, d), lambda i: (0, 0))],
        out_specs=pl.BlockSpec((tc, d), lambda i: (i, 0)),
        compiler_params=_cparams(("arbitrary",)),
        name="moe_combine",
    )(x1, y12, y12, gates, mod_l, final_gain)


ROUTE_TILE = 512


def _moe(h2, sel, totals, w_gate, w_up, w_down, layer):
    t = h2.shape[0]
    ne = w_gate.shape[1]
    nb = (t * TOP_K + ne * (MOE_BLOCK - 1) + MOE_BLOCK - 1) // MOE_BLOCK
    nbpad = -(-nb // LANES) * LANES
    dest, gates, blk = _route(sel, totals, nbpad)
    slot = jnp.transpose(dest, (1, 0, 2)).reshape(TOP_K * t)
    xs = _sc_scatter_rows(h2, slot, nb * MOE_BLOCK)
    ys = _experts(xs, blk[0, :nb], blk[1, :1], w_gate, w_up, w_down, layer)
    return _sc_gather_rows(ys, slot), gates


def kernel(x, c, positions, mod_w, mod_b, norm_mix, norm_ffn, ev_w_in, ev_ret_norm, ev_gla_wa2, ev_gla_ba,
           ev_gla_norm, ev_w_out, od_w_in, od_conv_w, od_conv_b, od_dt_bias, od_a_log, od_d, od_norm, od_w_out,
           router_w, router_b, exp_w_gate, exp_w_up, exp_w_down, final_norm):
    bsz, seq, d = x.shape
    depth = mod_w.shape[0]
    t = bsz * seq
    ne = router_w.shape[1]
    epg = ne // N_GROUPS
    mod = _modulation(c, mod_w, mod_b)
    x2d = x.reshape(t, d)

    perm = np.array([g * epg + j for j in range(epg) for g in range(N_GROUPS)])
    rw_t = router_w.T[perm].astype(BF16)
    rb_b = jnp.broadcast_to(router_b.astype(F32)[perm][:, None], (ne, LANES))
    final_gain = final_norm.reshape(1, d)

    half = RET_DK // 2
    inv = ROPE_BASE ** (-jnp.arange(half, dtype=F32) / half)
    inv2 = jnp.concatenate([inv, inv]).reshape(1, LANES)
    pos_col = positions.reshape(t, 1)

    pending = None
    for layer in range(depth):
        i = layer // 2
        mod_l = mod[layer]
        gain1 = norm_mix[layer].reshape(1, d)
        gain2 = norm_ffn[layer].reshape(1, d)
        if layer % 2 == 0:
            if pending is not None:
                x2d = _combine(*pending, final_gain, ROUTE_TILE, seq, final=False)
            w_in = ev_w_in[i]
            nmain = w_in.shape[1] - GLA_RANK
            w_main = w_in[:, :nmain].astype(BF16)
            w_ga = jnp.pad(w_in[:, nmain:], ((0, 0), (0, LANES - GLA_RANK))).astype(BF16)
            wa2 = jnp.pad(ev_gla_wa2[i], ((0, LANES - GLA_RANK), (0, 0))).astype(BF16)
            y, glog = _even_proj(x2d, mod_l, gain1, pos_col, inv2, w_main, w_ga, wa2,
                                 ev_gla_ba[i].reshape(1, -1), PROJ_TILE, seq)
            o = _ret_gla(y, glog, ev_ret_norm[i].reshape(1, -1), ev_gla_norm[i].reshape(1, -1), bsz, seq, 512)
            w_out = ev_w_out[i].astype(BF16)
        else:
            assert pending is not None
            w_in = od_w_in[i]
            nheads = od_a_log.shape[1]
            nmain = w_in.shape[1] - nheads
            dinner = od_norm.shape[1]
            gw, ns_ = dinner // SSD_GROUPS, SSD_STATE
            bc0 = dinner + SSD_GROUPS * ns_

            def slabs(a, off=0):
                return [a[..., off + lo:off + lo + n] for g in range(SSD_GROUPS)
                        for lo, n in ((g * gw, gw), (dinner + g * ns_, ns_), (bc0 + g * ns_, ns_))]

            conv_w = jnp.concatenate(slabs(od_conv_w[i]), axis=-1)
            conv_b = jnp.concatenate(slabs(od_conv_b[i]), axis=-1).reshape(1, -1)
            w_main = jnp.concatenate(slabs(w_in, dinner) + [w_in[:, :dinner]], axis=1).astype(BF16)
            w_dt = jnp.pad(w_in[:, nmain:], ((0, 0), (0, LANES - nheads))).astype(BF16)
            dtb = jnp.pad(od_dt_bias[i], (0, LANES - nheads)).reshape(1, LANES)
            y, dt, x2d = _odd_proj(*pending, mod_l, gain1, w_main, w_dt, dtb, od_conv_w.shape[2], ROUTE_TILE, seq)
            alog = jnp.pad(od_a_log[i], (0, LANES - nheads)).reshape(1, LANES)
            d_row = jnp.repeat(od_d[i], SSD_HEADDIM).reshape(1, dinner)
            o = _ssd(y, dt, conv_w, conv_b, alog, d_row, od_norm[i].reshape(1, dinner), bsz, seq)
            w_out = od_w_out[i].astype(BF16)
        x1, h2, sel, totals = _outproj(o, x2d, mod_l, gain2, w_out, rw_t, rb_b, ROUTE_TILE, seq)
        y12, gates = _moe(h2, sel, totals, exp_w_gate, exp_w_up, exp_w_down, layer)
        pending = (x1, y12, gates, mod_l)
    out = _combine(*pending, final_gain, ROUTE_TILE, seq, final=True)
    return out.reshape(bsz, seq, d)
```

```python
import functools

import jax
import jax.numpy as jnp
import numpy as np
from jax import lax
from jax.experimental import pallas as pl
from jax.experimental.pallas import tpu as pltpu
from jax.experimental.pallas import tpu_sc as plsc

F32 = jnp.float32
BF16 = jnp.bfloat16
I32 = jnp.int32

RET_HEADS, RET_DK, RET_DV, RET_CHUNK = 4, 128, 256, 128
ROPE_BASE = 10000.0
GLA_HEADS, GLA_DK, GLA_DV, GLA_RANK, GLA_TAU, GLA_CHUNK = 4, 128, 256, 16, 16.0, 64
SSD_HEADDIM, SSD_GROUPS, SSD_STATE, SSD_CONV, SSD_CHUNK = 64, 8, 128, 4, 128
N_GROUPS, TOP_K, GROUP_SCORE_K = 8, 2, 2
MOE_BLOCK = 512
NORM_EPS = 1e-6

PROJ_TILE = 512
LANES = 128
VMEM_LIMIT = 56 * 1024 * 1024


def _cparams(sem, vmem=VMEM_LIMIT):
    return pltpu.CompilerParams(dimension_semantics=sem, vmem_limit_bytes=vmem)


def _dot(a, b):
    return jnp.dot(a, b, preferred_element_type=F32)


def _dot_nt(a, b):
    return lax.dot_general(a, b, (((1,), (1,)), ((), ())), preferred_element_type=F32)


def _dot_tn(a, b):
    return lax.dot_general(a, b, (((0,), (0,)), ((), ())), preferred_element_type=F32)


def _split3(x):
    a = x.astype(BF16)
    r = x - a.astype(F32)
    b = r.astype(BF16)
    c = (r - b.astype(F32)).astype(BF16)
    return a, b, c


def _dot_exact_rhs(m01, x):
    a, b, c = _split3(x)
    return _dot(m01, a) + _dot(m01, b) + _dot(m01, c)


def _dot_exact_lhs(x, m01):
    a, b, c = _split3(x)
    return _dot(a, m01) + _dot(b, m01) + _dot(c, m01)


def _dot_tn_exact(x, m01):
    a, b, c = _split3(x)
    return _dot_tn(a, m01) + _dot_tn(b, m01) + _dot_tn(c, m01)


def _silu(x):
    return x * jax.nn.sigmoid(x)


def _rms(x, eps=NORM_EPS):
    return x * lax.rsqrt(jnp.mean(x * x, axis=-1, keepdims=True) + eps)


def _iota(shape, dim):
    return lax.broadcasted_iota(I32, shape, dim)


def _interleave(items, starts_per_round):
    waiting, running = list(items), []
    while waiting or running:
        running += [waiting.pop(0) for _ in range(min(starts_per_round, len(waiting)))]
        for item in reversed(list(running)):
            if next(item, StopIteration) is StopIteration:
                running.remove(item)


PACKED = jnp.uint32
_HI16 = 0xFFFF0000


def _packed_width(n):
    return n // 2


def _pack_pair(x):
    n = x.shape[1] // 2
    lo = lax.bitcast_convert_type(x[:, :n].astype(BF16).astype(F32), jnp.uint32)
    hi = lax.bitcast_convert_type(x[:, n:].astype(BF16).astype(F32), jnp.uint32)
    return (hi & jnp.uint32(_HI16)) | (lo >> 16)


def _unpack_pair(w):
    lo = lax.bitcast_convert_type(w << 16, F32)
    hi = lax.bitcast_convert_type(w & jnp.uint32(_HI16), F32)
    return jnp.concatenate([lo, hi], axis=1)


def _mod_kernel(c_ref, w_ref, b_ref, o_ref):
    sc = _silu(c_ref[...])
    o_ref[0] = _dot(sc.astype(BF16), w_ref[0].astype(BF16)) + b_ref[0]


def _modulation(c, mod_w, mod_b):
    depth, d, d6 = mod_w.shape
    bsz = c.shape[0]
    nb = d6 // d
    out = pl.pallas_call(
        _mod_kernel,
        out_shape=jax.ShapeDtypeStruct((depth, bsz, d6), F32),
        grid=(depth, nb),
        in_specs=[pl.BlockSpec((bsz, d), lambda l, j: (0, 0)),
                  pl.BlockSpec((1, d, d), lambda l, j: (l, 0, j)),
                  pl.BlockSpec((1, 1, d), lambda l, j: (l, 0, j))],
        out_specs=pl.BlockSpec((1, bsz, d), lambda l, j: (l, 0, j)),
        compiler_params=_cparams(("arbitrary", "arbitrary")),
        name="adaln_mod",
    )(c, mod_w, mod_b.reshape(depth, 1, d6))
    return out.reshape(depth, bsz, nb, d)


def _norm_mod(x, gain, mod, shift_row, scale_row):
    y = _rms(x) * gain
    return y * (1.0 + mod[scale_row:scale_row + 1, :]) + mod[shift_row:shift_row + 1, :]


def _even_proj_kernel(x_ref, mod_ref, gain_ref, pos_ref, inv_ref, w_ref, wga_ref, wa2_ref, ba_ref,
                      y_ref, glog_ref, *, ncol):
    h = _norm_mod(x_ref[...], gain_ref[...], mod_ref[0], 0, 1)
    hb = h.astype(BF16)
    tm = hb.shape[0]
    hm = tm // 2
    half = LANES // 2
    lo = _iota((hm, LANES), 1) < half
    ang = jnp.where(lo, pos_ref[0:hm, :], pos_ref[hm:tm, :]).astype(F32) * inv_ref[...]
    c2, s2 = jnp.cos(ang), jnp.sin(ang)
    c2r, s2r = pltpu.roll(c2, half, 1), pltpu.roll(s2, half, 1)
    cos = jnp.concatenate([jnp.where(lo, c2, c2r), jnp.where(lo, c2r, c2)], axis=0)
    sin_a, sin_b = jnp.where(lo, s2, s2r), jnp.where(lo, s2r, s2)
    sin = jnp.concatenate([jnp.where(lo, -sin_a, sin_a), jnp.where(lo, -sin_b, sin_b)], axis=0)
    qk_scale = RET_DK ** -0.5
    rqk = RET_HEADS * RET_DK
    rv = RET_HEADS * RET_DV
    gq0 = 2 * rqk + 2 * rv
    width = w_ref.shape[1] // ncol
    for j in range(ncol):
        c0 = j * width
        acc = _dot(hb, w_ref[:, c0:c0 + width])
        for s in range(width // LANES):
            col = c0 + s * LANES
            blk = acc[:, s * LANES:(s + 1) * LANES]
            if col < 2 * rqk:
                blk = blk * cos + pltpu.roll(blk, LANES // 2, 1) * sin
                if col >= rqk:
                    blk = blk * qk_scale
            elif gq0 <= col < gq0 + GLA_HEADS * GLA_DK:
                blk = blk * (GLA_DK ** -0.5)
            y_ref[:, col:col + LANES] = blk.astype(y_ref.dtype)
    ga = _dot(hb, wga_ref[...])
    z = _dot(ga.astype(BF16), wa2_ref[...]) + ba_ref[...]
    logsig = jnp.minimum(z, 0.0) - jnp.log1p(jnp.exp(-jnp.abs(z)))
    glog_ref[...] = logsig * (1.0 / GLA_TAU)


def _even_proj(x2d, mod_l, gain, pos_col, inv2, w_main, w_ga, wa2, ba, tm, seq):
    t, d = x2d.shape
    n = w_main.shape[1]
    tiles_per_b = seq // tm
    gk = wa2.shape[1]
    const = lambda i: (0, 0)
    return pl.pallas_call(
        functools.partial(_even_proj_kernel, ncol=n // 512),
        out_shape=(jax.ShapeDtypeStruct((t, n), BF16), jax.ShapeDtypeStruct((t, gk), F32)),
        grid=(t // tm,),
        in_specs=[pl.BlockSpec((tm, d), lambda i: (i, 0)),
                  pl.BlockSpec((1, 6, d), lambda i: (i // tiles_per_b, 0, 0)),
                  pl.BlockSpec((1, d), const),
                  pl.BlockSpec((tm, 1), lambda i: (i, 0)),
                  pl.BlockSpec((1, LANES), const),
                  pl.BlockSpec((d, n), const, pipeline_mode=pl.Buffered(1)),
                  pl.BlockSpec((d, LANES), const),
                  pl.BlockSpec((LANES, gk), const),
                  pl.BlockSpec((1, gk), const)],
        out_specs=(pl.BlockSpec((tm, n), lambda i: (i, 0)), pl.BlockSpec((tm, gk), lambda i: (i, 0))),
        compiler_params=_cparams(("arbitrary",)),
        name="even_proj",
    )(x2d, mod_l, gain, pos_col, inv2, w_main, w_ga, wa2, ba)


RET_GLA_UNROLL = 4
RET_GLA_STARTS = 4


def _ret_gla_kernel(rq_ref, rk_ref, rv_ref, rg_ref, gq_ref, gk_ref, gv_ref, gg_ref, gl_ref,
                    rn_ref, gn_ref, dec_ref, zeta_ref, xi_ref, o_ref, rst_ref, gst_ref, *, chunk_decay):
    @pl.when(pl.program_id(1) == 0)
    def _():
        rst_ref[...] = jnp.zeros_like(rst_ref)
        gst_ref[...] = jnp.zeros_like(gst_ref)

    ls = rq_ref.shape[0]
    lr, lg = RET_CHUNK, GLA_CHUNK
    trow = _iota((lg, lg), 0)
    tcol = _iota((lg, lg), 1)
    causal = trow >= tcol
    tri = causal.astype(BF16)
    base = RET_HEADS * RET_DV

    def ret_item(r0, h):
        q = rq_ref[pl.ds(r0, lr), h * RET_DK:(h + 1) * RET_DK]
        k = rk_ref[pl.ds(r0, lr), h * RET_DK:(h + 1) * RET_DK]
        v = rv_ref[pl.ds(r0, lr), h * RET_DV:(h + 1) * RET_DV]
        s = _dot_nt(q, k) * dec_ref[h]
        qx = (q.astype(F32) * xi_ref[h]).astype(BF16)
        kz = (k.astype(F32) * zeta_ref[h]).astype(BF16)
        yield
        inner = _dot(s.astype(BF16), v)
        kv = _dot_tn(kz, v)
        yield
        gate = _silu(rg_ref[pl.ds(r0, lr), h * RET_DV:(h + 1) * RET_DV].astype(F32))
        yield
        st = rst_ref[h]
        cross = _dot(qx, st.astype(BF16))
        rst_ref[h] = st * chunk_decay[h] + kv
        o = _rms(inner + cross) * rn_ref[:, h * RET_DV:(h + 1) * RET_DV] * gate
        o_ref[pl.ds(r0, lr), h * RET_DV:(h + 1) * RET_DV] = o.astype(o_ref.dtype)

    def gla_item(r0, h):
        gl = gl_ref[pl.ds(r0, lg), h * GLA_DK:(h + 1) * GLA_DK]
        b = _dot_exact_rhs(tri, gl)
        yield
        q = gq_ref[pl.ds(r0, lg), h * GLA_DK:(h + 1) * GLA_DK].astype(F32)
        k = gk_ref[pl.ds(r0, lg), h * GLA_DK:(h + 1) * GLA_DK].astype(F32)
        bref = b[lg // 2:lg // 2 + 1, :]
        blast = b[lg - 1:lg, :]
        att = _dot_nt((q * jnp.exp(b - bref)).astype(BF16), (k * jnp.exp(bref - b)).astype(BF16))
        qb = (q * jnp.exp(b)).astype(BF16)
        kd = (k * jnp.exp(blast - b)).astype(BF16)
        dec = jnp.exp(blast)
        yield
        v = gv_ref[pl.ds(r0, lg), h * GLA_DV:(h + 1) * GLA_DV]
        inner = _dot(jnp.where(causal, att, 0.0).astype(BF16), v)
        kvt = _dot_tn(v, kd)
        gate = _silu(gg_ref[pl.ds(r0, lg), h * GLA_DV:(h + 1) * GLA_DV].astype(F32))
        yield
        st = gst_ref[h]
        cross = _dot_nt(qb, st.astype(BF16))
        gst_ref[h] = st * dec + kvt
        o = _rms(inner + cross) * gn_ref[:, h * GLA_DV:(h + 1) * GLA_DV] * gate
        o_ref[pl.ds(r0, lg), base + h * GLA_DV:base + (h + 1) * GLA_DV] = o.astype(o_ref.dtype)

    def step(c, carry):
        waiting = []
        for u in range(RET_GLA_UNROLL):
            r0 = pl.multiple_of((c * RET_GLA_UNROLL + u) * lr, lr)
            waiting += [ret_item(r0, h) for h in range(RET_HEADS)]
            for j in range(lr // lg):
                rj = pl.multiple_of(r0 + j * lg, lg)
                waiting += [gla_item(rj, h) for h in range(GLA_HEADS)]
        assert RET_HEADS % RET_GLA_STARTS == 0 and GLA_HEADS % RET_GLA_STARTS == 0
        _interleave(waiting, RET_GLA_STARTS)
        return carry

    lax.fori_loop(0, ls // (lr * RET_GLA_UNROLL), step, 0)


def _ret_gla(y, glog, ret_norm, gla_norm, bsz, seq, ls):
    t = y.shape[0]
    ns = seq // ls
    qw, vw = RET_HEADS * RET_DK, RET_HEADS * RET_DV
    L = RET_CHUNK
    log_gamma = jnp.log1p(-jnp.exp2(-5.0 - jnp.arange(RET_HEADS, dtype=F32)))
    idx = jnp.arange(L, dtype=F32)
    diff = idx[:, None] - idx[None, :]
    decay = jnp.where(diff >= 0, jnp.exp(log_gamma[:, None, None] * jnp.maximum(diff, 0.0)), 0.0)
    zeta = jnp.broadcast_to(jnp.exp(log_gamma[:, None] * (L - 1 - idx)[None, :])[:, :, None], (RET_HEADS, L, RET_DK))
    xi = jnp.broadcast_to(jnp.exp(log_gamma[:, None] * (idx + 1.0)[None, :])[:, :, None], (RET_HEADS, L, RET_DK))
    chunk_decay = tuple(float(np.exp(np.float32(np.log1p(-np.exp2(np.float32(-5.0 - i)))) * np.float32(L)))
                        for i in range(RET_HEADS))
    rowmap = lambda j: (lambda b, i: (b * ns + i, j))
    const = lambda b, i: (0, 0)
    const3 = lambda b, i: (0, 0, 0)
    return pl.pallas_call(
        functools.partial(_ret_gla_kernel, chunk_decay=chunk_decay),
        out_shape=jax.ShapeDtypeStruct((t, 2 * vw), BF16),
        grid=(bsz, ns),
        in_specs=[pl.BlockSpec((ls, qw), rowmap(0)),
                  pl.BlockSpec((ls, qw), rowmap(1)),
                  pl.BlockSpec((ls, vw), rowmap(1)),
                  pl.BlockSpec((ls, vw), rowmap(2)),
                  pl.BlockSpec((ls, qw), rowmap(6)),
                  pl.BlockSpec((ls, qw), rowmap(7)),
                  pl.BlockSpec((ls, vw), rowmap(4)),
                  pl.BlockSpec((ls, vw), rowmap(5)),
                  pl.BlockSpec((ls, qw), rowmap(0)),
                  pl.BlockSpec((1, vw), const),
                  pl.BlockSpec((1, vw), const),
                  pl.BlockSpec((RET_HEADS, L, L), const3),
                  pl.BlockSpec((RET_HEADS, L, RET_DK), const3),
                  pl.BlockSpec((RET_HEADS, L, RET_DK), const3)],
        out_specs=pl.BlockSpec((ls, 2 * vw), lambda b, i: (b * ns + i, 0)),
        scratch_shapes=[pltpu.VMEM((RET_HEADS, RET_DK, RET_DV), F32),
                        pltpu.VMEM((GLA_HEADS, GLA_DV, GLA_DK), F32)],
        compiler_params=_cparams(("arbitrary", "arbitrary")),
        name="ret_gla_mixer",
    )(y, y, y, y, y, y, y, y, glog, ret_norm, gla_norm, decay, zeta, xi)


def _odd_proj_kernel(x1_ref, y1_ref, y2_ref, gate_ref, modp_ref, mod_ref, gain_ref, w_ref, wdt_ref, dtb_ref,
                     y_ref, dt_ref, x_ref, *, ncol, cch):
    x = _moe_residual(x1_ref[...], y1_ref, y2_ref, gate_ref[0], modp_ref[0][5:6, :])
    x_ref[...] = x
    h = _norm_mod(x, gain_ref[...], mod_ref[0], 0, 1)
    hb = h.astype(BF16)
    width = w_ref.shape[1] // ncol
    for j in range(ncol):
        c0 = j * width
        acc = _dot(hb, w_ref[:, c0:c0 + width])
        if c0 >= cch:
            acc = _silu(acc)
        y_ref[:, c0:c0 + width] = acc.astype(y_ref.dtype)
    z = _dot(hb, wdt_ref[...]) + dtb_ref[...]
    dt_ref[...] = jnp.maximum(z, 0.0) + jnp.log1p(jnp.exp(-jnp.abs(z)))


def _odd_proj(x1, y12, gates, mod_prev, mod_l, gain, w_main, w_dt, dt_bias, cch, tm, seq):
    t, d = x1.shape
    n = w_main.shape[1]
    nt = t // tm
    tiles_per_b = seq // tm
    const = lambda i: (0, 0)
    batch = lambda i: (i // tiles_per_b, 0, 0)
    return pl.pallas_call(
        functools.partial(_odd_proj_kernel, ncol=n // 512, cch=cch),
        out_shape=(jax.ShapeDtypeStruct((t, n), BF16), jax.ShapeDtypeStruct((t, LANES), F32),
                   jax.ShapeDtypeStruct((t, d), F32)),
        grid=(nt,),
        in_specs=[pl.BlockSpec((tm, d), lambda i: (i, 0)),
                  pl.BlockSpec((tm, _packed_width(d)), lambda i: (i, 0)),
                  pl.BlockSpec((tm, _packed_width(d)), lambda i: (nt + i, 0)),
                  pl.BlockSpec((1, TOP_K, tm), lambda i: (i, 0, 0)),
                  pl.BlockSpec((1, 6, d), batch),
                  pl.BlockSpec((1, 6, d), batch),
                  pl.BlockSpec((1, d), const),
                  pl.BlockSpec((d, n), const, pipeline_mode=pl.Buffered(1)),
                  pl.BlockSpec((d, LANES), const),
                  pl.BlockSpec((1, LANES), const)],
        out_specs=(pl.BlockSpec((tm, n), lambda i: (i, 0)), pl.BlockSpec((tm, LANES), lambda i: (i, 0)),
                   pl.BlockSpec((tm, d), lambda i: (i, 0))),
        compiler_params=_cparams(("arbitrary",)),
        name="odd_proj",
    )(x1, y12, y12, gates, mod_prev, mod_l, gain, w_main, w_dt, dt_bias)


def _ssd_kernel(z_ref, xbc_ref, dt_ref, cw_ref, cb_ref, alog_ref, dsk_ref, ng_ref, o_ref, hist_ref, st_ref):
    @pl.when(pl.program_id(1) == 0)
    def _():
        hist_ref[...] = jnp.zeros_like(hist_ref)
        st_ref[...] = jnp.zeros_like(st_ref)

    L = SSD_CHUNK
    nchunks = xbc_ref.shape[0] // L
    P, N, G = SSD_HEADDIM, SSD_STATE, SSD_GROUPS
    dinner = z_ref.shape[1]
    gw = dinner // G
    hpg = gw // P
    slab = gw + 2 * N
    hrows = hist_ref.shape[0]

    trow = _iota((L, L), 0)
    tcol = _iota((L, L), 1)
    causal = trow >= tcol

    def discretise(c):
        dt = dt_ref[c * L:(c + 1) * L, :]
        da = dt * (-jnp.exp(alog_ref[...]))
        cum = _dot_exact_rhs(causal.astype(BF16), da)
        wend = dt * jnp.exp(cum[L - 1:L, :] - cum)
        return cum, jnp.transpose(cum), jnp.transpose(dt), jnp.exp(cum), wend

    heads = [discretise(c) for c in range(nchunks)]
    lane = _iota((L, LANES), 1)
    lo_half = lane < P

    def pair_cols(a, ha):
        return jnp.where(lo_half, jnp.broadcast_to(a[:, ha:ha + 1], (L, LANES)),
                         jnp.broadcast_to(a[:, ha + 1:ha + 2], (L, LANES)))

    srow = _iota((L, hrows + L), 0)
    scol = _iota((L, hrows + L), 1)
    shift = jnp.concatenate([(scol == srow + (hrows - s)).astype(BF16) for s in range(1, SSD_CONV)], axis=0)

    def group_item(c, g):
        cum, cum_t, dt_t, ecum, wend = heads[c]
        r0 = c * L
        c0 = g * slab
        xin = xbc_ref[r0:r0 + L, c0:c0 + slab]
        prev = hist_ref[:, c0:c0 + slab] if c == 0 else xbc_ref[r0 - hrows:r0, c0:c0 + slab]
        xcat = jnp.concatenate([prev, xin], axis=0)
        if c == nchunks - 1:
            hist_ref[:, c0:c0 + slab] = xin[L - hrows:L]
        sh = _dot(shift, xcat)
        conv = xin.astype(F32) * cw_ref[SSD_CONV - 1:SSD_CONV, c0:c0 + slab] + cb_ref[:, c0:c0 + slab]
        for s in range(1, SSD_CONV):
            conv = conv + sh[(s - 1) * L:s * L] * cw_ref[SSD_CONV - 1 - s:SSD_CONV - s, c0:c0 + slab]
        act = _silu(conv)
        xs = act[:, :gw]
        bm_g = act[:, gw:gw + N].astype(BF16)
        cm_g = act[:, gw + N:].astype(BF16)
        yield
        cb = _dot_nt(cm_g, bm_g)
        lhs, xw, ec = [], [], []
        for pr in range(hpg // 2):
            ha = g * hpg + 2 * pr
            ms = []
            for hh in (ha, ha + 1):
                seg = jnp.broadcast_to(cum[:, hh:hh + 1], (L, L)) - cum_t[hh:hh + 1, :]
                ms.append((cb * jnp.where(causal, jnp.exp(seg), 0.0) * dt_t[hh:hh + 1, :]).astype(BF16))
            lhs.append(jnp.concatenate(ms, axis=1))
            xw.append((xs[:, pr * 2 * P:(pr + 1) * 2 * P] * pair_cols(wend, ha)).astype(BF16))
            ec.append(pair_cols(ecum, ha))
        ecum_g = jnp.concatenate(ec, axis=1)
        yield
        xs_b = xs.astype(BF16)
        pieces = []
        for pr in range(hpg // 2):
            blk = xs_b[:, pr * 2 * P:(pr + 1) * 2 * P]
            zero = jnp.zeros_like(blk)
            rhs = jnp.concatenate([jnp.where(lo_half, blk, zero), jnp.where(lo_half, zero, blk)], axis=0)
            pieces.append(_dot(lhs[pr], rhs))
        st = st_ref[g]
        cross = _dot(cm_g, st.astype(BF16)) * ecum_g
        st_ref[g] = st * ecum_g[L - 1:L, :] + _dot_tn(bm_g, jnp.concatenate(xw, axis=1))
        y = jnp.concatenate(pieces, axis=1) + cross + dsk_ref[:, g * gw:(g + 1) * gw] * xs
        y = y * z_ref[r0:r0 + L, g * gw:(g + 1) * gw].astype(F32)
        o_ref[r0:r0 + L, g * gw:(g + 1) * gw] = (_rms(y) * ng_ref[:, g * gw:(g + 1) * gw]).astype(o_ref.dtype)

    _interleave([group_item(c, g) for c in range(nchunks) for g in range(G)], SSD_STARTS)


SSD_STARTS = 1
SSD_CHUNKS_PER_STEP = 4
SSD_HIST_ROWS = 16


def _ssd(y, dt, conv_w, conv_b, a_log_row, d_row, norm_g, bsz, seq):
    t = y.shape[0]
    L = SSD_CHUNK * SSD_CHUNKS_PER_STEP
    ns = seq // L
    dinner = norm_g.shape[1]
    cch = conv_w.shape[1]
    gw = dinner // SSD_GROUPS
    const = lambda b, i: (0, 0)
    assert cch % dinner == 0
    return pl.pallas_call(
        _ssd_kernel,
        out_shape=jax.ShapeDtypeStruct((t, dinner), BF16),
        grid=(bsz, ns),
        in_specs=[pl.BlockSpec((L, dinner), lambda b, i: (b * ns + i, cch // dinner)),
                  pl.BlockSpec((L, cch), lambda b, i: (b * ns + i, 0)),
                  pl.BlockSpec((L, LANES), lambda b, i: (b * ns + i, 0)),
                  pl.BlockSpec((SSD_CONV, cch), const),
                  pl.BlockSpec((1, cch), const),
                  pl.BlockSpec((1, LANES), const),
                  pl.BlockSpec((1, dinner), const),
                  pl.BlockSpec((1, dinner), const)],
        out_specs=pl.BlockSpec((L, dinner), lambda b, i: (b * ns + i, 0)),
        scratch_shapes=[pltpu.VMEM((SSD_HIST_ROWS, cch), BF16),
                        pltpu.VMEM((SSD_GROUPS, SSD_STATE, gw), F32)],
        compiler_params=_cparams(("arbitrary", "arbitrary")),
        name="ssd_mixer",
    )(y, y, dt, conv_w, conv_b, a_log_row, d_row, norm_g)


OUT_SLAB = 256


def _route_choices(hb, rw, rb):
    tm = hb.shape[0]
    ne = rw.shape[0]
    epg = ne // N_GROUPS
    logits = _dot_nt(rw, hb)
    s = jax.nn.sigmoid(logits)
    sb = s + jnp.concatenate([rb] * (tm // LANES), axis=1)
    G = N_GROUPS
    v = [sb[j * G:(j + 1) * G, :] for j in range(epg)]
    m01, n01 = jnp.maximum(v[0], v[1]), jnp.minimum(v[0], v[1])
    m23, n23 = jnp.maximum(v[2], v[3]), jnp.minimum(v[2], v[3])
    top1 = jnp.maximum(m01, m23)
    top2 = jnp.maximum(jnp.minimum(m01, m23), jnp.maximum(n01, n23))
    gscore = top1 + top2
    gi = _iota((G, tm), 0)
    gmax = jnp.max(gscore, axis=0, keepdims=True)
    gidx = jnp.min(jnp.where(gscore == gmax, gi, G), axis=0, keepdims=True)
    sel = gi == gidx
    vb = [jnp.sum(jnp.where(sel, v[j], 0.0), axis=0, keepdims=True) for j in range(epg)]
    vs = [jnp.sum(jnp.where(sel, s[j * G:(j + 1) * G, :], 0.0), axis=0, keepdims=True) for j in range(epg)]
    zero = jnp.zeros_like(vb[0])
    e1 = zero
    e2 = zero
    w1 = zero
    w2 = zero
    for j in range(epg):
        rank = zero
        for i in range(epg):
            if i == j:
                continue
            ahead = (vb[i] >= vb[j]) if i < j else (vb[i] > vb[j])
            rank = rank + jnp.where(ahead, 1.0, 0.0)
        first = rank == 0.0
        second = rank == 1.0
        e1 = e1 + jnp.where(first, float(j), 0.0)
        e2 = e2 + jnp.where(second, float(j), 0.0)
        w1 = w1 + jnp.where(first, vs[j], 0.0)
        w2 = w2 + jnp.where(second, vs[j], 0.0)
    gf = gidx.astype(F32) * float(epg)
    wsum = w1 + w2
    return jnp.concatenate([e1 + gf, e2 + gf, w1 / wsum, w2 / wsum], axis=0)


def _outproj_kernel(o_ref, x_ref, mod_ref, gain_ref, w_ref, rw_ref, rb_ref, x1_ref, h2_ref, sel_ref, cnt_ref):
    mod = mod_ref[0]
    ne = rw_ref.shape[0]

    @pl.when(pl.program_id(0) == 0)
    def _():
        cnt_ref[...] = jnp.zeros_like(cnt_ref)

    def row_slab(r0):
        rows = pl.ds(r0, OUT_SLAB)
        mix = _dot(o_ref[rows, :], w_ref[...])
        yield
        x1 = x_ref[rows, :] + mod[2:3, :] * mix
        x1_ref[rows, :] = x1
        h2 = _norm_mod(x1, gain_ref[...], mod, 3, 4)
        h2_ref[rows, :] = _pack_pair(h2)
        choice = _route_choices(h2.astype(BF16), rw_ref[...], rb_ref[...])
        sel_ref[0, :, rows] = choice
        ei = _iota((ne, OUT_SLAB), 0).astype(F32)
        onehot = jnp.where((ei == choice[0:1, :]) | (ei == choice[1:2, :]), 1.0, 0.0)
        cnt_ref[...] = cnt_ref[...] + jnp.sum(onehot, axis=1, keepdims=True)

    _interleave([row_slab(r0) for r0 in range(0, o_ref.shape[0], OUT_SLAB)], 1)


def _outproj(o, x2d, mod_l, gain2, w_out, rw_t, rb_b, tm, seq):
    t, d = x2d.shape
    kin = o.shape[1]
    ne = rw_t.shape[0]
    nt = t // tm
    tiles_per_b = seq // tm
    const = lambda i: (0, 0)
    return pl.pallas_call(
        _outproj_kernel,
        out_shape=(jax.ShapeDtypeStruct((t, d), F32), jax.ShapeDtypeStruct((t, _packed_width(d)), PACKED),
                   jax.ShapeDtypeStruct((nt, 4, tm), F32), jax.ShapeDtypeStruct((ne, LANES), F32)),
        grid=(nt,),
        in_specs=[pl.BlockSpec((tm, kin), lambda i: (i, 0)),
                  pl.BlockSpec((tm, d), lambda i: (i, 0)),
                  pl.BlockSpec((1, 6, d), lambda i: (i // tiles_per_b, 0, 0)),
                  pl.BlockSpec((1, d), const),
                  pl.BlockSpec((kin, d), const),
                  pl.BlockSpec((ne, d), const),
                  pl.BlockSpec((ne, LANES), const)],
        out_specs=(pl.BlockSpec((tm, d), lambda i: (i, 0)), pl.BlockSpec((tm, _packed_width(d)), lambda i: (i, 0)),
                   pl.BlockSpec((1, 4, tm), lambda i: (i, 0, 0)), pl.BlockSpec((ne, LANES), const)),
        compiler_params=_cparams(("arbitrary",)),
        name="out_proj",
    )(o, x2d, mod_l, gain2, w_out, rw_t, rb_b)


def _route_kernel(sel_ref, tot_ref, dest_ref, gate_ref, blk_ref, cnt_ref, pst_ref, *, nbpad):
    step = pl.program_id(0)
    tm = sel_ref.shape[2]
    ne = tot_ref.shape[0]
    ei = _iota((ne, tm), 0).astype(F32)

    @pl.when(step == 0)
    def _():
        nblk = jnp.floor((tot_ref[...] + (MOE_BLOCK - 1.0)) * (1.0 / MOE_BLOCK))
        hi = jnp.floor(nblk * (1.0 / 16.0))
        lo = nblk - hi * 16.0
        er = _iota((ne, ne), 0)
        ec = _iota((ne, ne), 1)
        lower = (ec < er).astype(BF16)
        pst = _dot(lower, hi.astype(BF16)) * 16.0 + _dot(lower, lo.astype(BF16))
        pst_ref[...] = pst
        pend = jnp.concatenate([pst + nblk] * (nbpad // LANES), axis=1)
        bidx = _iota((ne, nbpad), 1).astype(F32)
        be = jnp.sum(jnp.where(pend <= bidx, 1.0, 0.0), axis=0, keepdims=True)
        total = jnp.max(pend, axis=0, keepdims=True)
        blk_ref[0:1, :] = jnp.minimum(be, ne - 1.0).astype(I32)
        blk_ref[1:2, :] = total.astype(I32)
        cnt_ref[...] = jnp.zeros_like(cnt_ref)

    choice = sel_ref[0]
    oh1 = ei == choice[0:1, :]
    oh2 = ei == choice[1:2, :]
    onehot = jnp.where(oh1 | oh2, 1.0, 0.0)
    upper = (_iota((tm, tm), 0) < _iota((tm, tm), 1)).astype(BF16)
    prefix = _dot(onehot.astype(BF16), upper)
    cnt = cnt_ref[...]
    base = jnp.concatenate([cnt] * (tm // LANES), axis=1) + prefix
    pst = jnp.concatenate([pst_ref[...]] * (tm // LANES), axis=1) * float(MOE_BLOCK)
    slot = base + pst
    d1 = jnp.sum(jnp.where(oh1, slot, 0.0), axis=0, keepdims=True)
    d2 = jnp.sum(jnp.where(oh2, slot, 0.0), axis=0, keepdims=True)
    cnt_ref[...] = cnt + jnp.sum(onehot, axis=1, keepdims=True)
    dest_ref[0, 0:1, :] = d1.astype(I32)
    dest_ref[0, 1:2, :] = d2.astype(I32)
    gate_ref[0] = choice[2:4, :]


def _route(sel, totals, nbpad):
    nt, _, tm = sel.shape
    ne = totals.shape[0]
    return pl.pallas_call(
        functools.partial(_route_kernel, nbpad=nbpad),
        out_shape=(jax.ShapeDtypeStruct((nt, 2, tm), I32), jax.ShapeDtypeStruct((nt, 2, tm), F32),
                   jax.ShapeDtypeStruct((2, nbpad), I32)),
        grid=(nt,),
        in_specs=[pl.BlockSpec((1, 4, tm), lambda i: (i, 0, 0)),
                  pl.BlockSpec((ne, LANES), lambda i: (0, 0))],
        out_specs=(pl.BlockSpec((1, 2, tm), lambda i: (i, 0, 0)),
                   pl.BlockSpec((1, 2, tm), lambda i: (i, 0, 0)),
                   pl.BlockSpec((2, nbpad), lambda i: (0, 0))),
        scratch_shapes=[pltpu.VMEM((ne, LANES), F32), pltpu.VMEM((ne, LANES), F32)],
        compiler_params=_cparams(("arbitrary",)),
        name="router_slots",
    )(sel, totals)


SC_CORES, SC_SUBCORES = 2, 16
SC_ROWS = 64


def _sc_mesh():
    return plsc.VectorSubcoreMesh(core_axis_name="c", subcore_axis_name="s",
                                  num_cores=SC_CORES, num_subcores=SC_SUBCORES)


def _sc_worker_base(rows_per_worker):
    return (lax.axis_index("s") * SC_CORES + lax.axis_index("c")) * rows_per_worker


def _sc_scatter_rows(src, idx, n_out):
    t, d = src.shape
    assert idx.shape[0] == TOP_K * t
    per_w = t // (SC_CORES * SC_SUBCORES)
    assert per_w % (2 * SC_ROWS) == 0
    rows = pltpu.VMEM((SC_ROWS, d), src.dtype)
    ids = pltpu.VMEM((SC_ROWS,), I32)

    @functools.partial(
        pl.kernel, mesh=_sc_mesh(), out_type=jax.ShapeDtypeStruct((n_out, d), src.dtype),
        scratch_types=[rows, rows, ids, ids, ids, ids] + [pltpu.SemaphoreType.DMA] * 4,
        compiler_params=pltpu.CompilerParams(use_tc_tiling_on_sc=True), name="moe_dispatch_sc")
    def k(src_hbm, idx_hbm, out_hbm, rows_a, rows_b, ia0, ia1, ib0, ib1, s0, s1, s2, s3):
        base = _sc_worker_base(per_w)

        @pl.loop(0, per_w // (2 * SC_ROWS))
        def _(j):
            off_a = pl.multiple_of(base + 2 * j * SC_ROWS, SC_ROWS)
            off_b = pl.multiple_of(off_a + SC_ROWS, SC_ROWS)
            load_a = pltpu.async_copy(src_hbm.at[pl.ds(off_a, SC_ROWS)], rows_a, s0)
            load_b = pltpu.async_copy(src_hbm.at[pl.ds(off_b, SC_ROWS)], rows_b, s1)
            pltpu.sync_copy(idx_hbm.at[pl.ds(off_a, SC_ROWS)], ia0)
            pltpu.sync_copy(idx_hbm.at[pl.ds(t + off_a, SC_ROWS)], ia1)
            pltpu.sync_copy(idx_hbm.at[pl.ds(off_b, SC_ROWS)], ib0)
            pltpu.sync_copy(idx_hbm.at[pl.ds(t + off_b, SC_ROWS)], ib1)
            load_a.wait()
            put_a0 = pltpu.async_copy(rows_a, out_hbm.at[ia0], s0)
            put_a1 = pltpu.async_copy(rows_a, out_hbm.at[ia1], s2)
            load_b.wait()
            put_b0 = pltpu.async_copy(rows_b, out_hbm.at[ib0], s1)
            put_b1 = pltpu.async_copy(rows_b, out_hbm.at[ib1], s3)
            put_a0.wait()
            put_a1.wait()
            put_b0.wait()
            put_b1.wait()

    return k(src, idx)


def _sc_gather_rows(table, idx):
    _, d = table.shape
    b = idx.shape[0]
    per_w = b // (SC_CORES * SC_SUBCORES)
    assert per_w % (2 * SC_ROWS) == 0
    rows = pltpu.VMEM((SC_ROWS, d), table.dtype)
    ids = pltpu.VMEM((SC_ROWS,), I32)

    @functools.partial(
        pl.kernel, mesh=_sc_mesh(), out_type=jax.ShapeDtypeStruct((b, d), table.dtype),
        scratch_types=[rows, rows, ids, ids] + [pltpu.SemaphoreType.DMA] * 2,
        compiler_params=pltpu.CompilerParams(use_tc_tiling_on_sc=True), name="moe_gather_sc")
    def k(table_hbm, idx_hbm, out_hbm, rows_a, rows_b, ia, ib, s0, s1):
        base = _sc_worker_base(per_w)

        @pl.loop(0, per_w // (2 * SC_ROWS))
        def _(j):
            off_a = pl.multiple_of(base + 2 * j * SC_ROWS, SC_ROWS)
            off_b = pl.multiple_of(off_a + SC_ROWS, SC_ROWS)
            pltpu.sync_copy(idx_hbm.at[pl.ds(off_a, SC_ROWS)], ia)
            pltpu.sync_copy(idx_hbm.at[pl.ds(off_b, SC_ROWS)], ib)
            get_a = pltpu.async_copy(table_hbm.at[ia], rows_a, s0)
            get_b = pltpu.async_copy(table_hbm.at[ib], rows_b, s1)
            get_a.wait()
            put_a = pltpu.async_copy(rows_a, out_hbm.at[pl.ds(off_a, SC_ROWS)], s0)
            get_b.wait()
            put_b = pltpu.async_copy(rows_b, out_hbm.at[pl.ds(off_b, SC_ROWS)], s1)
            put_a.wait()
            put_b.wait()

    return k(table, idx)


def _expert_kernel(blk_ref, nused_ref, first_ref, slot_ref, nxt_ref, xs_ref, wg_hbm, wu_hbm, wd_hbm, y_ref,
                   wg_f, wu_f, wd_f, wgb, wub, wdb, sem, *, layer):
    b = pl.program_id(0)

    def fetch(e, s):
        return [pltpu.make_async_copy(w_hbm.at[layer, e], stage.at[s], sem.at[s, k])
                for k, (w_hbm, stage) in enumerate(((wg_hbm, wg_f), (wu_hbm, wu_f), (wd_hbm, wd_f)))]

    @pl.when(b == 0)
    def _():
        for cp in fetch(blk_ref[0], 0):
            cp.start()

    @pl.when(first_ref[b] == 1)
    def _():
        s = slot_ref[b]
        for cp in fetch(blk_ref[b], s):
            cp.wait()

        @pl.when(nxt_ref[b] >= 0)
        def _():
            for cp in fetch(nxt_ref[b], 1 - s):
                cp.start()

        wgb[...] = wg_f[s].astype(BF16)
        wub[...] = wu_f[s].astype(BF16)
        wdb[...] = wd_f[s].astype(BF16)

    @pl.when(b < nused_ref[0])
    def _():
        xb = _unpack_pair(xs_ref[...]).astype(BF16)
        g = _dot(xb, wgb[...])
        u = _dot(xb, wub[...])
        a = (_silu(g) * u).astype(BF16)
        y_ref[...] = _pack_pair(_dot(a, wdb[...]))


def _experts(xs, blk_e, nused, w_gate, w_up, w_down, layer):
    np_rows, dp = xs.shape
    _, ne, d, ff = w_gate.shape
    nb = np_rows // MOE_BLOCK
    pos = jnp.arange(nb, dtype=I32)
    first = jnp.concatenate([jnp.ones((1,), I32), (blk_e[1:] != blk_e[:-1]).astype(I32)])
    slot = (jnp.cumsum(first) - 1) % 2
    later = lax.cummin(jnp.where(first == 1, pos, nb)[::-1])[::-1]
    nxt_pos = jnp.concatenate([later[1:], jnp.full((1,), nb, I32)])
    nxt_e = jnp.where(nxt_pos < nb, blk_e[jnp.minimum(nxt_pos, nb - 1)], -1).astype(I32)
    grid_spec = pltpu.PrefetchScalarGridSpec(
        num_scalar_prefetch=5,
        grid=(nb,),
        in_specs=[pl.BlockSpec((MOE_BLOCK, dp), lambda b, be, nu, *_: (jnp.minimum(b, nu[0] - 1), 0)),
                  pl.BlockSpec(memory_space=pl.ANY),
                  pl.BlockSpec(memory_space=pl.ANY),
                  pl.BlockSpec(memory_space=pl.ANY)],
        out_specs=pl.BlockSpec((MOE_BLOCK, dp), lambda b, be, nu, *_: (jnp.minimum(b, nu[0] - 1), 0)),
        scratch_shapes=[pltpu.VMEM((2, d, ff), F32), pltpu.VMEM((2, d, ff), F32), pltpu.VMEM((2, ff, d), F32),
                        pltpu.VMEM((d, ff), BF16), pltpu.VMEM((d, ff), BF16), pltpu.VMEM((ff, d), BF16),
                        pltpu.SemaphoreType.DMA((2, 3))],
    )
    return pl.pallas_call(
        functools.partial(_expert_kernel, layer=layer),
        out_shape=jax.ShapeDtypeStruct((np_rows, dp), PACKED),
        grid_spec=grid_spec,
        compiler_params=_cparams(("arbitrary",)),
        name="moe_experts",
    )(blk_e, nused, first, slot.astype(I32), nxt_e, xs, w_gate, w_up, w_down)


def _moe_residual(x1, y1_ref, y2_ref, gates, out_gate):
    tc, d = x1.shape
    acc = jnp.zeros((tc, d), F32)
    for k, y_ref in enumerate((y1_ref, y2_ref)):
        gcol = jnp.transpose(jnp.broadcast_to(gates[k:k + 1, :], (LANES, tc)))
        acc = acc + _unpack_pair(y_ref[...]) * jnp.concatenate([gcol] * (d // LANES), axis=1)
    return x1 + out_gate * acc


def _combine_kernel(x1_ref, y1_ref, y2_ref, gate_ref, mod_ref, fg_ref, o_ref, *, final):
    x2 = _moe_residual(x1_ref[...], y1_ref, y2_ref, gate_ref[0], mod_ref[0][5:6, :])
    if final:
        x2 = _rms(x2) * fg_ref[...]
    o_ref[...] = x2


def _combine(x1, y12, gates, mod_l, final_gain, tc, seq, final):
    t, d = x1.shape
    nt = t // tc
    tiles_per_b = seq // tc
    return pl.pallas_call(
        functools.partial(_combine_kernel, final=final),
        out_shape=jax.ShapeDtypeStruct((t, d), F32),
        grid=(nt,),
        in_specs=[pl.BlockSpec((tc, d), lambda i: (i, 0)),
                  pl.BlockSpec((tc, _packed_width(d)), lambda i: (i, 0)),
                  pl.BlockSpec((tc, _packed_width(d)), lambda i: (nt + i, 0)),
                  pl.BlockSpec((1, TOP_K, tc), lambda i: (i, 0, 0)),
                  pl.BlockSpec((1, 6, d), lambda i: (i // tiles_per_b, 0, 0)),
                  pl.BlockSpec((1, d), lambda i: (0, 0))],
        out_specs=pl.BlockSpec((tc, d), lambda i: (i, 0)),
        compiler_params=_cparams(("arbitrary",)),
        name="moe_combine",
    )(x1, y12, y12, gates, mod_l, final_gain)


ROUTE_TILE = 512


def _moe(h2, sel, totals, w_gate, w_up, w_down, layer):
    t = h2.shape[0]
    ne = w_gate.shape[1]
    nb = (t * TOP_K + ne * (MOE_BLOCK - 1) + MOE_BLOCK - 1) // MOE_BLOCK
    nbpad = -(-nb // LANES) * LANES
    dest, gates, blk = _route(sel, totals, nbpad)
    slot = jnp.transpose(dest, (1, 0, 2)).reshape(TOP_K * t)
    xs = _sc_scatter_rows(h2, slot, nb * MOE_BLOCK)
    ys = _experts(xs, blk[0, :nb], blk[1, :1], w_gate, w_up, w_down, layer)
    return _sc_gather_rows(ys, slot), gates


def kernel(x, c, positions, mod_w, mod_b, norm_mix, norm_ffn, ev_w_in, ev_ret_norm, ev_gla_wa2, ev_gla_ba,
           ev_gla_norm, ev_w_out, od_w_in, od_conv_w, od_conv_b, od_dt_bias, od_a_log, od_d, od_norm, od_w_out,
           router_w, router_b, exp_w_gate, exp_w_up, exp_w_down, final_norm):
    bsz, seq, d = x.shape
    depth = mod_w.shape[0]
    t = bsz * seq
    ne = router_w.shape[1]
    epg = ne // N_GROUPS
    mod = _modulation(c, mod_w, mod_b)
    x2d = x.reshape(t, d)

    perm = np.array([g * epg + j for j in range(epg) for g in range(N_GROUPS)])
    rw_t = router_w.T[perm].astype(BF16)
    rb_b = jnp.broadcast_to(router_b.astype(F32)[perm][:, None], (ne, LANES))
    final_gain = final_norm.reshape(1, d)

    half = RET_DK // 2
    inv = ROPE_BASE ** (-jnp.arange(half, dtype=F32) / half)
    inv2 = jnp.concatenate([inv, inv]).reshape(1, LANES)
    pos_col = positions.reshape(t, 1)

    pending = None
    for layer in range(depth):
        i = layer // 2
        mod_l = mod[layer]
        gain1 = norm_mix[layer].reshape(1, d)
        gain2 = norm_ffn[layer].reshape(1, d)
        if layer % 2 == 0:
            if pending is not None:
                x2d = _combine(*pending, final_gain, ROUTE_TILE, seq, final=False)
            w_in = ev_w_in[i]
            nmain = w_in.shape[1] - GLA_RANK
            w_main = w_in[:, :nmain].astype(BF16)
            w_ga = jnp.pad(w_in[:, nmain:], ((0, 0), (0, LANES - GLA_RANK))).astype(BF16)
            wa2 = jnp.pad(ev_gla_wa2[i], ((0, LANES - GLA_RANK), (0, 0))).astype(BF16)
            y, glog = _even_proj(x2d, mod_l, gain1, pos_col, inv2, w_main, w_ga, wa2,
                                 ev_gla_ba[i].reshape(1, -1), PROJ_TILE, seq)
            o = _ret_gla(y, glog, ev_ret_norm[i].reshape(1, -1), ev_gla_norm[i].reshape(1, -1), bsz, seq, 512)
            w_out = ev_w_out[i].astype(BF16)
        else:
            assert pending is not None
            w_in = od_w_in[i]
            nheads = od_a_log.shape[1]
            nmain = w_in.shape[1] - nheads
            dinner = od_norm.shape[1]
            gw, ns_ = dinner // SSD_GROUPS, SSD_STATE
            bc0 = dinner + SSD_GROUPS * ns_

            def slabs(a, off=0):
                return [a[..., off + lo:off + lo + n] for g in range(SSD_GROUPS)
                        for lo, n in ((g * gw, gw), (dinner + g * ns_, ns_), (bc0 + g * ns_, ns_))]

            conv_w = jnp.concatenate(slabs(od_conv_w[i]), axis=-1)
            conv_b = jnp.concatenate(slabs(od_conv_b[i]), axis=-1).reshape(1, -1)
            w_main = jnp.concatenate(slabs(w_in, dinner) + [w_in[:, :dinner]], axis=1).astype(BF16)
            w_dt = jnp.pad(w_in[:, nmain:], ((0, 0), (0, LANES - nheads))).astype(BF16)
            dtb = jnp.pad(od_dt_bias[i], (0, LANES - nheads)).reshape(1, LANES)
            y, dt, x2d = _odd_proj(*pending, mod_l, gain1, w_main, w_dt, dtb, od_conv_w.shape[2], ROUTE_TILE, seq)
            alog = jnp.pad(od_a_log[i], (0, LANES - nheads)).reshape(1, LANES)
            d_row = jnp.repeat(od_d[i], SSD_HEADDIM).reshape(1, dinner)
            o = _ssd(y, dt, conv_w, conv_b, alog, d_row, od_norm[i].reshape(1, dinner), bsz, seq)
            w_out = od_w_out[i].astype(BF16)
        x1, h2, sel, totals = _outproj(o, x2d, mod_l, gain2, w_out, rw_t, rb_b, ROUTE_TILE, seq)
        y12, gates = _moe(h2, sel, totals, exp_w_gate, exp_w_up, exp_w_down, layer)
        pending = (x1, y12, gates, mod_l)
    out = _combine(*pending, final_gain, ROUTE_TILE, seq, final=True)
    return out.reshape(bsz, seq, d)
```
